```python
import jax, jax.numpy as jnp
from jax import lax
import numpy as np

D_MODEL = 1024
BATCH = 32
SEQ = 256
DEPTH = 2
DEC_BATCH = 4
DEC_SEQ = 4096
PAST_LEN = 512

GRID_W = 64
N_DIRS = 2
EPS = 1e-6
HEAD_DIM = 64
ATTN_WIDTH = D_MODEL // 2
ATTN_HEADS = ATTN_WIDTH // HEAD_DIM
ATTN_KV_HEADS = ATTN_HEADS // 4
ATTN_GROUPS = ATTN_HEADS // ATTN_KV_HEADS
KV_WIDTH = ATTN_KV_HEADS * HEAD_DIM
WINDOW = 128
ATTN_BLOCK = 128
ROPE_BASE = 10000.0
NEG_INF = -1e30
GLA_WIDTH = D_MODEL // 4
GLA_DK = 64
GLA_DV = 64
GLA_HEADS = GLA_WIDTH // GLA_DV
GLA_QK_WIDTH = GLA_HEADS * GLA_DK
GLA_GATE_RANK = 16
GLA_GATE_NORM = 16.0
GLA_CHUNK = 64
LRU_WIDTH = D_MODEL - ATTN_WIDTH - GLA_WIDTH
LRU_BLOCK_W = 64
LRU_BLOCKS = LRU_WIDTH // LRU_BLOCK_W
LRU_CONV = 4
LRU_C = 8.0
MIX_WIDTH = ATTN_WIDTH + GLA_WIDTH + LRU_WIDTH
D_FF = 4 * D_MODEL
IN_SIZES = (ATTN_WIDTH, KV_WIDTH, KV_WIDTH,
            GLA_QK_WIDTH, GLA_QK_WIDTH, GLA_WIDTH, N_DIRS * GLA_GATE_RANK, GLA_WIDTH,
            LRU_WIDTH, LRU_WIDTH)
IN_COLS = (ATTN_WIDTH + 2 * KV_WIDTH + 2 * GLA_QK_WIDTH + 2 * GLA_WIDTH
           + N_DIRS * GLA_GATE_RANK + 2 * LRU_WIDTH)

kernel_name = 'hybrid_prefix_flow_step'


def rmsnorm(x, g):
    xf = x.astype(jnp.float32)
    y = xf * lax.rsqrt(jnp.mean(xf * xf, axis=-1, keepdims=True) + EPS)
    return (y * g.astype(jnp.float32)).astype(x.dtype)


def split_cols(a, sizes):
    out, start = [], 0
    for s in sizes:
        out.append(a[..., start:start + s])
        start += s
    return out


def ada_mod(cond, w_mod, b_mod):
    mod = jax.nn.silu(cond) @ w_mod + b_mod
    mod = mod.reshape(-1, 1, 6 * D_MODEL)
    return jnp.split(mod, 6, axis=-1)


def axial_rope(x):
    T = x.shape[1]
    rows = T // GRID_W
    row = jnp.repeat(jnp.arange(rows), GRID_W).astype(jnp.float32)
    col = jnp.tile(jnp.arange(GRID_W), rows).astype(jnp.float32)
    nf = HEAD_DIM // 4
    inv = ROPE_BASE ** (-jnp.arange(nf, dtype=jnp.float32) / nf)
    ang = jnp.stack([row[:, None] * inv, col[:, None] * inv], axis=1)
    cos = jnp.cos(ang)[None, :, None]
    sin = jnp.sin(ang)[None, :, None]
    xr = x.astype(jnp.float32).reshape(x.shape[:3] + (2, 2, nf))
    x1, x2 = xr[..., 0, :], xr[..., 1, :]
    out = jnp.stack([x1 * cos - x2 * sin, x2 * cos + x1 * sin], axis=-2)
    return out.reshape(x.shape).astype(x.dtype)


def softmax_with_sink(scores, sink):
    m = sink
    for s in scores:
        m = jnp.maximum(m, jnp.max(s, axis=-1, keepdims=True))
    es = [jnp.exp(s - m) for s in scores]
    denom = jnp.exp(sink - m)
    for e in es:
        denom = denom + jnp.sum(e, axis=-1, keepdims=True)
    inv = 1.0 / denom
    return [e * inv for e in es]


def attn_context(q, k, v, sink):
    B, L = q.shape[0], q.shape[1]
    qg = q.reshape(B, L, ATTN_KV_HEADS, ATTN_GROUPS, HEAD_DIM)
    s = jnp.einsum('blkgd,bmkd->bkglm', qg, k, preferred_element_type=jnp.float32) * (HEAD_DIM ** -0.5)
    s_sink = jnp.broadcast_to(sink.astype(jnp.float32).reshape(1, ATTN_KV_HEADS, ATTN_GROUPS, 1, 1), s.shape[:-1] + (1,))
    (p,) = softmax_with_sink([s], s_sink)
    o = jnp.einsum('bkglm,bmkd->blkgd', p.astype(v.dtype), v)
    return o.reshape(B, L, ATTN_WIDTH)


def attn_latent(q, k, v, ck, cv, sink):
    B, T = q.shape[0], q.shape[1]
    nb = T // ATTN_BLOCK
    qb = q.reshape(B, nb, ATTN_BLOCK, ATTN_KV_HEADS, ATTN_GROUPS, HEAD_DIM)

    def band(a):
        ap = jnp.pad(a, ((0, 0), (ATTN_BLOCK, ATTN_BLOCK), (0, 0), (0, 0)))
        ap = ap.reshape(B, nb + 2, ATTN_BLOCK, ATTN_KV_HEADS, HEAD_DIM)
        return jnp.concatenate([ap[:, :-2], ap[:, 1:-1], ap[:, 2:]], axis=2)

    kw, vw = band(k), band(v)
    blk = jnp.arange(nb)[:, None, None] * ATTN_BLOCK
    qpos = blk + jnp.arange(ATTN_BLOCK)[None, :, None]
    kpos = blk - ATTN_BLOCK + jnp.arange(3 * ATTN_BLOCK)[None, None, :]
    valid = (kpos >= 0) & (kpos < T) & (jnp.abs(qpos - kpos) <= WINDOW)
    scale = HEAD_DIM ** -0.5
    s_loc = jnp.einsum('bnikgd,bnjkd->bnkgij', qb, kw, preferred_element_type=jnp.float32) * scale
    s_loc = jnp.where(valid[None, :, None, None], s_loc, NEG_INF)
    s_ctx = jnp.einsum('bnikgd,bmkd->bnkgim', qb, ck, preferred_element_type=jnp.float32) * scale
    s_sink = jnp.broadcast_to(sink.astype(jnp.float32).reshape(1, 1, ATTN_KV_HEADS, ATTN_GROUPS, 1, 1), s_loc.shape[:-1] + (1,))
    p_loc, p_ctx = softmax_with_sink([s_loc, s_ctx], s_sink)
    o = (jnp.einsum('bnkgij,bnjkd->bnikgd', p_loc.astype(v.dtype), vw)
         + jnp.einsum('bnkgim,bmkd->bnikgd', p_ctx.astype(cv.dtype), cv))
    return o.reshape(B, T, ATTN_WIDTH)


def gla_chunked(q, k, v, g, s0):
    B, T, H, _ = q.shape
    n = T // GLA_CHUNK
    rs = lambda a: a.reshape(B, n, GLA_CHUNK, H, a.shape[-1])
    q, k, v, g = rs(q), rs(k), rs(v), rs(g)
    b = jnp.cumsum(g, axis=2)
    b_last = b[:, :, -1:]
    q_in = q * jnp.exp(b)
    k_in = k * jnp.exp(-b)
    k_end = k * jnp.exp(b_last - b)
    causal = jnp.tril(jnp.ones((GLA_CHUNK, GLA_CHUNK), dtype=bool))
    a_intra = jnp.where(causal, jnp.einsum('bnchd,bnshd->bnhcs', q_in, k_in), 0.0)
    o_intra = jnp.einsum('bnhcs,bnshv->bnchv', a_intra, v)
    delta = jnp.einsum('bnchd,bnchv->bnhdv', k_end, v)
    decay = jnp.exp(b_last[:, :, 0])

    def step(s, inp):
        dec, dlt = inp
        return dec[..., None] * s + dlt, s

    s_final, s_before = lax.scan(step, s0, (jnp.moveaxis(decay, 1, 0), jnp.moveaxis(delta, 1, 0)))
    s_before = jnp.moveaxis(s_before, 0, 1)
    o_inter = jnp.einsum('bnchd,bnhdv->bnchv', q_in, s_before)
    return (o_intra + o_inter).reshape(B, T, H, v.shape[-1]), s_final


def gla_mixer(q, k, v, z, r, gate_w, gate_b, norm_g, s0):
    B, T = q.shape[0], q.shape[1]
    f = jnp.float32
    q = q.astype(f).reshape(B, T, GLA_HEADS, GLA_DK) * (GLA_DK ** -0.5)
    k = k.astype(f).reshape(B, T, GLA_HEADS, GLA_DK)
    v = v.astype(f).reshape(B, T, GLA_HEADS, GLA_DV)
    z = z.astype(f).reshape(B, T, N_DIRS, GLA_GATE_RANK)
    g = jax.nn.log_sigmoid(jnp.einsum('btdr,drw->btdw', z, gate_w.astype(f)) + gate_b.astype(f)) / GLA_GATE_NORM
    g = g.reshape(B, T, N_DIRS, GLA_HEADS, GLA_DK)
    s0 = s0.astype(f)
    flip = lambda a: jnp.flip(a, axis=1)
    o_f, s_f = gla_chunked(q, k, v, g[:, :, 0], s0[:, 0])
    o_b, s_b = gla_chunked(flip(q), flip(k), flip(v), flip(g[:, :, 1]), s0[:, 1])
    o = o_f + flip(o_b)
    o = o * lax.rsqrt(jnp.mean(o * o, axis=-1, keepdims=True) + EPS)
    o = o.reshape(B, T, GLA_WIDTH) * norm_g.astype(f) * jax.nn.silu(r.astype(f))
    return o, jnp.stack([s_f, s_b], axis=1)


def centred_dwconv(x, w, b):
    T = x.shape[1]
    left = LRU_CONV // 2
    xp = jnp.pad(x, ((0, 0), (left, LRU_CONV - 1 - left), (0, 0)))
    y = b
    for tap in range(LRU_CONV):
        y = y + w[tap] * xp[:, tap:tap + T]
    return y


def linear_scan(a, b, h0):
    def combine(e1, e2):
        a1, b1 = e1
        a2, b2 = e2
        return a1 * a2, a2 * b1 + b2
    a_cum, b_cum = lax.associative_scan(combine, (a, b), axis=1)
    return a_cum * h0[:, None, :] + b_cum


def rglru_direction(x, wa, ba, wx, bx, lam, h0):
    B, T, W = x.shape
    xb = x.reshape(B, T, LRU_BLOCKS, LRU_BLOCK_W)
    r = jax.nn.sigmoid(jnp.einsum('btnc,ncd->btnd', xb, wa).reshape(B, T, W) + ba)
    i = jax.nn.sigmoid(jnp.einsum('btnc,ncd->btnd', xb, wx).reshape(B, T, W) + bx)
    log_a = -LRU_C * r * jax.nn.softplus(-lam)
    h = linear_scan(jnp.exp(log_a), jnp.sqrt(-jnp.expm1(2.0 * log_a)) * (i * x), h0)
    return h, h[:, -1]


def rglru_mixer(xb, gb, conv_w, conv_b, wa, ba, wx, bx, lam, h0):
    f = jnp.float32
    xc = centred_dwconv(xb.astype(f), conv_w.astype(f), conv_b.astype(f))
    wa, ba, wx, bx, lam, h0 = (a.astype(f) for a in (wa, ba, wx, bx, lam, h0))
    flip = lambda a: jnp.flip(a, axis=1)
    h_f, s_f = rglru_direction(xc, wa[0], ba[0], wx[0], bx[0], lam[0], h0[:, 0])
    h_b, s_b = rglru_direction(flip(xc), wa[1], ba[1], wx[1], bx[1], lam[1], h0[:, 1])
    y = (h_f + flip(h_b)) * jax.nn.gelu(gb.astype(f))
    return y, jnp.stack([s_f, s_b], axis=1)


def token_mixer(h, l, P, ctx):
    B, T, _ = h.shape
    q, k, v, gq, gk, gv, gz, gr, lx, lg = split_cols(h @ P['w_in'][l], IN_SIZES)
    q = q.reshape(B, T, ATTN_HEADS, HEAD_DIM)
    k = k.reshape(B, T, ATTN_KV_HEADS, HEAD_DIM)
    v = v.reshape(B, T, ATTN_KV_HEADS, HEAD_DIM)
    sink = P['attn_sink'][l]
    if ctx is None:
        att = attn_context(q, k, v, sink)
        s_gla0 = jnp.zeros((B, N_DIRS, GLA_HEADS, GLA_DK, GLA_DV), jnp.float32)
        s_lru0 = jnp.zeros((B, N_DIRS, LRU_WIDTH), jnp.float32)
    else:
        ck, cv, s_gla0, s_lru0 = ctx
        att = attn_latent(axial_rope(q), axial_rope(k), v, ck, cv, sink)
    gla, s_gla = gla_mixer(gq, gk, gv, gz, gr, P['gla_gate_w'][l], P['gla_gate_b'][l], P['gla_norm'][l], s_gla0)
    lru, s_lru = rglru_mixer(lx, lg, P['lru_conv_w'][l], P['lru_conv_b'][l], P['lru_wa'][l], P['lru_ba'][l],
                             P['lru_wx'][l], P['lru_bx'][l], P['lru_lambda'][l], s_lru0)
    mixed = jnp.concatenate([att, gla.astype(h.dtype), lru.astype(h.dtype)], axis=-1)
    out = mixed @ P['w_out'][l]
    if ctx is None:
        return out, (k, v, s_gla.astype(h.dtype), s_lru.astype(h.dtype))
    return out, None


def layer(x, cond, l, P, ctx):
    sh1, sc1, g1, sh2, sc2, g2 = ada_mod(cond, P['w_mod'][l], P['b_mod'][l])
    h = rmsnorm(x, P['norm1'][l]) * (1.0 + sc1) + sh1
    mix, new_ctx = token_mixer(h, l, P, ctx)
    x = x + g1 * mix
    h = rmsnorm(x, P['norm2'][l]) * (1.0 + sc2) + sh2
    u = jax.nn.relu(h @ P['w_mlp1'][l])
    x = x + g2 * ((u * u) @ P['w_mlp2'][l])
    return x, new_ctx


def setup_inputs(seed: int = 0) -> dict:
    key = jax.random.key(seed)
    ks = jax.random.split(key, 28)
    nrm = lambda k, shape, s: jax.random.normal(k, shape, jnp.float32) * s
    L, D = DEPTH, D_MODEL
    u = jax.random.uniform(ks[23], (L, N_DIRS, LRU_WIDTH), jnp.float32, 0.9, 0.999)
    a0 = u ** (1.0 / LRU_C)
    return {
        'x_prompt': nrm(ks[0], (BATCH, SEQ, D), 1.0),
        'x_sample': nrm(ks[1], (DEC_BATCH, DEC_SEQ, D), 1.0),
        'c': nrm(ks[2], (DEC_BATCH, D), 1.0),
        'cache_k': nrm(ks[3], (DEC_BATCH, L, PAST_LEN, ATTN_KV_HEADS, HEAD_DIM), 1.0),
        'cache_v': nrm(ks[4], (DEC_BATCH, L, PAST_LEN, ATTN_KV_HEADS, HEAD_DIM), 1.0),
        'state_gla': nrm(ks[5], (DEC_BATCH, L, N_DIRS, GLA_HEADS, GLA_DK, GLA_DV), 1.0),
        'state_lru': nrm(ks[6], (DEC_BATCH, L, N_DIRS, LRU_WIDTH), 0.5),
        'c_ctx': nrm(ks[7], (D,), 1.0),
        'w_mod': nrm(ks[8], (L, D, 6 * D), D ** -0.5),
        'b_mod': nrm(ks[9], (L, 6 * D), 0.02),
        'norm1': 1.0 + nrm(ks[10], (L, D), 0.02),
        'norm2': 1.0 + nrm(ks[11], (L, D), 0.02),
        'w_in': nrm(ks[12], (L, D, IN_COLS), D ** -0.5),
        'attn_sink': nrm(ks[13], (L, ATTN_HEADS), 0.5),
        'gla_gate_w': nrm(ks[14], (L, N_DIRS, GLA_GATE_RANK, GLA_QK_WIDTH), GLA_GATE_RANK ** -0.5),
        'gla_gate_b': nrm(ks[15], (L, N_DIRS, GLA_QK_WIDTH), 0.1),
        'gla_norm': 1.0 + nrm(ks[16], (L, GLA_WIDTH), 0.02),
        'lru_conv_w': nrm(ks[17], (L, LRU_CONV, LRU_WIDTH), LRU_CONV ** -0.5),
        'lru_conv_b': nrm(ks[18], (L, LRU_WIDTH), 0.02),
        'lru_wa': nrm(ks[19], (L, N_DIRS, LRU_BLOCKS, LRU_BLOCK_W, LRU_BLOCK_W), LRU_BLOCK_W ** -0.5),
        'lru_ba': nrm(ks[20], (L, N_DIRS, LRU_WIDTH), 0.1),
        'lru_wx': nrm(ks[21], (L, N_DIRS, LRU_BLOCKS, LRU_BLOCK_W, LRU_BLOCK_W), LRU_BLOCK_W ** -0.5),
        'lru_bx': nrm(ks[22], (L, N_DIRS, LRU_WIDTH), 0.1),
        'lru_lambda': jnp.log(a0) - jnp.log1p(-a0),
        'w_out': nrm(ks[24], (L, MIX_WIDTH, D), MIX_WIDTH ** -0.5),
        'w_mlp1': nrm(ks[25], (L, D, D_FF), D ** -0.5),
        'w_mlp2': nrm(ks[26], (L, D_FF, D), D_FF ** -0.5),
        'final_norm': 1.0 + nrm(ks[27], (D,), 0.02),
    }


def reference(x_prompt, x_sample, c, cache_k, cache_v, state_gla, state_lru, c_ctx,
              w_mod, b_mod, norm1, norm2, w_in, attn_sink, gla_gate_w, gla_gate_b, gla_norm,
              lru_conv_w, lru_conv_b, lru_wa, lru_ba, lru_wx, lru_bx, lru_lambda,
              w_out, w_mlp1, w_mlp2, final_norm):
    P = {'w_mod': w_mod, 'b_mod': b_mod, 'norm1': norm1, 'norm2': norm2, 'w_in': w_in,
         'attn_sink': attn_sink, 'gla_gate_w': gla_gate_w, 'gla_gate_b': gla_gate_b,
         'gla_norm': gla_norm, 'lru_conv_w': lru_conv_w, 'lru_conv_b': lru_conv_b,
         'lru_wa': lru_wa, 'lru_ba': lru_ba, 'lru_wx': lru_wx, 'lru_bx': lru_bx,
         'lru_lambda': lru_lambda, 'w_out': w_out, 'w_mlp1': w_mlp1, 'w_mlp2': w_mlp2}

    xp = x_prompt
    ks_, vs_, sg_, sl_ = [], [], [], []
    for l in range(DEPTH):
        xp, (k_l, v_l, sg_l, sl_l) = layer(xp, c_ctx, l, P, None)
        ks_.append(k_l)
        vs_.append(v_l)
        sg_.append(sg_l)
        sl_.append(sl_l)
    y_prompt = rmsnorm(xp, final_norm)
    new_cache_k = jnp.stack(ks_, axis=1)
    new_cache_v = jnp.stack(vs_, axis=1)
    new_state_gla = jnp.stack(sg_, axis=1)
    new_state_lru = jnp.stack(sl_, axis=1)

    xs = x_sample
    for l in range(DEPTH):
        xs, _ = layer(xs, c, l, P, (cache_k[:, l], cache_v[:, l], state_gla[:, l], state_lru[:, l]))
    y_sample = rmsnorm(xs, final_norm)

    return (y_prompt, y_sample, new_cache_k, new_cache_v, new_state_gla, new_state_lru)
```

```python
import functools

import jax
import jax.numpy as jnp
from jax import lax
from jax.experimental import pallas as pl
from jax.experimental.pallas import tpu as pltpu

F32 = jnp.float32
BF16 = jnp.bfloat16

D_MODEL = 1024
GRID_W = 64
EPS = 1e-6
HEAD_DIM = 64
ATTN_WIDTH = 512
ATTN_HEADS = 8
KV_WIDTH = 128
WINDOW = 128
ROPE_BASE = 10000.0
NEG_INF = -1e30
GLA_WIDTH = 256
GLA_DK = 64
GLA_HEADS = 4
GLA_GATE_RANK = 16
GLA_GATE_NORM = 16.0
GLA_CHUNK = 64
LRU_WIDTH = 256
LRU_C = 8.0
D_FF = 4096
MOD_ROWS = 8
GZ_PAD = 128
C_Q, C_K, C_V, C_GLA, C_LX, C_LG, C_END = 0, 512, 640, 768, 1920, 2176, 2432
GLA_IN = C_LX - C_GLA
VMEM_LIMIT = 56 * 1024 * 1024

NT_DIMS = (((1,), (1,)), ((), ()))


def _split_bf16(a):
    hi = a.astype(BF16)
    lo = (a - hi.astype(F32)).astype(BF16)
    return hi, lo


def _dot(a, b):
    return jnp.dot(a, b, preferred_element_type=F32)


def _dot_nt(a, b):
    return lax.dot_general(a, b, NT_DIMS, preferred_element_type=F32)


def _dot_x3(a, b):
    ah, al = _split_bf16(a)
    bh, bl = _split_bf16(b)
    return _dot(ah, bh) + (_dot(ah, bl) + _dot(al, bh))


def _dot_exact_lhs(a_bf16, b):
    bh, bl = _split_bf16(b)
    return _dot(a_bf16, bh) + _dot(a_bf16, bl)


def _softplus(y):
    return jnp.maximum(y, 0.0) + jnp.log1p(jnp.exp(-jnp.abs(y)))


def _log_sigmoid(y):
    return -_softplus(-y)


def _silu(y):
    return y * jax.nn.sigmoid(y)


def _gelu_tanh(y):
    c = 0.7978845608028654
    return 0.5 * y * (1.0 + jnp.tanh(c * (y + 0.044715 * (y * y * y))))


def _params(n_axes):
    return pltpu.CompilerParams(
        dimension_semantics=("arbitrary",) * n_axes, vmem_limit_bytes=VMEM_LIMIT)


def _const_spec(shape):
    nd = len(shape)
    return pl.BlockSpec(shape, lambda *_: (0,) * nd)


def _ada_kernel(c_ref, w_ref, b_ref, o_ref):
    s = _silu(c_ref[...])
    o_ref[0] = _dot_x3(s, w_ref[0]) + b_ref[0]


def _ada_mod(cond, w_mod, b_mod):
    depth, d, n = w_mod.shape
    tn = 1536
    return pl.pallas_call(
        _ada_kernel,
        out_shape=jax.ShapeDtypeStruct((depth, MOD_ROWS, n), F32),
        grid=(depth, n // tn),
        in_specs=[
            pl.BlockSpec((MOD_ROWS, d), lambda l, j: (0, 0)),
            pl.BlockSpec((1, d, tn), lambda l, j: (l, 0, j)),
            pl.BlockSpec((1, 1, tn), lambda l, j: (l, 0, j)),
        ],
        out_specs=pl.BlockSpec((1, MOD_ROWS, tn), lambda l, j: (l, 0, j)),
        compiler_params=_params(2),
        name="ada_mod",
    )(cond, w_mod, b_mod.reshape(depth, 1, n))


def _rope(x, cos, sin_signed):
    w = x.shape[1]
    lane = lax.broadcasted_iota(jnp.int32, (1, w), 1)
    first = (lane % 32) < 16
    swapped = jnp.where(first, pltpu.roll(x, w - 16, 1), pltpu.roll(x, 16, 1))
    return x * cos + swapped * sin_signed


def _inproj_kernel(*refs, rope):
    if rope:
        (x_ref, mod_ref, n1_ref, w_ref, cos_ref, sin_ref,
         q_ref, k_ref, v_ref, gla_ref, lx_ref, lg_ref) = refs
    else:
        (x_ref, mod_ref, n1_ref, w_ref,
         q_ref, k_ref, v_ref, gla_ref, lx_ref, lg_ref) = refs
    d = D_MODEL
    x = x_ref[...]
    ms = jnp.mean(x * x, axis=-1, keepdims=True)
    xn = x * lax.rsqrt(ms + EPS) * n1_ref[...]
    sh = mod_ref[0, :, 0:d]
    sc = mod_ref[0, :, d:2 * d]
    h = (xn * (1.0 + sc) + sh).astype(BF16)
    p = _dot(h, w_ref[...])
    q = p[:, C_Q:C_K]
    k = p[:, C_K:C_V]
    v = p[:, C_V:C_GLA]
    if rope:
        cos = cos_ref[...]
        sin = sin_ref[...]
        k = _rope(k, cos, sin)
        q = _rope(q, jnp.concatenate([cos] * 4, axis=1), jnp.concatenate([sin] * 4, axis=1))
    q_ref[...] = (q * (HEAD_DIM ** -0.5)).astype(BF16)
    k_ref[...] = k
    v_ref[...] = v
    gla_ref[...] = p[:, C_GLA:C_LX]
    lx_ref[...] = p[:, C_LX:C_LG]
    lg_ref[...] = p[:, C_LG:C_END]


def _in_proj(x2d, mod_l, norm1, w, rope_tabs, rows_per_mod, mod_row0, tt):
    n, d = x2d.shape
    nt = n // tt
    rope = rope_tabs is not None
    mod_map = lambda i: (mod_row0 + (i * tt) // rows_per_mod, 0, 0)
    in_specs = [
        pl.BlockSpec((tt, d), lambda i: (i, 0)),
        pl.BlockSpec((1, 1, 6 * d), mod_map),
        _const_spec((1, d)),
        _const_spec((d, C_END)),
    ]
    args = [x2d, mod_l.reshape(MOD_ROWS, 1, 6 * d), norm1.reshape(1, d), w]
    if rope:
        t = rope_tabs[0].shape[0]
        tpb = t // tt
        in_specs += [pl.BlockSpec((tt, 128), lambda i: (i % tpb, 0))] * 2
        args += list(rope_tabs)
    widths = (ATTN_WIDTH, KV_WIDTH, KV_WIDTH, GLA_IN, LRU_WIDTH, LRU_WIDTH)
    dtypes = (BF16, F32, F32, F32, F32, F32)
    return pl.pallas_call(
        functools.partial(_inproj_kernel, rope=rope),
        out_shape=[jax.ShapeDtypeStruct((n, wd), dt) for wd, dt in zip(widths, dtypes)],
        grid=(nt,),
        in_specs=in_specs,
        out_specs=[pl.BlockSpec((tt, wd), lambda i: (i, 0)) for wd in widths],
        compiler_params=_params(1),
        name="in_proj_lat" if rope else "in_proj_ctx",
    )(*args)


def _half_variants(a):
    lane = lax.broadcasted_iota(jnp.int32, (1, 128), 1)
    low = lane < HEAD_DIM
    ar = pltpu.roll(a, HEAD_DIM, 1)
    z = jnp.zeros_like(a)
    cast = lambda t: t.astype(BF16)
    return ((cast(jnp.where(low, a, z)), cast(jnp.where(low, z, ar))),
            (cast(jnp.where(low, ar, z)), cast(jnp.where(low, z, a))))


def _attn_kernel(*refs, latent, tq, seq):
    if latent:
        (sink_ref, q_ref, kp_ref, kc_ref, kn_ref, vp_ref, vc_ref, vn_ref,
         ck_ref, cv_ref, o_ref) = refs
        i = pl.program_id(1)
        k_loc = jnp.concatenate([kp_ref[...], kc_ref[...], kn_ref[...]], axis=0)
        v_loc = jnp.concatenate([vp_ref[...], vc_ref[...], vn_ref[...]], axis=0)
        nk = tq + 2 * WINDOW
        r = lax.broadcasted_iota(jnp.int32, (tq, nk), 0)
        j = lax.broadcasted_iota(jnp.int32, (tq, nk), 1)
        kpos = i * tq - WINDOW + j
        valid = (j >= r) & (j <= r + 2 * WINDOW) & (kpos >= 0) & (kpos < seq)
        kvar_c = _half_variants(ck_ref[0, 0])
        vvar_c = _half_variants(cv_ref[0, 0])
    else:
        sink_ref, q_ref, kc_ref, vc_ref, o_ref = refs
        k_loc = kc_ref[...]
        v_loc = vc_ref[...]
        valid = None
    kvar = _half_variants(k_loc)
    vvar = _half_variants(v_loc)
    for c in range(ATTN_HEADS // 2):
        kv = c // 2
        qc = q_ref[:, c * 128:(c + 1) * 128]
        acc = None
        for half in range(2):
            sink = sink_ref[2 * c + half]
            s = _dot_nt(qc, kvar[kv][half])
            if valid is not None:
                s = jnp.where(valid, s, NEG_INF)
            m = jnp.maximum(jnp.max(s, axis=-1, keepdims=True), sink)
            if latent:
                s_c = _dot_nt(qc, kvar_c[kv][half])
                m = jnp.maximum(m, jnp.max(s_c, axis=-1, keepdims=True))
            p = jnp.exp(s - m)
            denom = jnp.exp(sink - m) + jnp.sum(p, axis=-1, keepdims=True)
            o = _dot(p.astype(BF16), vvar[kv][half])
            if latent:
                p_c = jnp.exp(s_c - m)
                denom = denom + jnp.sum(p_c, axis=-1, keepdims=True)
                o = o + _dot(p_c.astype(BF16), vvar_c[kv][half])
            o = o * (1.0 / denom)
            acc = o if acc is None else acc + o
        o_ref[:, c * 128:(c + 1) * 128] = acc.astype(BF16)


def _attention(q, k, v, sink, batch, seq, tq, ctx_kv=None):
    latent = ctx_kv is not None
    nq = seq // tq
    kern = functools.partial(_attn_kernel, latent=latent, tq=tq, seq=seq)
    smem = pl.BlockSpec(memory_space=pltpu.SMEM)
    if latent:
        ck, cv, layer = ctx_kv
        past = ck.shape[2]
        wb = tq // WINDOW
        nwb = seq // WINDOW
        cur = pl.BlockSpec((tq, KV_WIDTH), lambda b, i: (b * nq + i, 0))
        prev = pl.BlockSpec((WINDOW, KV_WIDTH),
                            lambda b, i: (b * nwb + jnp.maximum(i * wb - 1, 0), 0))
        nxt = pl.BlockSpec((WINDOW, KV_WIDTH),
                           lambda b, i: (b * nwb + jnp.minimum((i + 1) * wb, nwb - 1), 0))
        cspec = pl.BlockSpec((1, 1, past, KV_WIDTH), lambda b, i: (b, layer, 0, 0))
        in_specs = [smem, pl.BlockSpec((tq, ATTN_WIDTH), lambda b, i: (b * nq + i, 0)),
                    prev, cur, nxt, prev, cur, nxt, cspec, cspec]
        args = (sink, q, k, k, k, v, v, v, ck, cv)
    else:
        cur = pl.BlockSpec((tq, KV_WIDTH), lambda b, i: (b * nq + i, 0))
        in_specs = [smem, pl.BlockSpec((tq, ATTN_WIDTH), lambda b, i: (b * nq + i, 0)), cur, cur]
        args = (sink, q, k, v)
    return pl.pallas_call(
        kern,
        out_shape=jax.ShapeDtypeStruct((batch * seq, ATTN_WIDTH), BF16),
        grid=(batch, nq),
        in_specs=in_specs,
        out_specs=pl.BlockSpec((tq, ATTN_WIDTH), lambda b, i: (b * nq + i, 0)),
        compiler_params=_params(2),
        name="attn_lat" if latent else "attn_ctx",
    )(*args)


def _gla_sweep(fwd, j, nt, tt, gin_ref, wg_ref, bg_ref, ng_ref, s0_ref,
               o_ref, sout_ref, st_scr, g_scr, ob_scr):
    w = GLA_WIDTH
    ck = GLA_CHUNK
    nc = tt // ck
    tile = j if fwd else nt - 1 - j

    @pl.when(j == 0)
    def _():
        st_scr[...] = s0_ref[0, 0]

    zpad = gin_ref[:, 4 * w:4 * w + GZ_PAD]
    logit = _dot_x3(zpad, wg_ref[0]) + bg_ref[0]
    g_scr[...] = _log_sigmoid(logit) * (1.0 / GLA_GATE_NORM)

    lane = lax.broadcasted_iota(jnp.int32, (1, w), 1) // GLA_DK
    head_mask = [(lane == h).astype(F32) for h in range(GLA_HEADS)]
    rr = lax.broadcasted_iota(jnp.int32, (w, w), 0) // GLA_DK
    cc = lax.broadcasted_iota(jnp.int32, (w, w), 1) // GLA_DK
    block_diag = rr == cc
    tr = lax.broadcasted_iota(jnp.int32, (ck, ck), 0)
    tc = lax.broadcasted_iota(jnp.int32, (ck, ck), 1)
    tri = (tr >= tc) if fwd else (tr <= tc)
    tri_bf = tri.astype(F32).astype(BF16)
    ar = lax.broadcasted_iota(jnp.int32, (GLA_HEADS * ck, ck), 0) % ck
    ac = lax.broadcasted_iota(jnp.int32, (GLA_HEADS * ck, ck), 1)
    causal4 = (ar >= ac) if fwd else (ar <= ac)
    mean_mat = block_diag.astype(F32).astype(BF16)

    for ci in range(nc):
        c0 = (ci if fwd else nc - 1 - ci) * ck
        rows = pl.ds(c0, ck)
        q = gin_ref[rows, 0:w] * (GLA_DK ** -0.5)
        k = gin_ref[rows, w:2 * w]
        v = gin_ref[rows, 2 * w:3 * w]
        g = g_scr[rows, :]
        st = st_scr[...]
        b = _dot_exact_lhs(tri_bf, g)
        btot = jnp.sum(g, axis=0, keepdims=True)
        q_in = q * jnp.exp(b)
        k_in = (k * jnp.exp(-b)).astype(BF16)
        k_end = (k * jnp.exp(btot - b)).astype(BF16)
        vb = v.astype(BF16)
        q_stack = jnp.concatenate([q_in * head_mask[h] for h in range(GLA_HEADS)],
                                  axis=0).astype(BF16)
        a = jnp.where(causal4, _dot_nt(q_stack, k_in), 0.0)
        ov = _dot(a.astype(BF16), vb)
        o = _dot_nt(q_in.astype(BF16), st.astype(BF16))
        for h in range(GLA_HEADS):
            o = o + ov[h * ck:(h + 1) * ck, :] * head_mask[h]
        delta_t = _dot(v.T.astype(BF16), k_end)
        st_scr[...] = st * jnp.exp(btot) + jnp.where(block_diag, delta_t, 0.0)
        all_rows = pl.ds(pl.multiple_of(tile * tt, ck) + c0, ck)
        if fwd:
            o = o + ob_scr[all_rows, :]
            ms = _dot_exact_lhs_rhs(o * o, mean_mat) * (1.0 / GLA_DK)
            r = gin_ref[rows, 3 * w:4 * w]
            o_ref[rows, :] = o * lax.rsqrt(ms + EPS) * ng_ref[...] * _silu(r)
        else:
            ob_scr[all_rows, :] = o

    @pl.when(j == nt - 1)
    def _():
        sout_ref[0, 0] = st_scr[...]


def _dot_exact_lhs_rhs(a, b_bf16):
    ah, al = _split_bf16(a)
    return _dot(ah, b_bf16) + _dot(al, b_bf16)


def _gla_kernel(gin_ref, wg_ref, bg_ref, ng_ref, s0_ref, o_ref, sout_ref,
                st_scr, g_scr, ob_scr, *, nt, tt):
    sweep = pl.program_id(1)
    j = pl.program_id(2)
    args = (j, nt, tt, gin_ref, wg_ref, bg_ref, ng_ref, s0_ref, o_ref, sout_ref,
            st_scr, g_scr, ob_scr)

    @pl.when(sweep == 0)
    def _():
        _gla_sweep(False, *args)

    @pl.when(sweep == 1)
    def _():
        _gla_sweep(True, *args)


def _gla(gin, wg, bg, norm_g, s0, batch, seq, tt):
    nt = seq // tt
    w = GLA_WIDTH
    per_batch_state = s0.shape[0] == batch
    tile_of = lambda s, j: j * (2 * s - 1) + (1 - s) * (nt - 1)
    return pl.pallas_call(
        functools.partial(_gla_kernel, nt=nt, tt=tt),
        out_shape=[jax.ShapeDtypeStruct((batch * seq, w), F32),
                   jax.ShapeDtypeStruct((batch, 2, w, w), F32)],
        grid=(batch, 2, nt),
        in_specs=[
            pl.BlockSpec((tt, GLA_IN), lambda b, s, j: (b * nt + tile_of(s, j), 0)),
            pl.BlockSpec((1, GZ_PAD, w), lambda b, s, j: (s, 0, 0)),
            pl.BlockSpec((1, 1, w), lambda b, s, j: (s, 0, 0)),
            _const_spec((1, w)),
            pl.BlockSpec((1, 1, w, w),
                         lambda b, s, j: (b if per_batch_state else 0, s, 0, 0)),
        ],
        out_specs=[
            pl.BlockSpec((tt, w), lambda b, s, j: (b * nt + j * s, 0)),
            pl.BlockSpec((1, 1, w, w), lambda b, s, j: (b, s, 0, 0)),
        ],
        scratch_shapes=[pltpu.VMEM((w, w), F32), pltpu.VMEM((tt, w), F32),
                        pltpu.VMEM((seq, w), F32)],
        compiler_params=_params(3),
        name="gla",
    )(gin, wg, bg, norm_g.reshape(1, w), s0)


def _lru_sweep(fwd, j, nt, tt, xp_ref, xc_ref, xn_ref, lg_ref, cw_ref, cb_ref,
               wg_ref, bg_ref, lam_ref, h0_ref, o_ref, sout_ref,
               carry_scr, a_scr, b_scr, hb_scr):
    w = LRU_WIDTH
    ng = tt // 8
    tile = j if fwd else nt - 1 - j

    @pl.when(j == 0)
    def _():
        carry_scr[...] = jnp.broadcast_to(h0_ref[0, 0], (8, w))

    pre = jnp.where(tile > 0, xp_ref[...], 0.0)
    post = jnp.where(tile < nt - 1, xn_ref[...], 0.0)
    xe = jnp.concatenate([pre, xc_ref[...], post], axis=0)
    ne = tt + 16
    xc = cb_ref[...] + cw_ref[2:3, :] * xe[8:8 + tt]
    xc = xc + cw_ref[0:1, :] * pltpu.roll(xe, 2, 0)[8:8 + tt]
    xc = xc + cw_ref[1:2, :] * pltpu.roll(xe, 1, 0)[8:8 + tt]
    xc = xc + cw_ref[3:4, :] * pltpu.roll(xe, ne - 1, 0)[8:8 + tt]

    ri = _dot(xc.astype(BF16), wg_ref[0]) + bg_ref[0]
    r = jax.nn.sigmoid(ri[:, 0:w])
    gate_i = jax.nn.sigmoid(ri[:, w:2 * w])
    log_a = (-LRU_C) * r * _softplus(-lam_ref[0])
    a = jnp.exp(log_a)
    bt = jnp.sqrt(1.0 - jnp.exp(2.0 * log_a)) * (gate_i * xc)

    a3 = a.reshape(ng, 8, w)
    b3 = bt.reshape(ng, 8, w)
    sub = lax.broadcasted_iota(jnp.int32, (1, 8, w), 1)
    for s in (1, 2, 4):
        if fwd:
            ok = sub >= s
            a_sh = pltpu.roll(a3, s, 1)
            b_sh = pltpu.roll(b3, s, 1)
        else:
            ok = sub < 8 - s
            a_sh = pltpu.roll(a3, 8 - s, 1)
            b_sh = pltpu.roll(b3, 8 - s, 1)
        b3 = jnp.where(ok, a3 * b_sh + b3, b3)
        a3 = jnp.where(ok, a3 * a_sh, a3)
    a_scr[...] = a3.reshape(tt, w)
    b_scr[...] = b3.reshape(tt, w)

    def group(gi, h_prev):
        g = gi if fwd else ng - 1 - gi
        rows = pl.ds(pl.multiple_of(g * 8, 8), 8)
        h = a_scr[rows, :] * h_prev + b_scr[rows, :]
        b_scr[rows, :] = h
        last = h[7:8, :] if fwd else h[0:1, :]
        return jnp.broadcast_to(last, (8, w))

    carry_scr[...] = lax.fori_loop(0, ng, group, carry_scr[...])
    h_all = b_scr[...]
    all_rows = pl.ds(pl.multiple_of(tile * tt, 8), tt)
    if fwd:
        o_ref[...] = (h_all + hb_scr[all_rows, :]) * _gelu_tanh(lg_ref[...])
    else:
        hb_scr[all_rows, :] = h_all

    @pl.when(j == nt - 1)
    def _():
        sout_ref[0, 0] = carry_scr[...]


def _lru_kernel(xp_ref, xc_ref, xn_ref, lg_ref, cw_ref, cb_ref, wg_ref, bg_ref,
                lam_ref, h0_ref, o_ref, sout_ref, carry_scr, a_scr, b_scr, hb_scr,
                *, nt, tt):
    sweep = pl.program_id(1)
    j = pl.program_id(2)
    args = (j, nt, tt, xp_ref, xc_ref, xn_ref, lg_ref, cw_ref, cb_ref, wg_ref, bg_ref,
            lam_ref, h0_ref, o_ref, sout_ref, carry_scr, a_scr, b_scr, hb_scr)

    @pl.when(sweep == 0)
    def _():
        _lru_sweep(False, *args)

    @pl.when(sweep == 1)
    def _():
        _lru_sweep(True, *args)


def _lru(lx, lg, conv_w, conv_b, wg, bg, lam, h0, batch, seq, tt):
    nt = seq // tt
    w = LRU_WIDTH
    t8 = tt // 8
    n8 = seq // 8
    per_batch_state = h0.shape[0] == batch
    tile_of = lambda s, j: j * (2 * s - 1) + (1 - s) * (nt - 1)
    return pl.pallas_call(
        functools.partial(_lru_kernel, nt=nt, tt=tt),
        out_shape=[jax.ShapeDtypeStruct((batch * seq, w), F32),
                   jax.ShapeDtypeStruct((batch, 2, 8, w), F32)],
        grid=(batch, 2, nt),
        in_specs=[
            pl.BlockSpec((8, w), lambda b, s, j:
                         (b * n8 + jnp.maximum(tile_of(s, j) * t8 - 1, 0), 0)),
            pl.BlockSpec((tt, w), lambda b, s, j: (b * nt + tile_of(s, j), 0)),
            pl.BlockSpec((8, w), lambda b, s, j:
                         (b * n8 + jnp.minimum((tile_of(s, j) + 1) * t8, n8 - 1), 0)),
            pl.BlockSpec((tt, w), lambda b, s, j: (b * nt + j * s, 0)),
            _const_spec((4, w)),
            _const_spec((1, w)),
            pl.BlockSpec((1, w, 2 * w), lambda b, s, j: (s, 0, 0)),
            pl.BlockSpec((1, 1, 2 * w), lambda b, s, j: (s, 0, 0)),
            pl.BlockSpec((1, 1, w), lambda b, s, j: (s, 0, 0)),
            pl.BlockSpec((1, 1, 1, w),
                         lambda b, s, j: (b if per_batch_state else 0, s, 0, 0)),
        ],
        out_specs=[
            pl.BlockSpec((tt, w), lambda b, s, j: (b * nt + j * s, 0)),
            pl.BlockSpec((1, 1, 8, w), lambda b, s, j: (b, s, 0, 0)),
        ],
        scratch_shapes=[pltpu.VMEM((8, w), F32), pltpu.VMEM((tt, w), F32),
                        pltpu.VMEM((tt, w), F32), pltpu.VMEM((seq, w), F32)],
        compiler_params=_params(3),
        name="lru",
    )(lx, lx, lx, lg, conv_w, conv_b.reshape(1, w), wg, bg, lam, h0)


def _mlp_kernel(x_ref, att_ref, gla_ref, lru_ref, mod_ref, n2_ref, fn_ref,
                wo_ref, w1_ref, w2_ref, o_ref, *, final, ff_chunk):
    d = D_MODEL
    mix = _dot(att_ref[...], wo_ref[0:ATTN_WIDTH, :])
    mix = mix + _dot(gla_ref[...].astype(BF16), wo_ref[ATTN_WIDTH:ATTN_WIDTH + GLA_WIDTH, :])
    mix = mix + _dot(lru_ref[...].astype(BF16), wo_ref[ATTN_WIDTH + GLA_WIDTH:d, :])
    g1 = mod_ref[0, :, 2 * d:3 * d]
    sh2 = mod_ref[0, :, 3 * d:4 * d]
    sc2 = mod_ref[0, :, 4 * d:5 * d]
    g2 = mod_ref[0, :, 5 * d:6 * d]
    x = x_ref[...] + g1 * mix
    ms = jnp.mean(x * x, axis=-1, keepdims=True)
    h = (x * lax.rsqrt(ms + EPS) * n2_ref[...] * (1.0 + sc2) + sh2).astype(BF16)
    y = None
    for c in range(D_FF // ff_chunk):
        cols = slice(c * ff_chunk, (c + 1) * ff_chunk)
        u = jnp.maximum(_dot(h, w1_ref[:, cols]), 0.0)
        part = _dot((u * u).astype(BF16), w2_ref[cols, :])
        y = part if y is None else y + part
    x = x + g2 * y
    if final:
        ms = jnp.mean(x * x, axis=-1, keepdims=True)
        x = x * lax.rsqrt(ms + EPS) * fn_ref[...]
    o_ref[...] = x


def _out_mlp(x2d, att, gla, lru, mod_l, norm2, final_norm, wo, w1, w2,
             rows_per_mod, mod_row0, tt, final):
    n, d = x2d.shape
    nt = n // tt
    mod_map = lambda i: (mod_row0 + (i * tt) // rows_per_mod, 0, 0)
    row = lambda wd: pl.BlockSpec((tt, wd), lambda i: (i, 0))
    resident = lambda shape: pl.BlockSpec(shape, lambda i: (0, 0),
                                          pipeline_mode=pl.Buffered(1))
    return pl.pallas_call(
        functools.partial(_mlp_kernel, final=final, ff_chunk=1024),
        out_shape=jax.ShapeDtypeStruct((n, d), F32),
        grid=(nt,),
        in_specs=[row(d), row(ATTN_WIDTH), row(GLA_WIDTH), row(LRU_WIDTH),
                  pl.BlockSpec((1, 1, 6 * d), mod_map),
                  _const_spec((1, d)), _const_spec((1, d)),
                  resident((d, d)), resident((d, D_FF)), resident((D_FF, d))],
        out_specs=row(d),
        compiler_params=_params(1),
        name="out_mlp",
    )(x2d, att, gla, lru, mod_l.reshape(MOD_ROWS, 1, 6 * d), norm2.reshape(1, d),
      final_norm.reshape(1, d), wo, w1, w2)


def _prep_w_in(w_in_l):
    d = w_in_l.shape[0]
    o = 0
    parts = {}
    for name, size in (("q", 512), ("k", 128), ("v", 128), ("gq", 256), ("gk", 256),
                       ("gv", 256), ("gz", 32), ("gr", 256), ("lx", 256), ("lg", 256)):
        parts[name] = w_in_l[:, o:o + size]
        o += size
    gz = jnp.concatenate([parts["gz"], jnp.zeros((d, GZ_PAD - 32), w_in_l.dtype)], axis=1)
    cols = [parts["q"], parts["k"], parts["v"], parts["gq"], parts["gk"], parts["gv"],
            parts["gr"], gz, parts["lx"], parts["lg"]]
    return jnp.concatenate(cols, axis=1).astype(BF16)


def _block_diag(w4):
    n, a, b = w4.shape
    eye = jnp.eye(n, dtype=w4.dtype)
    return jnp.einsum("nab,nm->namb", w4, eye).reshape(n * a, n * b)


def _rope_tables(seq):
    rows = seq // GRID_W
    row = jnp.repeat(jnp.arange(rows), GRID_W).astype(F32)
    col = jnp.tile(jnp.arange(GRID_W), rows).astype(F32)
    nf = HEAD_DIM // 4
    inv = ROPE_BASE ** (-jnp.arange(nf, dtype=F32) / nf)
    ang_r = row[:, None] * inv
    ang_c = col[:, None] * inv
    cos = jnp.concatenate([jnp.cos(ang_r)] * 2 + [jnp.cos(ang_c)] * 2, axis=1)
    sin = jnp.concatenate([-jnp.sin(ang_r), jnp.sin(ang_r),
                           -jnp.sin(ang_c), jnp.sin(ang_c)], axis=1)
    return jnp.concatenate([cos] * 2, axis=1), jnp.concatenate([sin] * 2, axis=1)


def _gla_state_to_internal(s):
    b = s.shape[0]
    st = jnp.swapaxes(s, -1, -2)
    eye = jnp.eye(GLA_HEADS, dtype=s.dtype)
    full = jnp.einsum("bshvd,hg->bshvgd", st, eye)
    return full.reshape(b, 2, GLA_WIDTH, GLA_WIDTH)


def _gla_state_from_internal(st):
    b = st.shape[0]
    s6 = st.reshape(b, 2, GLA_HEADS, GLA_DK, GLA_HEADS, GLA_DK)
    diag = jnp.stack([s6[:, :, h, :, h, :] for h in range(GLA_HEADS)], axis=2)
    return jnp.swapaxes(diag, -1, -2)


def kernel(x_prompt, x_sample, c, cache_k, cache_v, state_gla, state_lru, c_ctx, w_mod, b_mod, norm1, norm2, w_in, attn_sink, gla_gate_w, gla_gate_b, gla_norm, lru_conv_w, lru_conv_b, lru_wa, lru_ba, lru_wx, lru_bx, lru_lambda, w_out, w_mlp1, w_mlp2, final_norm):
    depth = w_in.shape[0]
    bc, sc_len, d = x_prompt.shape
    bl, sl_len, _ = x_sample.shape
    past = cache_k.shape[2]

    cond = jnp.concatenate([c_ctx[None], c, jnp.zeros((MOD_ROWS - 1 - bl, d), F32)], axis=0)
    mod = _ada_mod(cond, w_mod, b_mod)

    flip_dir = lambda a: a[:, ::-1]
    rope_tabs = _rope_tables(sl_len)
    ck = cache_k.reshape(bl, depth, past, KV_WIDTH)
    cv = cache_v.reshape(bl, depth, past, KV_WIDTH)
    gla_s0_ctx = jnp.zeros((1, 2, GLA_WIDTH, GLA_WIDTH), F32)
    lru_s0_ctx = jnp.zeros((1, 2, 1, LRU_WIDTH), F32)

    xp = x_prompt.reshape(bc * sc_len, d)
    xs = x_sample.reshape(bl * sl_len, d)
    ks, vs, sgs, sls = [], [], [], []
    for l in range(depth):
        w_in_l = _prep_w_in(w_in[l])
        wo = w_out[l].astype(BF16)
        w1 = w_mlp1[l].astype(BF16)
        w2 = w_mlp2[l].astype(BF16)
        gate_w = jnp.zeros((2, GZ_PAD, GLA_WIDTH), F32)
        gate_w = gate_w.at[0, 0:GLA_GATE_RANK].set(gla_gate_w[l, 0])
        gate_w = gate_w.at[1, GLA_GATE_RANK:2 * GLA_GATE_RANK].set(gla_gate_w[l, 1])
        gate_w = gate_w[::-1]
        gate_b = gla_gate_b[l][::-1].reshape(2, 1, GLA_WIDTH)
        lru_wg = jnp.stack([jnp.concatenate([_block_diag(lru_wa[l, dr]), _block_diag(lru_wx[l, dr])],
                                            axis=1) for dr in (1, 0)]).astype(BF16)
        lru_bg = jnp.stack([jnp.concatenate([lru_ba[l, dr], lru_bx[l, dr]]) for dr in (1, 0)]
                           ).reshape(2, 1, 2 * LRU_WIDTH)
        lam = lru_lambda[l][::-1].reshape(2, 1, LRU_WIDTH)
        final = l == depth - 1

        for is_lat in (False, True):
            if is_lat:
                x2d, batch, seq, row0, rpm = xs, bl, sl_len, 1, sl_len
                tabs, tt_proj, tq, tt_rec = rope_tabs, 512, 256, 512
                gla_s0 = flip_dir(_gla_state_to_internal(state_gla[:, l]))
                lru_s0 = flip_dir(state_lru[:, l]).reshape(bl, 2, 1, LRU_WIDTH)
            else:
                x2d, batch, seq, row0, rpm = xp, bc, sc_len, 0, bc * sc_len
                tabs, tt_proj, tq, tt_rec = None, 512, sc_len, sc_len
                gla_s0, lru_s0 = gla_s0_ctx, lru_s0_ctx
            q, k, v, gin, lx, lg = _in_proj(x2d, mod[l], norm1[l], w_in_l, tabs, rpm, row0, tt_proj)
            att = _attention(q, k, v, attn_sink[l], batch, seq, tq,
                             ctx_kv=(ck, cv, l) if is_lat else None)
            gla, sg = _gla(gin, gate_w, gate_b, gla_norm[l], gla_s0, batch, seq, tt_rec)
            lru, sl = _lru(lx, lg, lru_conv_w[l], lru_conv_b[l], lru_wg, lru_bg, lam, lru_s0,
                           batch, seq, tt_rec)
            x2d = _out_mlp(x2d, att, gla, lru, mod[l], norm2[l], final_norm, wo, w1, w2,
                           rpm, row0, 512, final)
            if is_lat:
                xs = x2d
            else:
                xp = x2d
                ks.append(k.reshape(bc, sc_len, 2, HEAD_DIM))
                vs.append(v.reshape(bc, sc_len, 2, HEAD_DIM))
                sgs.append(_gla_state_from_internal(flip_dir(sg)))
                sls.append(flip_dir(sl[:, :, 0, :]))

    y_prompt = xp.reshape(bc, sc_len, d)
    y_sample = xs.reshape(bl, sl_len, d)
    return (y_prompt, y_sample, jnp.stack(ks, axis=1), jnp.stack(vs, axis=1),
            jnp.stack(sgs, axis=1), jnp.stack(sls, axis=1))
```

```python
import functools

import jax
import jax.numpy as jnp
from jax import lax
from jax.experimental import pallas as pl
from jax.experimental.pallas import tpu as pltpu

F32 = jnp.float32
BF16 = jnp.bfloat16

D_MODEL = 1024
GRID_W = 64
EPS = 1e-6
HEAD_DIM = 64
ATTN_WIDTH = 512
ATTN_HEADS = 8
KV_WIDTH = 128
WINDOW = 128
ROPE_BASE = 10000.0
NEG_INF = -1e30
GLA_WIDTH = 256
GLA_DK = 64
GLA_HEADS = 4
GLA_GATE_RANK = 16
GLA_GATE_NORM = 16.0
GLA_CHUNK = 64
LRU_WIDTH = 256
LRU_C = 8.0
LRU_SEG_PAD = 8
D_FF = 4096
MOD_ROWS = 8
GZ_PAD = 128
C_Q, C_K, C_V, C_GLA, C_LX, C_LG, C_END = 0, 512, 640, 768, 1920, 2176, 2432
GLA_IN = C_LX - C_GLA
VMEM_LIMIT = 56 * 1024 * 1024

NT_DIMS = (((1,), (1,)), ((), ()))


def _split_bf16(a):
    hi = a.astype(BF16)
    lo = (a - hi.astype(F32)).astype(BF16)
    return hi, lo


def _dot(a, b):
    return jnp.dot(a, b, preferred_element_type=F32)


def _dot_nt(a, b):
    return lax.dot_general(a, b, NT_DIMS, preferred_element_type=F32)


def _dot_x3(a, b):
    ah, al = _split_bf16(a)
    bh, bl = _split_bf16(b)
    return _dot(ah, bh) + (_dot(ah, bl) + _dot(al, bh))


def _dot_exact_lhs(a_bf16, b):
    bh, bl = _split_bf16(b)
    return _dot(a_bf16, bh) + _dot(a_bf16, bl)


def _softplus(y):
    return jnp.maximum(y, 0.0) + jnp.log1p(jnp.exp(-jnp.abs(y)))


def _log_sigmoid(y):
    return jnp.minimum(y, 0.0) - jnp.log(1.0 + jnp.exp(-jnp.abs(y)))


def _silu(y):
    return y * jax.nn.sigmoid(y)


def _gelu_tanh(y):
    c = 0.7978845608028654
    return 0.5 * y * (1.0 + jnp.tanh(c * (y + 0.044715 * (y * y * y))))


def _params(n_axes):
    return pltpu.CompilerParams(
        dimension_semantics=("arbitrary",) * n_axes, vmem_limit_bytes=VMEM_LIMIT)


def _const_spec(shape):
    nd = len(shape)
    return pl.BlockSpec(shape, lambda *_: (0,) * nd)


def _ada_kernel(c_ref, w_ref, b_ref, o_ref):
    s = _silu(c_ref[...])
    o_ref[0] = _dot_x3(s, w_ref[0]) + b_ref[0]


def _ada_mod(cond, w_mod, b_mod):
    depth, d, n = w_mod.shape
    tn = 1536
    return pl.pallas_call(
        _ada_kernel,
        out_shape=jax.ShapeDtypeStruct((depth, MOD_ROWS, n), F32),
        grid=(depth, n // tn),
        in_specs=[
            pl.BlockSpec((MOD_ROWS, d), lambda l, j: (0, 0)),
            pl.BlockSpec((1, d, tn), lambda l, j: (l, 0, j)),
            pl.BlockSpec((1, 1, tn), lambda l, j: (l, 0, j)),
        ],
        out_specs=pl.BlockSpec((1, MOD_ROWS, tn), lambda l, j: (l, 0, j)),
        compiler_params=_params(2),
        name="ada_mod",
    )(cond, w_mod, b_mod.reshape(depth, 1, n))


def _rope(x, cos, sin_signed):
    w = x.shape[1]
    lane = lax.broadcasted_iota(jnp.int32, (1, w), 1)
    first = (lane % 32) < 16
    swapped = jnp.where(first, pltpu.roll(x, w - 16, 1), pltpu.roll(x, 16, 1))
    return x * cos + swapped * sin_signed


def _inproj_kernel(*refs, rope):
    if rope:
        (x_ref, mod_ref, n1_ref, w_ref, cos_ref, sin_ref,
         q_ref, k_ref, v_ref, gla_ref, lx_ref, lg_ref) = refs
    else:
        (x_ref, mod_ref, n1_ref, w_ref,
         q_ref, k_ref, v_ref, gla_ref, lx_ref, lg_ref) = refs
    d = D_MODEL
    x = x_ref[...]
    ms = jnp.mean(x * x, axis=-1, keepdims=True)
    xn = x * lax.rsqrt(ms + EPS) * n1_ref[...]
    sh = mod_ref[0, :, 0:d]
    sc = mod_ref[0, :, d:2 * d]
    h = (xn * (1.0 + sc) + sh).astype(BF16)
    p = _dot(h, w_ref[...])
    q = p[:, C_Q:C_K]
    k = p[:, C_K:C_V]
    v = p[:, C_V:C_GLA]
    if rope:
        cos = cos_ref[...]
        sin = sin_ref[...]
        k = _rope(k, cos, sin)
        q = _rope(q, jnp.concatenate([cos] * 4, axis=1), jnp.concatenate([sin] * 4, axis=1))
    q_ref[...] = (q * (HEAD_DIM ** -0.5)).astype(BF16)
    k_ref[...] = k
    v_ref[...] = v
    gla_ref[...] = p[:, C_GLA:C_LX]
    lx_ref[...] = p[:, C_LX:C_LG]
    lg_ref[...] = p[:, C_LG:C_END]


def _in_proj(x2d, mod_l, norm1, w, rope_tabs, rows_per_mod, mod_row0, tt):
    n, d = x2d.shape
    nt = n // tt
    rope = rope_tabs is not None
    mod_map = lambda i: (mod_row0 + (i * tt) // rows_per_mod, 0, 0)
    in_specs = [
        pl.BlockSpec((tt, d), lambda i: (i, 0)),
        pl.BlockSpec((1, 1, 6 * d), mod_map),
        _const_spec((1, d)),
        _const_spec((d, C_END)),
    ]
    args = [x2d, mod_l.reshape(MOD_ROWS, 1, 6 * d), norm1.reshape(1, d), w]
    if rope:
        t = rope_tabs[0].shape[0]
        tpb = t // tt
        in_specs += [pl.BlockSpec((tt, 128), lambda i: (i % tpb, 0))] * 2
        args += list(rope_tabs)
    widths = (ATTN_WIDTH, KV_WIDTH, KV_WIDTH, GLA_IN, LRU_WIDTH, LRU_WIDTH)
    dtypes = (BF16, F32, F32, F32, F32, F32)
    return pl.pallas_call(
        functools.partial(_inproj_kernel, rope=rope),
        out_shape=[jax.ShapeDtypeStruct((n, wd), dt) for wd, dt in zip(widths, dtypes)],
        grid=(nt,),
        in_specs=in_specs,
        out_specs=[pl.BlockSpec((tt, wd), lambda i: (i, 0)) for wd in widths],
        compiler_params=_params(1),
        name="in_proj_lat" if rope else "in_proj_ctx",
    )(*args)


def _half_variants(a):
    lane = lax.broadcasted_iota(jnp.int32, (1, 128), 1)
    low = lane < HEAD_DIM
    ar = pltpu.roll(a, HEAD_DIM, 1)
    z = jnp.zeros_like(a)
    cast = lambda t: t.astype(BF16)
    return ((cast(jnp.where(low, a, z)), cast(jnp.where(low, z, ar))),
            (cast(jnp.where(low, ar, z)), cast(jnp.where(low, z, a))))


def _attn_kernel(*refs, latent, tq, seq):
    if latent:
        (sink_ref, q_ref, kp_ref, kc_ref, kn_ref, vp_ref, vc_ref, vn_ref,
         ck_ref, cv_ref, o_ref) = refs
        i = pl.program_id(1)
        k_loc = jnp.concatenate([kp_ref[...], kc_ref[...], kn_ref[...]], axis=0)
        v_loc = jnp.concatenate([vp_ref[...], vc_ref[...], vn_ref[...]], axis=0)
        nk = tq + 2 * WINDOW
        r = lax.broadcasted_iota(jnp.int32, (tq, nk), 0)
        j = lax.broadcasted_iota(jnp.int32, (tq, nk), 1)
        kpos = i * tq - WINDOW + j
        valid = (j >= r) & (j <= r + 2 * WINDOW) & (kpos >= 0) & (kpos < seq)
        kvar_c = _half_variants(ck_ref[0, 0])
        vvar_c = _half_variants(cv_ref[0, 0])
    else:
        sink_ref, q_ref, kc_ref, vc_ref, o_ref = refs
        k_loc = kc_ref[...]
        v_loc = vc_ref[...]
        valid = None
    kvar = _half_variants(k_loc)
    vvar = _half_variants(v_loc)
    for c in range(ATTN_HEADS // 2):
        kv = c // 2
        qc = q_ref[:, c * 128:(c + 1) * 128]
        acc = None
        for half in range(2):
            sink = sink_ref[2 * c + half]
            s = _dot_nt(qc, kvar[kv][half])
            if valid is not None:
                s = jnp.where(valid, s, NEG_INF)
            m = jnp.maximum(jnp.max(s, axis=-1, keepdims=True), sink)
            if latent:
                s_c = _dot_nt(qc, kvar_c[kv][half])
                m = jnp.maximum(m, jnp.max(s_c, axis=-1, keepdims=True))
            p = jnp.exp(s - m)
            denom = jnp.exp(sink - m) + jnp.sum(p, axis=-1, keepdims=True)
            o = _dot(p.astype(BF16), vvar[kv][half])
            if latent:
                p_c = jnp.exp(s_c - m)
                denom = denom + jnp.sum(p_c, axis=-1, keepdims=True)
                o = o + _dot(p_c.astype(BF16), vvar_c[kv][half])
            o = o * (1.0 / denom)
            acc = o if acc is None else acc + o
        o_ref[:, c * 128:(c + 1) * 128] = acc.astype(BF16)


def _attention(q, k, v, sink, batch, seq, tq, ctx_kv=None):
    latent = ctx_kv is not None
    nq = seq // tq
    kern = functools.partial(_attn_kernel, latent=latent, tq=tq, seq=seq)
    smem = pl.BlockSpec(memory_space=pltpu.SMEM)
    if latent:
        ck, cv, layer = ctx_kv
        past = ck.shape[2]
        wb = tq // WINDOW
        nwb = seq // WINDOW
        cur = pl.BlockSpec((tq, KV_WIDTH), lambda b, i: (b * nq + i, 0))
        prev = pl.BlockSpec((WINDOW, KV_WIDTH),
                            lambda b, i: (b * nwb + jnp.maximum(i * wb - 1, 0), 0))
        nxt = pl.BlockSpec((WINDOW, KV_WIDTH),
                           lambda b, i: (b * nwb + jnp.minimum((i + 1) * wb, nwb - 1), 0))
        cspec = pl.BlockSpec((1, 1, past, KV_WIDTH), lambda b, i: (b, layer, 0, 0))
        in_specs = [smem, pl.BlockSpec((tq, ATTN_WIDTH), lambda b, i: (b * nq + i, 0)),
                    prev, cur, nxt, prev, cur, nxt, cspec, cspec]
        args = (sink, q, k, k, k, v, v, v, ck, cv)
    else:
        cur = pl.BlockSpec((tq, KV_WIDTH), lambda b, i: (b * nq + i, 0))
        in_specs = [smem, pl.BlockSpec((tq, ATTN_WIDTH), lambda b, i: (b * nq + i, 0)), cur, cur]
        args = (sink, q, k, v)
    return pl.pallas_call(
        kern,
        out_shape=jax.ShapeDtypeStruct((batch * seq, ATTN_WIDTH), BF16),
        grid=(batch, nq),
        in_specs=in_specs,
        out_specs=pl.BlockSpec((tq, ATTN_WIDTH), lambda b, i: (b * nq + i, 0)),
        compiler_params=_params(2),
        name="attn_lat" if latent else "attn_ctx",
    )(*args)


def _gla_sweep(fwd, j, nt, tt, gin_ref, wg_ref, bg_ref, ng_ref, s0_ref,
               o_ref, sout_ref, st_scr, g_scr, ob_scr):
    w = GLA_WIDTH
    ck = GLA_CHUNK
    nc = tt // ck
    tile = j if fwd else nt - 1 - j

    @pl.when(j == 0)
    def _():
        st_scr[...] = s0_ref[0, 0]

    zpad = gin_ref[:, 4 * w:4 * w + GZ_PAD]
    zh = zpad.astype(BF16).astype(F32)
    zcat = zh + pltpu.roll(zpad - zh, 32, 1) + pltpu.roll(zh, 64, 1)
    logit = _dot(zcat.astype(BF16), wg_ref[0]) + bg_ref[0]
    g_scr[...] = _log_sigmoid(logit) * (1.0 / GLA_GATE_NORM)

    rr = lax.broadcasted_iota(jnp.int32, (w, w), 0) // GLA_DK
    cc = lax.broadcasted_iota(jnp.int32, (w, w), 1) // GLA_DK
    block_diag = rr == cc
    bd_bf = block_diag.astype(F32).astype(BF16)
    tr = lax.broadcasted_iota(jnp.int32, (ck, 2 * ck), 0)
    tc = lax.broadcasted_iota(jnp.int32, (ck, 2 * ck), 1) % ck
    tri2 = ((tr >= tc) if fwd else (tr <= tc)).astype(F32).astype(BF16)
    ar = lax.broadcasted_iota(jnp.int32, (ck, w), 0)
    ac = lax.broadcasted_iota(jnp.int32, (ck, w), 1) % ck
    causal = (ar >= ac) if fwd else (ar <= ac)

    for ci in range(nc):
        c0 = (ci if fwd else nc - 1 - ci) * ck
        rows = pl.ds(c0, ck)
        q = gin_ref[rows, 0:w] * (GLA_DK ** -0.5)
        k = gin_ref[rows, w:2 * w]
        v = gin_ref[rows, 2 * w:3 * w]
        g = g_scr[rows, :]
        st = st_scr[...]
        gh, gl = _split_bf16(g)
        b = _dot(tri2, jnp.concatenate([gh, gl], axis=0))
        btot = jnp.sum(g, axis=0, keepdims=True)
        q_in = (q * jnp.exp(b)).astype(BF16)
        k_in = (k * jnp.exp(-b)).astype(BF16)
        k_end = (k * jnp.exp(btot - b)).astype(BF16)
        vb = v.astype(BF16)
        k_bd = jnp.concatenate([k_in] * GLA_HEADS, axis=0) * bd_bf
        v_bd = jnp.concatenate([vb] * GLA_HEADS, axis=0) * bd_bf
        a = jnp.where(causal, _dot_nt(q_in, k_bd), 0.0)
        o = _dot(a.astype(BF16), v_bd) + _dot_nt(q_in, st.astype(BF16))
        delta_t = _dot(v.T.astype(BF16), k_end)
        st_scr[...] = st * jnp.exp(btot) + jnp.where(block_diag, delta_t, 0.0)
        all_rows = pl.ds(pl.multiple_of(tile * tt, ck) + c0, ck)
        if fwd:
            o = o + ob_scr[all_rows, :]
            ms = _dot((o * o).astype(BF16), bd_bf) * (1.0 / GLA_DK)
            r = gin_ref[rows, 3 * w:4 * w]
            o_ref[rows, :] = o * lax.rsqrt(ms + EPS) * ng_ref[...] * _silu(r)
        else:
            ob_scr[all_rows, :] = o

    @pl.when(j == nt - 1)
    def _():
        sout_ref[0, 0] = st_scr[...]


def _dot_exact_lhs_rhs(a, b_bf16):
    ah, al = _split_bf16(a)
    return _dot(ah, b_bf16) + _dot(al, b_bf16)


def _gla_kernel(gin_ref, wg_ref, bg_ref, ng_ref, s0_ref, o_ref, sout_ref,
                st_scr, g_scr, ob_scr, *, nt, tt):
    sweep = pl.program_id(1)
    j = pl.program_id(2)
    args = (j, nt, tt, gin_ref, wg_ref, bg_ref, ng_ref, s0_ref, o_ref, sout_ref,
            st_scr, g_scr, ob_scr)

    @pl.when(sweep == 0)
    def _():
        _gla_sweep(False, *args)

    @pl.when(sweep == 1)
    def _():
        _gla_sweep(True, *args)


def _gla(gin, wg, bg, norm_g, s0, batch, seq, tt):
    nt = seq // tt
    w = GLA_WIDTH
    per_batch_state = s0.shape[0] == batch
    tile_of = lambda s, j: j * (2 * s - 1) + (1 - s) * (nt - 1)
    return pl.pallas_call(
        functools.partial(_gla_kernel, nt=nt, tt=tt),
        out_shape=[jax.ShapeDtypeStruct((batch * seq, w), F32),
                   jax.ShapeDtypeStruct((batch, 2, w, w), F32)],
        grid=(batch, 2, nt),
        in_specs=[
            pl.BlockSpec((tt, GLA_IN), lambda b, s, j: (b * nt + tile_of(s, j), 0)),
            pl.BlockSpec((1, GZ_PAD, w), lambda b, s, j: (s, 0, 0)),
            pl.BlockSpec((1, 1, w), lambda b, s, j: (s, 0, 0)),
            _const_spec((1, w)),
            pl.BlockSpec((1, 1, w, w),
                         lambda b, s, j: (b if per_batch_state else 0, s, 0, 0)),
        ],
        out_specs=[
            pl.BlockSpec((tt, w), lambda b, s, j: (b * nt + j * s, 0)),
            pl.BlockSpec((1, 1, w, w), lambda b, s, j: (b, s, 0, 0)),
        ],
        scratch_shapes=[pltpu.VMEM((w, w), F32), pltpu.VMEM((tt, w), F32),
                        pltpu.VMEM((seq, w), F32)],
        compiler_params=_params(3),
        name="gla",
    )(gin, wg, bg, norm_g.reshape(1, w), s0)


def _lru_sweep(fwd, j, nt, tt, xp_ref, xc_ref, xn_ref, lg_ref, cw_ref, cb_ref,
               wg_ref, bg_ref, lam_ref, h0_ref, o_ref, sout_ref,
               carry_scr, a_scr, ab_scr, hl_scr, pl_scr, h_scr, hb_scr):
    w = LRU_WIDTH
    seg_len = tt // 8
    tile = j if fwd else nt - 1 - j

    @pl.when(j == 0)
    def _():
        carry_scr[...] = jnp.broadcast_to(h0_ref[0, 0], (8, w))

    pre = jnp.where(tile > 0, xp_ref[...], 0.0)
    post = jnp.where(tile < nt - 1, xn_ref[...], 0.0)
    xe = jnp.concatenate([pre, xc_ref[...], post], axis=0)
    ne = tt + 16
    xc = cb_ref[...] + cw_ref[2:3, :] * xe[8:8 + tt]
    xc = xc + cw_ref[0:1, :] * pltpu.roll(xe, 2, 0)[8:8 + tt]
    xc = xc + cw_ref[1:2, :] * pltpu.roll(xe, 1, 0)[8:8 + tt]
    xc = xc + cw_ref[3:4, :] * pltpu.roll(xe, ne - 1, 0)[8:8 + tt]

    ri = _dot(xc.astype(BF16), wg_ref[0]) + bg_ref[0]
    r = jax.nn.sigmoid(ri[:, 0:w])
    gate_i = jax.nn.sigmoid(ri[:, w:2 * w])
    log_a = (-LRU_C) * r * _softplus(-lam_ref[0])
    a = jnp.exp(log_a)
    bt = jnp.sqrt(1.0 - a * a) * (gate_i * xc)

    nl = w // 128
    pitch = seg_len + LRU_SEG_PAD
    for lt in range(nl):
        for k in range(8):
            dst = slice(k * pitch, k * pitch + seg_len)
            src = slice(k * seg_len, (k + 1) * seg_len)
            a_scr[lt, dst, :] = a[src, lt * 128:(lt + 1) * 128]
            ab_scr[lt, dst, :] = bt[src, lt * 128:(lt + 1) * 128]

    def step(ii, hp):
        i = ii if fwd else seg_len - 1 - ii
        seg_rows = pl.ds(i, 8, stride=pitch)
        rows = pl.ds(pl.multiple_of(i * 8, 8), 8)
        out = []
        for lt in range(nl):
            h, p = hp[lt]
            a_i = a_scr[lt, seg_rows, :]
            h = a_i * h + ab_scr[lt, seg_rows, :]
            p = a_i * p
            hl_scr[lt, rows, :] = h
            pl_scr[lt, rows, :] = p
            out.append((h, p))
        return tuple(out)

    init = tuple((jnp.zeros((8, 128), F32), jnp.ones((8, 128), F32)) for _ in range(nl))
    ends = lax.fori_loop(0, seg_len, step, init, unroll=8)
    h_end = jnp.concatenate([e[0] for e in ends], axis=1)
    p_end = jnp.concatenate([e[1] for e in ends], axis=1)

    c = carry_scr[0:1, :]
    c_in = [None] * 8
    for k in (range(8) if fwd else range(7, -1, -1)):
        c_in[k] = c
        c = h_end[k:k + 1, :] + p_end[k:k + 1, :] * c
    carry_scr[...] = jnp.broadcast_to(c, (8, w))

    for k in range(8):
        for i0 in range(0, seg_len, 8):
            src = pl.ds(i0 * 8 + k, 8, stride=8)
            r0 = k * seg_len + i0
            for lt in range(nl):
                c_k = c_in[k][:, lt * 128:(lt + 1) * 128]
                h_scr[lt, r0:r0 + 8, :] = hl_scr[lt, src, :] + pl_scr[lt, src, :] * c_k
    h_all = jnp.concatenate([h_scr[lt] for lt in range(nl)], axis=1)
    all_rows = pl.ds(pl.multiple_of(tile * tt, 8), tt)
    if fwd:
        o_ref[...] = (h_all + hb_scr[all_rows, :]) * _gelu_tanh(lg_ref[...])
    else:
        hb_scr[all_rows, :] = h_all

    @pl.when(j == nt - 1)
    def _():
        sout_ref[0, 0] = carry_scr[...]


def _lru_kernel(xp_ref, xc_ref, xn_ref, lg_ref, cw_ref, cb_ref, wg_ref, bg_ref,
                lam_ref, h0_ref, o_ref, sout_ref, carry_scr, a_scr, ab_scr, hl_scr, pl_scr, h_scr, hb_scr,
                *, nt, tt):
    sweep = pl.program_id(1)
    j = pl.program_id(2)
    args = (j, nt, tt, xp_ref, xc_ref, xn_ref, lg_ref, cw_ref, cb_ref, wg_ref, bg_ref,
            lam_ref, h0_ref, o_ref, sout_ref, carry_scr, a_scr, ab_scr, hl_scr, pl_scr, h_scr, hb_scr)

    @pl.when(sweep == 0)
    def _():
        _lru_sweep(False, *args)

    @pl.when(sweep == 1)
    def _():
        _lru_sweep(True, *args)


def _lru(lx, lg, conv_w, conv_b, wg, bg, lam, h0, batch, seq, tt):
    nt = seq // tt
    w = LRU_WIDTH
    t8 = tt // 8
    n8 = seq // 8
    per_batch_state = h0.shape[0] == batch
    tile_of = lambda s, j: j * (2 * s - 1) + (1 - s) * (nt - 1)
    return pl.pallas_call(
        functools.partial(_lru_kernel, nt=nt, tt=tt),
        out_shape=[jax.ShapeDtypeStruct((batch * seq, w), F32),
                   jax.ShapeDtypeStruct((batch, 2, 8, w), F32)],
        grid=(batch, 2, nt),
        in_specs=[
            pl.BlockSpec((8, w), lambda b, s, j:
                         (b * n8 + jnp.maximum(tile_of(s, j) * t8 - 1, 0), 0)),
            pl.BlockSpec((tt, w), lambda b, s, j: (b * nt + tile_of(s, j), 0)),
            pl.BlockSpec((8, w), lambda b, s, j:
                         (b * n8 + jnp.minimum((tile_of(s, j) + 1) * t8, n8 - 1), 0)),
            pl.BlockSpec((tt, w), lambda b, s, j: (b * nt + j * s, 0)),
            _const_spec((4, w)),
            _const_spec((1, w)),
            pl.BlockSpec((1, w, 2 * w), lambda b, s, j: (s, 0, 0)),
            pl.BlockSpec((1, 1, 2 * w), lambda b, s, j: (s, 0, 0)),
            pl.BlockSpec((1, 1, w), lambda b, s, j: (s, 0, 0)),
            pl.BlockSpec((1, 1, 1, w),
                         lambda b, s, j: (b if per_batch_state else 0, s, 0, 0)),
        ],
        out_specs=[
            pl.BlockSpec((tt, w), lambda b, s, j: (b * nt + j * s, 0)),
            pl.BlockSpec((1, 1, 8, w), lambda b, s, j: (b, s, 0, 0)),
        ],
        scratch_shapes=[pltpu.VMEM((8, w), F32)]
        + [pltpu.VMEM((w // 128, tt + 8 * LRU_SEG_PAD, 128), F32)] * 2
        + [pltpu.VMEM((w // 128, tt, 128), F32)] * 3
        + [pltpu.VMEM((seq, w), F32)],
        compiler_params=_params(3),
        name="lru",
    )(lx, lx, lx, lg, conv_w, conv_b.reshape(1, w), wg, bg, lam, h0)


def _mlp_kernel(x_ref, att_ref, gla_ref, lru_ref, mod_ref, n2_ref, fn_ref,
                wo_ref, w1_ref, w2_ref, o_ref, *, final, ff_chunk):
    d = D_MODEL
    mix = _dot(att_ref[...], wo_ref[0:ATTN_WIDTH, :])
    mix = mix + _dot(gla_ref[...].astype(BF16), wo_ref[ATTN_WIDTH:ATTN_WIDTH + GLA_WIDTH, :])
    mix = mix + _dot(lru_ref[...].astype(BF16), wo_ref[ATTN_WIDTH + GLA_WIDTH:d, :])
    g1 = mod_ref[0, :, 2 * d:3 * d]
    sh2 = mod_ref[0, :, 3 * d:4 * d]
    sc2 = mod_ref[0, :, 4 * d:5 * d]
    g2 = mod_ref[0, :, 5 * d:6 * d]
    x = x_ref[...] + g1 * mix
    ms = jnp.mean(x * x, axis=-1, keepdims=True)
    h = (x * lax.rsqrt(ms + EPS) * n2_ref[...] * (1.0 + sc2) + sh2).astype(BF16)
    y = None
    for c in range(D_FF // ff_chunk):
        cols = slice(c * ff_chunk, (c + 1) * ff_chunk)
        u = jnp.maximum(_dot(h, w1_ref[:, cols]), 0.0)
        part = _dot((u * u).astype(BF16), w2_ref[cols, :])
        y = part if y is None else y + part
    x = x + g2 * y
    if final:
        ms = jnp.mean(x * x, axis=-1, keepdims=True)
        x = x * lax.rsqrt(ms + EPS) * fn_ref[...]
    o_ref[...] = x


def _out_mlp(x2d, att, gla, lru, mod_l, norm2, final_norm, wo, w1, w2,
             rows_per_mod, mod_row0, tt, final):
    n, d = x2d.shape
    nt = n // tt
    mod_map = lambda i: (mod_row0 + (i * tt) // rows_per_mod, 0, 0)
    row = lambda wd: pl.BlockSpec((tt, wd), lambda i: (i, 0))
    resident = lambda shape: pl.BlockSpec(shape, lambda i: (0, 0),
                                          pipeline_mode=pl.Buffered(1))
    return pl.pallas_call(
        functools.partial(_mlp_kernel, final=final, ff_chunk=1024),
        out_shape=jax.ShapeDtypeStruct((n, d), F32),
        grid=(nt,),
        in_specs=[row(d), row(ATTN_WIDTH), row(GLA_WIDTH), row(LRU_WIDTH),
                  pl.BlockSpec((1, 1, 6 * d), mod_map),
                  _const_spec((1, d)), _const_spec((1, d)),
                  resident((d, d)), resident((d, D_FF)), resident((D_FF, d))],
        out_specs=row(d),
        compiler_params=_params(1),
        name="out_mlp",
    )(x2d, att, gla, lru, mod_l.reshape(MOD_ROWS, 1, 6 * d), norm2.reshape(1, d),
      final_norm.reshape(1, d), wo, w1, w2)


def _prep_w_in(w_in_l):
    d = w_in_l.shape[0]
    o = 0
    parts = {}
    for name, size in (("q", 512), ("k", 128), ("v", 128), ("gq", 256), ("gk", 256),
                       ("gv", 256), ("gz", 32), ("gr", 256), ("lx", 256), ("lg", 256)):
        parts[name] = w_in_l[:, o:o + size]
        o += size
    gz = jnp.concatenate([parts["gz"], jnp.zeros((d, GZ_PAD - 32), w_in_l.dtype)], axis=1)
    cols = [parts["q"], parts["k"], parts["v"], parts["gq"], parts["gk"], parts["gv"],
            parts["gr"], gz, parts["lx"], parts["lg"]]
    return jnp.concatenate(cols, axis=1).astype(BF16)


def _prep_gate_w(gw):
    out = []
    nz = 2 * GLA_GATE_RANK
    for dr in (1, 0):
        wf = jnp.zeros((nz, GLA_WIDTH), F32)
        wf = wf.at[dr * GLA_GATE_RANK:(dr + 1) * GLA_GATE_RANK].set(gw[dr])
        hi, lo = _split_bf16(wf)
        out.append(jnp.concatenate([hi, hi, lo, jnp.zeros((GZ_PAD - 3 * nz, GLA_WIDTH), BF16)],
                                   axis=0))
    return jnp.stack(out)


def _block_diag(w4):
    n, a, b = w4.shape
    eye = jnp.eye(n, dtype=w4.dtype)
    return jnp.einsum("nab,nm->namb", w4, eye).reshape(n * a, n * b)


def _rope_tables(seq):
    rows = seq // GRID_W
    row = jnp.repeat(jnp.arange(rows), GRID_W).astype(F32)
    col = jnp.tile(jnp.arange(GRID_W), rows).astype(F32)
    nf = HEAD_DIM // 4
    inv = ROPE_BASE ** (-jnp.arange(nf, dtype=F32) / nf)
    ang_r = row[:, None] * inv
    ang_c = col[:, None] * inv
    cos = jnp.concatenate([jnp.cos(ang_r)] * 2 + [jnp.cos(ang_c)] * 2, axis=1)
    sin = jnp.concatenate([-jnp.sin(ang_r), jnp.sin(ang_r),
                           -jnp.sin(ang_c), jnp.sin(ang_c)], axis=1)
    return jnp.concatenate([cos] * 2, axis=1), jnp.concatenate([sin] * 2, axis=1)


def _gla_state_to_internal(s):
    b = s.shape[0]
    st = jnp.swapaxes(s, -1, -2)
    eye = jnp.eye(GLA_HEADS, dtype=s.dtype)
    full = jnp.einsum("bshvd,hg->bshvgd", st, eye)
    return full.reshape(b, 2, GLA_WIDTH, GLA_WIDTH)


def _gla_state_from_internal(st):
    b = st.shape[0]
    s6 = st.reshape(b, 2, GLA_HEADS, GLA_DK, GLA_HEADS, GLA_DK)
    diag = jnp.stack([s6[:, :, h, :, h, :] for h in range(GLA_HEADS)], axis=2)
    return jnp.swapaxes(diag, -1, -2)


def kernel(x_prompt, x_sample, c, cache_k, cache_v, state_gla, state_lru, c_ctx, w_mod, b_mod, norm1, norm2, w_in, attn_sink, gla_gate_w, gla_gate_b, gla_norm, lru_conv_w, lru_conv_b, lru_wa, lru_ba, lru_wx, lru_bx, lru_lambda, w_out, w_mlp1, w_mlp2, final_norm):
    depth = w_in.shape[0]
    bc, sc_len, d = x_prompt.shape
    bl, sl_len, _ = x_sample.shape
    past = cache_k.shape[2]

    cond = jnp.concatenate([c_ctx[None], c, jnp.zeros((MOD_ROWS - 1 - bl, d), F32)], axis=0)
    mod = _ada_mod(cond, w_mod, b_mod)

    flip_dir = lambda a: a[:, ::-1]
    rope_tabs = _rope_tables(sl_len)
    ck = cache_k.reshape(bl, depth, past, KV_WIDTH)
    cv = cache_v.reshape(bl, depth, past, KV_WIDTH)
    gla_s0_ctx = jnp.zeros((1, 2, GLA_WIDTH, GLA_WIDTH), F32)
    lru_s0_ctx = jnp.zeros((1, 2, 1, LRU_WIDTH), F32)

    xp = x_prompt.reshape(bc * sc_len, d)
    xs = x_sample.reshape(bl * sl_len, d)
    ks, vs, sgs, sls = [], [], [], []
    for l in range(depth):
        w_in_l = _prep_w_in(w_in[l])
        wo = w_out[l].astype(BF16)
        w1 = w_mlp1[l].astype(BF16)
        w2 = w_mlp2[l].astype(BF16)
        gate_w = _prep_gate_w(gla_gate_w[l])
        gate_b = gla_gate_b[l][::-1].reshape(2, 1, GLA_WIDTH)
        lru_wg = jnp.stack([jnp.concatenate([_block_diag(lru_wa[l, dr]), _block_diag(lru_wx[l, dr])],
                                            axis=1) for dr in (1, 0)]).astype(BF16)
        lru_bg = jnp.stack([jnp.concatenate([lru_ba[l, dr], lru_bx[l, dr]]) for dr in (1, 0)]
                           ).reshape(2, 1, 2 * LRU_WIDTH)
        lam = lru_lambda[l][::-1].reshape(2, 1, LRU_WIDTH)
        final = l == depth - 1

        for is_lat in (False, True):
            if is_lat:
                x2d, batch, seq, row0, rpm = xs, bl, sl_len, 1, sl_len
                tabs, tt_proj, tq, tt_rec = rope_tabs, 512, 256, 512
                gla_s0 = flip_dir(_gla_state_to_internal(state_gla[:, l]))
                lru_s0 = flip_dir(state_lru[:, l]).reshape(bl, 2, 1, LRU_WIDTH)
            else:
                x2d, batch, seq, row0, rpm = xp, bc, sc_len, 0, bc * sc_len
                tabs, tt_proj, tq, tt_rec = None, 512, sc_len, sc_len
                gla_s0, lru_s0 = gla_s0_ctx, lru_s0_ctx
            q, k, v, gin, lx, lg = _in_proj(x2d, mod[l], norm1[l], w_in_l, tabs, rpm, row0, tt_proj)
            att = _attention(q, k, v, attn_sink[l], batch, seq, tq,
                             ctx_kv=(ck, cv, l) if is_lat else None)
            gla, sg = _gla(gin, gate_w, gate_b, gla_norm[l], gla_s0, batch, seq, tt_rec)
            lru, sl = _lru(lx, lg, lru_conv_w[l], lru_conv_b[l], lru_wg, lru_bg, lam, lru_s0,
                           batch, seq, tt_rec)
            x2d = _out_mlp(x2d, att, gla, lru, mod[l], norm2[l], final_norm, wo, w1, w2,
                           rpm, row0, 512, final)
            if is_lat:
                xs = x2d
            else:
                xp = x2d
                ks.append(k.reshape(bc, sc_len, 2, HEAD_DIM))
                vs.append(v.reshape(bc, sc_len, 2, HEAD_DIM))
                sgs.append(_gla_state_from_internal(flip_dir(sg)))
                sls.append(flip_dir(sl[:, :, 0, :]))

    y_prompt = xp.reshape(bc, sc_len, d)
    y_sample = xs.reshape(bl, sl_len, d)
    return (y_prompt, y_sample, jnp.stack(ks, axis=1), jnp.stack(vs, axis=1),
            jnp.stack(sgs, axis=1), jnp.stack(sls, axis=1))
```

```python
import functools

import jax
import jax.numpy as jnp
from jax import lax
from jax.experimental import pallas as pl
from jax.experimental.pallas import tpu as pltpu

F32 = jnp.float32
BF16 = jnp.bfloat16

D_MODEL = 1024
GRID_W = 64
EPS = 1e-6
HEAD_DIM = 64
ATTN_WIDTH = 512
ATTN_HEADS = 8
KV_WIDTH = 128
WINDOW = 128
ROPE_BASE = 10000.0
NEG_INF = -1e30
GLA_WIDTH = 256
GLA_DK = 64
GLA_HEADS = 4
GLA_GATE_RANK = 16
GLA_GATE_NORM = 16.0
GLA_CHUNK = 64
LRU_WIDTH = 256
LRU_C = 8.0
LRU_SEG_PAD = 8
D_FF = 4096
MOD_ROWS = 8
GZ_PAD = 128
C_Q, C_K, C_V, C_GLA, C_LX, C_LG, C_END = 0, 512, 640, 768, 1920, 2176, 2432
GLA_IN = C_LX - C_GLA
VMEM_LIMIT = 56 * 1024 * 1024

NT_DIMS = (((1,), (1,)), ((), ()))


def _split_bf16(a):
    hi = a.astype(BF16)
    lo = (a - hi.astype(F32)).astype(BF16)
    return hi, lo


def _dot(a, b):
    return jnp.dot(a, b, preferred_element_type=F32)


def _dot_nt(a, b):
    return lax.dot_general(a, b, NT_DIMS, preferred_element_type=F32)


def _dot_x3(a, b):
    ah, al = _split_bf16(a)
    bh, bl = _split_bf16(b)
    return _dot(ah, bh) + (_dot(ah, bl) + _dot(al, bh))


def _softplus(y):
    return jnp.maximum(y, 0.0) + jnp.log1p(jnp.exp(-jnp.abs(y)))


def _log_sigmoid(y):
    return jnp.minimum(y, 0.0) - jnp.log(1.0 + jnp.exp(-jnp.abs(y)))


def _sigmoid(y):
    return 0.5 * jnp.tanh(0.5 * y) + 0.5


def _silu(y):
    return y * _sigmoid(y)


def _gelu_tanh(y):
    c = 0.7978845608028654
    return 0.5 * y * (1.0 + jnp.tanh(c * (y + 0.044715 * (y * y * y))))


def _params(n_axes):
    return pltpu.CompilerParams(
        dimension_semantics=("arbitrary",) * n_axes, vmem_limit_bytes=VMEM_LIMIT)


def _const_spec(shape):
    nd = len(shape)
    return pl.BlockSpec(shape, lambda *_: (0,) * nd)


def _ada_kernel(c_ref, w_ref, b_ref, o_ref):
    s = _silu(c_ref[...])
    res = _dot_x3(s, w_ref[0]) + b_ref[0]
    for r in range(MOD_ROWS):
        o_ref[0, r] = res[r:r + 1, :]


def _ada_mod(cond, w_mod, b_mod):
    depth, d, n = w_mod.shape
    tn = 1536
    return pl.pallas_call(
        _ada_kernel,
        out_shape=jax.ShapeDtypeStruct((depth, MOD_ROWS, 1, n), F32),
        grid=(depth, n // tn),
        in_specs=[
            pl.BlockSpec((MOD_ROWS, d), lambda l, j: (0, 0)),
            pl.BlockSpec((1, d, tn), lambda l, j: (l, 0, j)),
            pl.BlockSpec((1, 1, tn), lambda l, j: (l, 0, j)),
        ],
        out_specs=pl.BlockSpec((1, MOD_ROWS, 1, tn), lambda l, j: (l, 0, 0, j)),
        compiler_params=_params(2),
        name="ada_mod",
    )(cond, w_mod, b_mod.reshape(depth, 1, n))


def _rope(x, cos, sin_signed):
    w = x.shape[1]
    lane = lax.broadcasted_iota(jnp.int32, (1, w), 1)
    first = (lane % 32) < 16
    swapped = jnp.where(first, pltpu.roll(x, w - 16, 1), pltpu.roll(x, 16, 1))
    return x * cos + swapped * sin_signed


def _inproj_kernel(*refs, rope):
    if rope:
        (x_ref, mod_ref, n1_ref, w_ref, cos_ref, sin_ref,
         q_ref, k_ref, v_ref, gla_ref, lx_ref, lg_ref) = refs
    else:
        (x_ref, mod_ref, n1_ref, w_ref,
         q_ref, k_ref, v_ref, gla_ref, lx_ref, lg_ref) = refs
    d = D_MODEL
    x = x_ref[...]
    ms = jnp.mean(x * x, axis=-1, keepdims=True)
    xn = x * lax.rsqrt(ms + EPS) * n1_ref[...]
    sh = mod_ref[0, 0, :, 0:d]
    sc = mod_ref[0, 0, :, d:2 * d]
    h = (xn * (1.0 + sc) + sh).astype(BF16)
    p = _dot(h, w_ref[0])
    q = p[:, C_Q:C_K]
    k = p[:, C_K:C_V]
    v = p[:, C_V:C_GLA]
    if rope:
        cos = cos_ref[...]
        sin = sin_ref[...]
        k = _rope(k, cos, sin)
        q = _rope(q, jnp.concatenate([cos] * 4, axis=1), jnp.concatenate([sin] * 4, axis=1))
    q_ref[...] = (q * (HEAD_DIM ** -0.5)).astype(BF16)
    k_ref[...] = k
    v_ref[...] = v
    gla_ref[...] = p[:, C_GLA:C_LX]
    lx_ref[...] = p[:, C_LX:C_LG]
    lg_ref[...] = p[:, C_LG:C_END]


def _in_proj(x2d, mod, norm1, w, layer, rope_tabs, rows_per_mod, mod_row0, tt):
    n, d = x2d.shape
    nt = n // tt
    rope = rope_tabs is not None
    mod_map = lambda i: (layer, mod_row0 + (i * tt) // rows_per_mod, 0, 0)
    in_specs = [
        pl.BlockSpec((tt, d), lambda i: (i, 0)),
        pl.BlockSpec((1, 1, 1, 6 * d), mod_map),
        _const_spec((1, d)),
        pl.BlockSpec((1, d, C_END), lambda i: (layer, 0, 0)),
    ]
    args = [x2d, mod, norm1.reshape(1, d), w]
    if rope:
        t = rope_tabs[0].shape[0]
        tpb = t // tt
        in_specs += [pl.BlockSpec((tt, 128), lambda i: (i % tpb, 0))] * 2
        args += list(rope_tabs)
    widths = (ATTN_WIDTH, KV_WIDTH, KV_WIDTH, GLA_IN, LRU_WIDTH, LRU_WIDTH)
    dtypes = (BF16, F32, F32, F32, F32, F32)
    return pl.pallas_call(
        functools.partial(_inproj_kernel, rope=rope),
        out_shape=[jax.ShapeDtypeStruct((n, wd), dt) for wd, dt in zip(widths, dtypes)],
        grid=(nt,),
        in_specs=in_specs,
        out_specs=[pl.BlockSpec((tt, wd), lambda i: (i, 0)) for wd in widths],
        compiler_params=_params(1),
        name="in_proj_lat" if rope else "in_proj_ctx",
    )(*args)


def _half_variants(a):
    lane = lax.broadcasted_iota(jnp.int32, (1, 128), 1)
    low = lane < HEAD_DIM
    ar = pltpu.roll(a, HEAD_DIM, 1)
    z = jnp.zeros_like(a)
    cast = lambda t: t.astype(BF16)
    return ((cast(jnp.where(low, a, z)), cast(jnp.where(low, z, ar))),
            (cast(jnp.where(low, ar, z)), cast(jnp.where(low, z, a))))


def _attn_kernel(*refs, latent, tq, seq):
    if latent:
        (sink_ref, q_ref, kp_ref, kc_ref, kn_ref, vp_ref, vc_ref, vn_ref,
         ck_ref, cv_ref, o_ref) = refs
        i = pl.program_id(1)
        k_loc = jnp.concatenate([kp_ref[...], kc_ref[...], kn_ref[...]], axis=0)
        v_loc = jnp.concatenate([vp_ref[...], vc_ref[...], vn_ref[...]], axis=0)
        nk = tq + 2 * WINDOW
        r = lax.broadcasted_iota(jnp.int32, (tq, nk), 0)
        j = lax.broadcasted_iota(jnp.int32, (tq, nk), 1)
        kpos = i * tq - WINDOW + j
        valid = (j >= r) & (j <= r + 2 * WINDOW) & (kpos >= 0) & (kpos < seq)
        kvar_c = _half_variants(ck_ref[0, 0])
        vvar_c = _half_variants(cv_ref[0, 0])
    else:
        sink_ref, q_ref, kc_ref, vc_ref, o_ref = refs
        k_loc = kc_ref[...]
        v_loc = vc_ref[...]
        valid = None
    kvar = _half_variants(k_loc)
    with_ones = lambda t: jnp.concatenate([t, jnp.ones_like(t)], axis=1)
    vvar = [[with_ones(t) for t in pair] for pair in _half_variants(v_loc)]
    if latent:
        vvar_c = [[with_ones(t) for t in pair] for pair in vvar_c]
    for c in range(ATTN_HEADS // 2):
        kv = c // 2
        qc = q_ref[:, c * 128:(c + 1) * 128]
        acc = None
        for half in range(2):
            sink = sink_ref[2 * c + half]
            s = _dot_nt(qc, kvar[kv][half])
            if valid is not None:
                s = jnp.where(valid, s, NEG_INF)
            m = jnp.maximum(jnp.max(s, axis=-1, keepdims=True), sink)
            if latent:
                s_c = _dot_nt(qc, kvar_c[kv][half])
                m = jnp.maximum(m, jnp.max(s_c, axis=-1, keepdims=True))
            o = _dot(jnp.exp(s - m).astype(BF16), vvar[kv][half])
            if latent:
                o = o + _dot(jnp.exp(s_c - m).astype(BF16), vvar_c[kv][half])
            denom = jnp.exp(sink - m) + o[:, 128:256]
            o = o[:, 0:128] * (1.0 / denom)
            acc = o if acc is None else acc + o
        o_ref[:, c * 128:(c + 1) * 128] = acc.astype(BF16)


def _attention(q, k, v, sink, batch, seq, tq, ctx_kv=None):
    latent = ctx_kv is not None
    nq = seq // tq
    kern = functools.partial(_attn_kernel, latent=latent, tq=tq, seq=seq)
    smem = pl.BlockSpec(memory_space=pltpu.SMEM)
    if latent:
        ck, cv, layer = ctx_kv
        past = ck.shape[2]
        wb = tq // WINDOW
        nwb = seq // WINDOW
        cur = pl.BlockSpec((tq, KV_WIDTH), lambda b, i: (b * nq + i, 0))
        prev = pl.BlockSpec((WINDOW, KV_WIDTH),
                            lambda b, i: (b * nwb + jnp.maximum(i * wb - 1, 0), 0))
        nxt = pl.BlockSpec((WINDOW, KV_WIDTH),
                           lambda b, i: (b * nwb + jnp.minimum((i + 1) * wb, nwb - 1), 0))
        cspec = pl.BlockSpec((1, 1, past, KV_WIDTH), lambda b, i: (b, layer, 0, 0))
        in_specs = [smem, pl.BlockSpec((tq, ATTN_WIDTH), lambda b, i: (b * nq + i, 0)),
                    prev, cur, nxt, prev, cur, nxt, cspec, cspec]
        args = (sink, q, k, k, k, v, v, v, ck, cv)
    else:
        cur = pl.BlockSpec((tq, KV_WIDTH), lambda b, i: (b * nq + i, 0))
        in_specs = [smem, pl.BlockSpec((tq, ATTN_WIDTH), lambda b, i: (b * nq + i, 0)), cur, cur]
        args = (sink, q, k, v)
    return pl.pallas_call(
        kern,
        out_shape=jax.ShapeDtypeStruct((batch * seq, ATTN_WIDTH), BF16),
        grid=(batch, nq),
        in_specs=in_specs,
        out_specs=pl.BlockSpec((tq, ATTN_WIDTH), lambda b, i: (b * nq + i, 0)),
        compiler_params=_params(2),
        name="attn_lat" if latent else "attn_ctx",
    )(*args)


def _gla_sweep(fwd, j, nt, tt, gin_ref, wg_ref, bg_ref, ng_ref, s0_ref, o_ref, sout_ref,
               st_scr, g_scr, ob_scr, qin_scr, kin_scr, kend_scr, a_scr, stc_scr, dec_scr):
    w = GLA_WIDTH
    ck = GLA_CHUNK
    nc = tt // ck
    tile = j if fwd else nt - 1 - j
    starts = [(ci if fwd else nc - 1 - ci) * ck for ci in range(nc)]

    rr = lax.broadcasted_iota(jnp.int32, (w, w), 0) // GLA_DK
    cc = lax.broadcasted_iota(jnp.int32, (w, w), 1) // GLA_DK
    block_diag = rr == cc
    bd_bf = block_diag.astype(F32).astype(BF16)
    tr = lax.broadcasted_iota(jnp.int32, (ck, 2 * ck), 0)
    tc = lax.broadcasted_iota(jnp.int32, (ck, 2 * ck), 1) % ck
    tri2 = ((tr >= tc) if fwd else (tr <= tc)).astype(F32).astype(BF16)
    ar = lax.broadcasted_iota(jnp.int32, (ck, w), 0)
    ac = lax.broadcasted_iota(jnp.int32, (ck, w), 1) % ck
    causal = (ar >= ac) if fwd else (ar <= ac)

    @pl.when(j == 0)
    def _():
        s0 = s0_ref[0, 0]
        st_scr[...] = jnp.where(block_diag, jnp.concatenate([s0] * GLA_HEADS, axis=0), 0.0)

    zpad = gin_ref[:, 4 * w:4 * w + GZ_PAD]
    zh = zpad.astype(BF16).astype(F32)
    zcat = zh + pltpu.roll(zpad - zh, 32, 1) + pltpu.roll(zh, 64, 1)
    logit = _dot(zcat.astype(BF16), wg_ref[0]) + bg_ref[0]
    g_scr[...] = _log_sigmoid(logit) * (1.0 / GLA_GATE_NORM)

    for ci, c0 in enumerate(starts):
        rows = pl.ds(c0, ck)
        g = g_scr[rows, :]
        gh, gl = _split_bf16(g)
        b = _dot(tri2, jnp.concatenate([gh, gl], axis=0))
        btot = jnp.sum(g, axis=0, keepdims=True)
        k = gin_ref[rows, w:2 * w]
        qin_scr[rows, :] = (gin_ref[rows, 0:w] * (GLA_DK ** -0.5) * jnp.exp(b)).astype(BF16)
        kin_scr[rows, :] = (k * jnp.exp(-b)).astype(BF16)
        kend_scr[rows, :] = (k * jnp.exp(btot - b)).astype(BF16)
        dec_scr[ci] = jnp.broadcast_to(jnp.exp(btot), (8, w))

    for c0 in starts:
        rows = pl.ds(c0, ck)
        k_bd = jnp.concatenate([kin_scr[rows, :]] * GLA_HEADS, axis=0) * bd_bf
        a_scr[rows, :] = jnp.where(causal, _dot_nt(qin_scr[rows, :], k_bd), 0.0).astype(BF16)

    for ci, c0 in enumerate(starts):
        rows = pl.ds(c0, ck)
        st = st_scr[...]
        stc_scr[ci] = st.astype(BF16)
        delta_t = _dot(gin_ref[rows, 2 * w:3 * w].T.astype(BF16), kend_scr[rows, :])
        st_scr[...] = st * dec_scr[ci][0:1, :] + jnp.where(block_diag, delta_t, 0.0)

    tile_rows = pl.multiple_of(tile * tt, ck)
    for ci, c0 in enumerate(starts):
        rows = pl.ds(c0, ck)
        v_bd = jnp.concatenate([gin_ref[rows, 2 * w:3 * w].astype(BF16)] * GLA_HEADS,
                               axis=0) * bd_bf
        o = _dot(a_scr[rows, :], v_bd) + _dot_nt(qin_scr[rows, :], stc_scr[ci])
        if fwd:
            o_ref[rows, :] = o
        else:
            ob_scr[pl.ds(tile_rows + c0, ck), :] = o

    if fwd:
        o = o_ref[...] + ob_scr[pl.ds(tile_rows, tt), :]
        ms = _dot((o * o).astype(BF16), bd_bf) * (1.0 / GLA_DK)
        r = gin_ref[:, 3 * w:4 * w]
        o_ref[...] = o * lax.rsqrt(ms + EPS) * ng_ref[...] * _silu(r)

    @pl.when(j == nt - 1)
    def _():
        st = st_scr[...]
        sout_ref[0, 0] = sum(st[h * GLA_DK:(h + 1) * GLA_DK, :] for h in range(1, GLA_HEADS)) \
            + st[0:GLA_DK, :]


def _gla_kernel(gin_ref, wg_ref, bg_ref, ng_ref, s0_ref, o_ref, sout_ref,
                st_scr, g_scr, ob_scr, qin_scr, kin_scr, kend_scr, a_scr, stc_scr, dec_scr,
                *, nt, tt):
    sweep = pl.program_id(1)
    j = pl.program_id(2)
    args = (j, nt, tt, gin_ref, wg_ref, bg_ref, ng_ref, s0_ref, o_ref, sout_ref,
            st_scr, g_scr, ob_scr, qin_scr, kin_scr, kend_scr, a_scr, stc_scr, dec_scr)

    @pl.when(sweep == 0)
    def _():
        _gla_sweep(False, *args)

    @pl.when(sweep == 1)
    def _():
        _gla_sweep(True, *args)


def _gla(gin, wg, bg, norm_g, s0, batch, seq, tt):
    nt = seq // tt
    w = GLA_WIDTH
    per_batch_state = s0.shape[0] == batch
    tile_of = lambda s, j: j * (2 * s - 1) + (1 - s) * (nt - 1)
    return pl.pallas_call(
        functools.partial(_gla_kernel, nt=nt, tt=tt),
        out_shape=[jax.ShapeDtypeStruct((batch * seq, w), F32),
                   jax.ShapeDtypeStruct((batch, 2, GLA_DK, w), F32)],
        grid=(batch, 2, nt),
        in_specs=[
            pl.BlockSpec((tt, GLA_IN), lambda b, s, j: (b * nt + tile_of(s, j), 0)),
            pl.BlockSpec((1, GZ_PAD, w), lambda b, s, j: (s, 0, 0)),
            pl.BlockSpec((1, 1, w), lambda b, s, j: (s, 0, 0)),
            _const_spec((1, w)),
            pl.BlockSpec((1, 1, GLA_DK, w),
                         lambda b, s, j: (b if per_batch_state else 0, s, 0, 0)),
        ],
        out_specs=[
            pl.BlockSpec((tt, w), lambda b, s, j: (b * nt + j * s, 0)),
            pl.BlockSpec((1, 1, GLA_DK, w), lambda b, s, j: (b, s, 0, 0)),
        ],
        scratch_shapes=[pltpu.VMEM((w, w), F32), pltpu.VMEM((tt, w), F32),
                        pltpu.VMEM((seq, w), F32)]
        + [pltpu.VMEM((tt, w), BF16)] * 4
        + [pltpu.VMEM((tt // GLA_CHUNK, w, w), BF16),
           pltpu.VMEM((tt // GLA_CHUNK, 8, w), F32)],
        compiler_params=_params(3),
        name="gla",
    )(gin, wg, bg, norm_g.reshape(1, w), s0)


def _lru_sweep(fwd, j, nt, tt, xp_ref, xc_ref, xn_ref, lg_ref, cw_ref, cb_ref,
               wg_ref, bg_ref, lam_ref, h0_ref, o_ref, sout_ref,
               carry_scr, a_scr, ab_scr, hl_scr, pl_scr, h_scr, hb_scr, xcv_scr):
    w = LRU_WIDTH
    seg_len = tt // 8
    tile = j if fwd else nt - 1 - j

    @pl.when(j == 0)
    def _():
        carry_scr[...] = jnp.broadcast_to(h0_ref[0, 0], (8, w))

    all_rows = pl.ds(pl.multiple_of(tile * tt, 8), tt)
    if fwd:
        xc = xcv_scr[all_rows, :]
    else:
        pre = jnp.where(tile > 0, xp_ref[...], 0.0)
        post = jnp.where(tile < nt - 1, xn_ref[...], 0.0)
        xe = jnp.concatenate([pre, xc_ref[...], post], axis=0)
        ne = tt + 16
        xc = cb_ref[...] + cw_ref[2:3, :] * xe[8:8 + tt]
        xc = xc + cw_ref[0:1, :] * pltpu.roll(xe, 2, 0)[8:8 + tt]
        xc = xc + cw_ref[1:2, :] * pltpu.roll(xe, 1, 0)[8:8 + tt]
        xc = xc + cw_ref[3:4, :] * pltpu.roll(xe, ne - 1, 0)[8:8 + tt]
        xcv_scr[all_rows, :] = xc

    ri = _dot(xc.astype(BF16), wg_ref[0]) + bg_ref[0]
    r = _sigmoid(ri[:, 0:w])
    gate_i = _sigmoid(ri[:, w:2 * w])
    log_a = r * ((-LRU_C) * _softplus(-lam_ref[0]))
    a = jnp.exp(log_a)
    bt = jnp.sqrt(1.0 - a * a) * (gate_i * xc)

    nl = w // 128
    pitch = seg_len + LRU_SEG_PAD
    for lt in range(nl):
        for k in range(8):
            dst = slice(k * pitch, k * pitch + seg_len)
            src = slice(k * seg_len, (k + 1) * seg_len)
            a_scr[lt, dst, :] = a[src, lt * 128:(lt + 1) * 128]
            ab_scr[lt, dst, :] = bt[src, lt * 128:(lt + 1) * 128]

    def step(ii, hp):
        i = ii if fwd else seg_len - 1 - ii
        seg_rows = pl.ds(i, 8, stride=pitch)
        rows = pl.ds(pl.multiple_of(i * 8, 8), 8)
        out = []
        for lt in range(nl):
            h, p = hp[lt]
            a_i = a_scr[lt, seg_rows, :]
            h = a_i * h + ab_scr[lt, seg_rows, :]
            p = a_i * p
            hl_scr[lt, rows, :] = h
            pl_scr[lt, rows, :] = p
            out.append((h, p))
        return tuple(out)

    init = tuple((jnp.zeros((8, 128), F32), jnp.ones((8, 128), F32)) for _ in range(nl))
    ends = lax.fori_loop(0, seg_len, step, init, unroll=8)
    h_end = jnp.concatenate([e[0] for e in ends], axis=1)
    p_end = jnp.concatenate([e[1] for e in ends], axis=1)

    c = carry_scr[0:1, :]
    c_in = [None] * 8
    for k in (range(8) if fwd else range(7, -1, -1)):
        c_in[k] = c
        c = h_end[k:k + 1, :] + p_end[k:k + 1, :] * c
    carry_scr[...] = jnp.broadcast_to(c, (8, w))

    for k in range(8):
        for i0 in range(0, seg_len, 8):
            src = pl.ds(i0 * 8 + k, 8, stride=8)
            r0 = k * seg_len + i0
            for lt in range(nl):
                c_k = c_in[k][:, lt * 128:(lt + 1) * 128]
                h_scr[lt, r0:r0 + 8, :] = hl_scr[lt, src, :] + pl_scr[lt, src, :] * c_k
    h_all = jnp.concatenate([h_scr[lt] for lt in range(nl)], axis=1)
    if fwd:
        o_ref[...] = (h_all + hb_scr[all_rows, :]) * _gelu_tanh(lg_ref[...])
    else:
        hb_scr[all_rows, :] = h_all

    @pl.when(j == nt - 1)
    def _():
        sout_ref[0, 0] = carry_scr[...]


def _lru_kernel(xp_ref, xc_ref, xn_ref, lg_ref, cw_ref, cb_ref, wg_ref, bg_ref,
                lam_ref, h0_ref, o_ref, sout_ref, carry_scr, a_scr, ab_scr, hl_scr, pl_scr, h_scr, hb_scr, xcv_scr,
                *, nt, tt):
    sweep = pl.program_id(1)
    j = pl.program_id(2)
    args = (j, nt, tt, xp_ref, xc_ref, xn_ref, lg_ref, cw_ref, cb_ref, wg_ref, bg_ref,
            lam_ref, h0_ref, o_ref, sout_ref, carry_scr, a_scr, ab_scr, hl_scr, pl_scr, h_scr, hb_scr, xcv_scr)

    @pl.when(sweep == 0)
    def _():
        _lru_sweep(False, *args)

    @pl.when(sweep == 1)
    def _():
        _lru_sweep(True, *args)


def _lru(lx, lg, conv_w, conv_b, wg, bg, lam, h0, batch, seq, tt):
    nt = seq // tt
    w = LRU_WIDTH
    t8 = tt // 8
    n8 = seq // 8
    per_batch_state = h0.shape[0] == batch
    tile_of = lambda s, j: j * (2 * s - 1) + (1 - s) * (nt - 1)
    return pl.pallas_call(
        functools.partial(_lru_kernel, nt=nt, tt=tt),
        out_shape=[jax.ShapeDtypeStruct((batch * seq, w), F32),
                   jax.ShapeDtypeStruct((batch, 2, 8, w), F32)],
        grid=(batch, 2, nt),
        in_specs=[
            pl.BlockSpec((8, w), lambda b, s, j:
                         (b * n8 + jnp.maximum(tile_of(s, j) * t8 - 1, 0), 0)),
            pl.BlockSpec((tt, w), lambda b, s, j: (b * nt + tile_of(s, j), 0)),
            pl.BlockSpec((8, w), lambda b, s, j:
                         (b * n8 + jnp.minimum((tile_of(s, j) + 1) * t8, n8 - 1), 0)),
            pl.BlockSpec((tt, w), lambda b, s, j: (b * nt + j * s, 0)),
            _const_spec((4, w)),
            _const_spec((1, w)),
            pl.BlockSpec((1, w, 2 * w), lambda b, s, j: (s, 0, 0)),
            pl.BlockSpec((1, 1, 2 * w), lambda b, s, j: (s, 0, 0)),
            pl.BlockSpec((1, 1, w), lambda b, s, j: (s, 0, 0)),
            pl.BlockSpec((1, 1, 1, w),
                         lambda b, s, j: (b if per_batch_state else 0, s, 0, 0)),
        ],
        out_specs=[
            pl.BlockSpec((tt, w), lambda b, s, j: (b * nt + j * s, 0)),
            pl.BlockSpec((1, 1, 8, w), lambda b, s, j: (b, s, 0, 0)),
        ],
        scratch_shapes=[pltpu.VMEM((8, w), F32)]
        + [pltpu.VMEM((w // 128, tt + 8 * LRU_SEG_PAD, 128), F32)] * 2
        + [pltpu.VMEM((w // 128, tt, 128), F32)] * 3
        + [pltpu.VMEM((seq, w), F32)] * 2,
        compiler_params=_params(3),
        name="lru",
    )(lx, lx, lx, lg, conv_w, conv_b.reshape(1, w), wg, bg, lam, h0)


def _mlp_kernel(x_ref, att_ref, gla_ref, lru_ref, mod_ref, n2_ref, fn_ref,
                wo_ref, w1_ref, w2_ref, o_ref, *, final, ff_chunk):
    d = D_MODEL
    mix = _dot(att_ref[...], wo_ref[0, 0:ATTN_WIDTH, :])
    mix = mix + _dot(gla_ref[...].astype(BF16), wo_ref[0, ATTN_WIDTH:ATTN_WIDTH + GLA_WIDTH, :])
    mix = mix + _dot(lru_ref[...].astype(BF16), wo_ref[0, ATTN_WIDTH + GLA_WIDTH:d, :])
    g1 = mod_ref[0, 0, :, 2 * d:3 * d]
    sh2 = mod_ref[0, 0, :, 3 * d:4 * d]
    sc2 = mod_ref[0, 0, :, 4 * d:5 * d]
    g2 = mod_ref[0, 0, :, 5 * d:6 * d]
    x = x_ref[...] + g1 * mix
    ms = jnp.mean(x * x, axis=-1, keepdims=True)
    h = (x * lax.rsqrt(ms + EPS) * n2_ref[...] * (1.0 + sc2) + sh2).astype(BF16)
    y = None
    for c in range(D_FF // ff_chunk):
        cols = slice(c * ff_chunk, (c + 1) * ff_chunk)
        u = jnp.maximum(_dot(h, w1_ref[0, :, cols]), 0.0)
        part = _dot((u * u).astype(BF16), w2_ref[0, cols, :])
        y = part if y is None else y + part
    x = x + g2 * y
    if final:
        ms = jnp.mean(x * x, axis=-1, keepdims=True)
        x = x * lax.rsqrt(ms + EPS) * fn_ref[...]
    o_ref[...] = x


def _out_mlp(x2d, att, gla, lru, mod, norm2, final_norm, wo, w1, w2, layer,
             rows_per_mod, mod_row0, tt, final):
    n, d = x2d.shape
    nt = n // tt
    mod_map = lambda i: (layer, mod_row0 + (i * tt) // rows_per_mod, 0, 0)
    row = lambda wd: pl.BlockSpec((tt, wd), lambda i: (i, 0))
    resident = lambda shape: pl.BlockSpec((1,) + shape, lambda i: (layer, 0, 0),
                                          pipeline_mode=pl.Buffered(1))
    return pl.pallas_call(
        functools.partial(_mlp_kernel, final=final, ff_chunk=1024),
        out_shape=jax.ShapeDtypeStruct((n, d), F32),
        grid=(nt,),
        in_specs=[row(d), row(ATTN_WIDTH), row(GLA_WIDTH), row(LRU_WIDTH),
                  pl.BlockSpec((1, 1, 1, 6 * d), mod_map),
                  _const_spec((1, d)), _const_spec((1, d)),
                  resident((d, d)), resident((d, D_FF)), resident((D_FF, d))],
        out_specs=row(d),
        compiler_params=_params(1),
        name="out_mlp",
    )(x2d, att, gla, lru, mod, norm2.reshape(1, d), final_norm.reshape(1, d), wo, w1, w2)


def _prep_w_in(w_in):
    o = 0
    parts = {}
    for name, size in (("q", 512), ("k", 128), ("v", 128), ("gq", 256), ("gk", 256),
                       ("gv", 256), ("gz", 32), ("gr", 256), ("lx", 256), ("lg", 256)):
        parts[name] = w_in[..., o:o + size].astype(BF16)
        o += size
    pad = jnp.zeros(w_in.shape[:-1] + (GZ_PAD - 32,), BF16)
    cols = [parts["q"], parts["k"], parts["v"], parts["gq"], parts["gk"], parts["gv"],
            parts["gr"], parts["gz"], pad, parts["lx"], parts["lg"]]
    return jnp.concatenate(cols, axis=-1)


def _prep_gate_w(gw):
    out = []
    nz = 2 * GLA_GATE_RANK
    for dr in (1, 0):
        wf = jnp.zeros((nz, GLA_WIDTH), F32)
        wf = wf.at[dr * GLA_GATE_RANK:(dr + 1) * GLA_GATE_RANK].set(gw[dr])
        hi, lo = _split_bf16(wf)
        out.append(jnp.concatenate([hi, hi, lo, jnp.zeros((GZ_PAD - 3 * nz, GLA_WIDTH), BF16)],
                                   axis=0))
    return jnp.stack(out)


def _block_diag(w4):
    n, a, b = w4.shape
    eye = jnp.eye(n, dtype=w4.dtype)
    return jnp.einsum("nab,nm->namb", w4, eye).reshape(n * a, n * b)


def _rope_tables(seq):
    rows = seq // GRID_W
    nf = HEAD_DIM // 4
    inv = ROPE_BASE ** (-jnp.arange(nf, dtype=F32) / nf)
    ang_r = jnp.arange(rows, dtype=F32)[:, None] * inv
    ang_c = jnp.arange(GRID_W, dtype=F32)[:, None] * inv
    by_row = lambda t: jnp.repeat(t, GRID_W, axis=0)
    by_col = lambda t: jnp.tile(t, (rows, 1))
    cos_r, sin_r = by_row(jnp.cos(ang_r)), by_row(jnp.sin(ang_r))
    cos_c, sin_c = by_col(jnp.cos(ang_c)), by_col(jnp.sin(ang_c))
    cos = jnp.concatenate([cos_r, cos_r, cos_c, cos_c], axis=1)
    sin = jnp.concatenate([-sin_r, sin_r, -sin_c, sin_c], axis=1)
    return jnp.concatenate([cos] * 2, axis=1), jnp.concatenate([sin] * 2, axis=1)


def _gla_state_to_internal(s):
    b = s.shape[0]
    return jnp.transpose(s, (0, 1, 4, 2, 3)).reshape(b, 2, GLA_DK, GLA_WIDTH)


def kernel(x_prompt, x_sample, c, cache_k, cache_v, state_gla, state_lru, c_ctx, w_mod, b_mod, norm1, norm2, w_in, attn_sink, gla_gate_w, gla_gate_b, gla_norm, lru_conv_w, lru_conv_b, lru_wa, lru_ba, lru_wx, lru_bx, lru_lambda, w_out, w_mlp1, w_mlp2, final_norm):
    depth = w_in.shape[0]
    bc, sc_len, d = x_prompt.shape
    bl, sl_len, _ = x_sample.shape
    past = cache_k.shape[2]

    cond = jnp.concatenate([c_ctx[None], c, jnp.zeros((MOD_ROWS - 1 - bl, d), F32)], axis=0)
    mod = _ada_mod(cond, w_mod, b_mod)

    flip_dir = lambda a: a[:, ::-1]
    rope_tabs = _rope_tables(sl_len)
    ck = cache_k.reshape(bl, depth, past, KV_WIDTH)
    cv = cache_v.reshape(bl, depth, past, KV_WIDTH)
    gla_s0_ctx = jnp.zeros((1, 2, GLA_DK, GLA_WIDTH), F32)
    lru_s0_ctx = jnp.zeros((1, 2, 1, LRU_WIDTH), F32)

    xp = x_prompt.reshape(bc * sc_len, d)
    xs = x_sample.reshape(bl * sl_len, d)
    w_in_b = _prep_w_in(w_in)
    wo = w_out.astype(BF16)
    w1 = w_mlp1.astype(BF16)
    w2 = w_mlp2.astype(BF16)
    ks, vs, sgs, sls = [], [], [], []
    for l in range(depth):
        gate_w = _prep_gate_w(gla_gate_w[l])
        gate_b = gla_gate_b[l][::-1].reshape(2, 1, GLA_WIDTH)
        lru_wg = jnp.stack([jnp.concatenate([_block_diag(lru_wa[l, dr]), _block_diag(lru_wx[l, dr])],
                                            axis=1) for dr in (1, 0)]).astype(BF16)
        lru_bg = jnp.stack([jnp.concatenate([lru_ba[l, dr], lru_bx[l, dr]]) for dr in (1, 0)]
                           ).reshape(2, 1, 2 * LRU_WIDTH)
        lam = lru_lambda[l][::-1].reshape(2, 1, LRU_WIDTH)
        final = l == depth - 1

        for is_lat in (False, True):
            if is_lat:
                x2d, batch, seq, row0, rpm = xs, bl, sl_len, 1, sl_len
                tabs, tt_proj, tq, tt_rec = rope_tabs, 512, 256, 512
                gla_s0 = flip_dir(_gla_state_to_internal(state_gla[:, l]))
                lru_s0 = flip_dir(state_lru[:, l]).reshape(bl, 2, 1, LRU_WIDTH)
            else:
                x2d, batch, seq, row0, rpm = xp, bc, sc_len, 0, bc * sc_len
                tabs, tt_proj, tq, tt_rec = None, 512, sc_len, sc_len
                gla_s0, lru_s0 = gla_s0_ctx, lru_s0_ctx
            q, k, v, gin, lx, lg = _in_proj(x2d, mod, norm1[l], w_in_b, l, tabs, rpm, row0, tt_proj)
            att = _attention(q, k, v, attn_sink[l], batch, seq, tq,
                             ctx_kv=(ck, cv, l) if is_lat else None)
            gla, sg = _gla(gin, gate_w, gate_b, gla_norm[l], gla_s0, batch, seq, tt_rec)
            lru, sl = _lru(lx, lg, lru_conv_w[l], lru_conv_b[l], lru_wg, lru_bg, lam, lru_s0,
                           batch, seq, tt_rec)
            x2d = _out_mlp(x2d, att, gla, lru, mod, norm2[l], final_norm, wo, w1, w2, l,
                           rpm, row0, 512, final)
            if is_lat:
                xs = x2d
            else:
                xp = x2d
                ks.append(k.reshape(bc, sc_len, KV_WIDTH))
                vs.append(v.reshape(bc, sc_len, KV_WIDTH))
                sgs.append(sg)
                sls.append(sl[:, :, 0, :])

    y_prompt = xp.reshape(bc, sc_len, d)
    y_sample = xs.reshape(bl, sl_len, d)
    kv_shape = (bc, depth, sc_len, KV_WIDTH // HEAD_DIM, HEAD_DIM)
    new_k = jnp.stack(ks, axis=1).reshape(kv_shape)
    new_v = jnp.stack(vs, axis=1).reshape(kv_shape)
    sg_all = jnp.stack(sgs, axis=1)[:, :, ::-1]
    new_sg = jnp.transpose(
        sg_all.reshape(bc, depth, 2, GLA_DK, GLA_HEADS, GLA_DK), (0, 1, 2, 4, 5, 3))
    new_sl = jnp.stack(sls, axis=1)[:, :, ::-1]
    return (y_prompt, y_sample, new_k, new_v, new_sg, new_sl)
```

```python
import functools

import jax
import jax.numpy as jnp
from jax import lax
from jax.experimental import pallas as pl
from jax.experimental.pallas import tpu as pltpu

F32 = jnp.float32
BF16 = jnp.bfloat16

D_MODEL = 1024
GRID_W = 64
EPS = 1e-6
HEAD_DIM = 64
ATTN_WIDTH = 512
ATTN_HEADS = 8
KV_WIDTH = 128
WINDOW = 128
ATTN_LOOKAHEAD = 1
ROPE_BASE = 10000.0
NEG_INF = -1e30
GLA_WIDTH = 256
GLA_DK = 64
GLA_HEADS = 4
GLA_GATE_RANK = 16
GLA_GATE_NORM = 16.0
GLA_CHUNK = 64
GLA_SKEW = 2
LRU_WIDTH = 256
LRU_C = 8.0
LRU_SEG_PAD = 8
D_FF = 4096
MOD_ROWS = 8
GZ_PAD = 128
C_Q, C_K, C_V, C_GLA, C_A_END, C_B = 0, 512, 640, 768, 1536, 1568
GLA_IN = 4 * GLA_WIDTH + GZ_PAD
VMEM_LIMIT = 56 * 1024 * 1024

NT_DIMS = (((1,), (1,)), ((), ()))


def _split_bf16(a):
    hi = a.astype(BF16)
    lo = (a - hi.astype(F32)).astype(BF16)
    return hi, lo


def _dot(a, b):
    return jnp.dot(a, b, preferred_element_type=F32)


def _dot_nt(a, b):
    return lax.dot_general(a, b, NT_DIMS, preferred_element_type=F32)


def _dot_x3(a, b):
    ah, al = _split_bf16(a)
    bh, bl = _split_bf16(b)
    return _dot(ah, bh) + (_dot(ah, bl) + _dot(al, bh))


def _softplus(y):
    return jnp.maximum(y, 0.0) + jnp.log1p(jnp.exp(-jnp.abs(y)))


def _log_sigmoid(y):
    return jnp.minimum(y, 0.0) - jnp.log(1.0 + jnp.exp(-jnp.abs(y)))


def _sigmoid(y):
    return 0.5 * jnp.tanh(0.5 * y) + 0.5


def _silu(y):
    return y * _sigmoid(y)


def _gelu_tanh(y):
    c = 0.7978845608028654
    return 0.5 * y * (1.0 + jnp.tanh(c * (y + 0.044715 * (y * y * y))))


def _params(n_axes):
    return pltpu.CompilerParams(
        dimension_semantics=("arbitrary",) * n_axes, vmem_limit_bytes=VMEM_LIMIT)


def _const_spec(shape):
    nd = len(shape)
    return pl.BlockSpec(shape, lambda *_: (0,) * nd)


def _ada_kernel(c_ref, w_ref, b_ref, o_ref):
    s = _silu(c_ref[...])
    res = _dot_x3(s, w_ref[0]) + b_ref[0]
    for r in range(MOD_ROWS):
        o_ref[0, r] = res[r:r + 1, :]


def _ada_mod(cond, w_mod, b_mod):
    depth, d, n = w_mod.shape
    tn = 1536
    return pl.pallas_call(
        _ada_kernel,
        out_shape=jax.ShapeDtypeStruct((depth, MOD_ROWS, 1, n), F32),
        grid=(depth, n // tn),
        in_specs=[
            pl.BlockSpec((MOD_ROWS, d), lambda l, j: (0, 0)),
            pl.BlockSpec((1, d, tn), lambda l, j: (l, 0, j)),
            pl.BlockSpec((1, 1, tn), lambda l, j: (l, 0, j)),
        ],
        out_specs=pl.BlockSpec((1, MOD_ROWS, 1, tn), lambda l, j: (l, 0, 0, j)),
        compiler_params=_params(2),
        name="ada_mod",
    )(cond, w_mod, b_mod.reshape(depth, 1, n))


def _rope(x, cos, sin_signed):
    w = x.shape[1]
    lane = lax.broadcasted_iota(jnp.int32, (1, w), 1)
    first = (lane % 32) < 16
    swapped = jnp.where(first, pltpu.roll(x, w - 16, 1), pltpu.roll(x, 16, 1))
    return x * cos + swapped * sin_signed


def _inproj_kernel(*refs, rope):
    if rope:
        (x_ref, mod_ref, n1_ref, wa_ref, wz_ref, wb_ref, cos_ref, sin_ref,
         q_ref, k_ref, v_ref, gla_ref, lx_ref, lg_ref) = refs
    else:
        (x_ref, mod_ref, n1_ref, wa_ref, wz_ref, wb_ref,
         q_ref, k_ref, v_ref, gla_ref, lx_ref, lg_ref) = refs
    d = D_MODEL
    x = x_ref[...]
    ms = jnp.mean(x * x, axis=-1, keepdims=True)
    xn = x * lax.rsqrt(ms + EPS) * n1_ref[...]
    sh = mod_ref[0, 0, :, 0:d]
    sc = mod_ref[0, 0, :, d:2 * d]
    h = (xn * (1.0 + sc) + sh).astype(BF16)
    pa = _dot(h, wa_ref[0])
    pb = _dot(h, wb_ref[0])
    q = pa[:, C_Q:C_K]
    k = pa[:, C_K:C_V]
    v = pa[:, C_V:C_GLA]
    if rope:
        cos = cos_ref[...]
        sin = sin_ref[...]
        k = _rope(k, cos, sin)
        q = _rope(q, jnp.concatenate([cos] * 4, axis=1), jnp.concatenate([sin] * 4, axis=1))
    q_ref[...] = (q * (HEAD_DIM ** -0.5)).astype(BF16)
    k_ref[...] = k
    v_ref[...] = v
    w = GLA_WIDTH
    gla_ref[:, 0:3 * w] = pa[:, C_GLA:C_A_END]
    gla_ref[:, 3 * w:4 * w] = pb[:, 0:w]
    gla_ref[:, 4 * w:4 * w + GZ_PAD] = _dot(h, wz_ref[0])
    lx_ref[...] = pb[:, w:w + LRU_WIDTH]
    lg_ref[...] = pb[:, w + LRU_WIDTH:w + 2 * LRU_WIDTH]


def _in_proj(x2d, mod, norm1, w_abz, layer, rope_tabs, rows_per_mod, mod_row0, tt):
    n, d = x2d.shape
    nt = n // tt
    rope = rope_tabs is not None
    mod_map = lambda i: (layer, mod_row0 + (i * tt) // rows_per_mod, 0, 0)
    by_layer = lambda arr: pl.BlockSpec((1,) + arr.shape[1:], lambda i: (layer, 0, 0))
    in_specs = [
        pl.BlockSpec((tt, d), lambda i: (i, 0)),
        pl.BlockSpec((1, 1, 1, 6 * d), mod_map),
        _const_spec((1, d)),
    ] + [by_layer(arr) for arr in w_abz]
    args = [x2d, mod, norm1.reshape(1, d)] + list(w_abz)
    if rope:
        t = rope_tabs[0].shape[0]
        tpb = t // tt
        in_specs += [pl.BlockSpec((tt, 128), lambda i: (i % tpb, 0))] * 2
        args += list(rope_tabs)
    widths = (ATTN_WIDTH, KV_WIDTH, KV_WIDTH, GLA_IN, LRU_WIDTH, LRU_WIDTH)
    dtypes = (BF16, F32, F32, F32, F32, F32)
    return pl.pallas_call(
        functools.partial(_inproj_kernel, rope=rope),
        out_shape=[jax.ShapeDtypeStruct((n, wd), dt) for wd, dt in zip(widths, dtypes)],
        grid=(nt,),
        in_specs=in_specs,
        out_specs=[pl.BlockSpec((tt, wd), lambda i: (i, 0)) for wd in widths],
        compiler_params=_params(1),
        name="in_proj_lat" if rope else "in_proj_ctx",
    )(*args)


def _half_variants(a):
    lane = lax.broadcasted_iota(jnp.int32, (1, 128), 1)
    low = lane < HEAD_DIM
    ar = pltpu.roll(a, HEAD_DIM, 1)
    z = jnp.zeros_like(a)
    cast = lambda t: t.astype(BF16)
    return ((cast(jnp.where(low, a, z)), cast(jnp.where(low, z, ar))),
            (cast(jnp.where(low, ar, z)), cast(jnp.where(low, z, a))))


def _attn_kernel(*refs, latent, tq, seq):
    if latent:
        (sink_ref, q_ref, kp_ref, kc_ref, kn_ref, vp_ref, vc_ref, vn_ref,
         ck_ref, cv_ref, o_ref) = refs
        i = pl.program_id(1)
        k_loc = jnp.concatenate([kp_ref[...], kc_ref[...], kn_ref[...]], axis=0)
        v_loc = jnp.concatenate([vp_ref[...], vc_ref[...], vn_ref[...]], axis=0)
        nk = tq + 2 * WINDOW
        r = lax.broadcasted_iota(jnp.int32, (tq, nk), 0)
        j = lax.broadcasted_iota(jnp.int32, (tq, nk), 1)
        kpos = i * tq - WINDOW + j
        valid = (j >= r) & (j <= r + 2 * WINDOW) & (kpos >= 0) & (kpos < seq)
        kvar_c = _half_variants(ck_ref[0, 0])
        vvar_c = _half_variants(cv_ref[0, 0])
    else:
        sink_ref, q_ref, kc_ref, vc_ref, o_ref = refs
        k_loc = kc_ref[...]
        v_loc = vc_ref[...]
        valid = None
    kvar = _half_variants(k_loc)
    with_ones = lambda t: jnp.concatenate([t, jnp.ones_like(t)], axis=1)
    vvar = [[with_ones(t) for t in pair] for pair in _half_variants(v_loc)]
    if latent:
        vvar_c = [[with_ones(t) for t in pair] for pair in vvar_c]

    def scores(head):
        c, half = divmod(head, 2)
        qc = q_ref[:, c * 128:(c + 1) * 128]
        s = _dot_nt(qc, kvar[c // 2][half])
        if valid is not None:
            s = jnp.where(valid, s, NEG_INF)
        return s, (_dot_nt(qc, kvar_c[c // 2][half]) if latent else None)

    ready = [scores(h) for h in range(ATTN_LOOKAHEAD)]
    acc = None
    for head in range(ATTN_HEADS):
        c, half = divmod(head, 2)
        s, s_c = ready.pop(0)
        if head + ATTN_LOOKAHEAD < ATTN_HEADS:
            ready.append(scores(head + ATTN_LOOKAHEAD))
        sink = sink_ref[head]
        m = jnp.maximum(jnp.max(s, axis=-1, keepdims=True), sink)
        if latent:
            m = jnp.maximum(m, jnp.max(s_c, axis=-1, keepdims=True))
        o = _dot(jnp.exp(s - m).astype(BF16), vvar[c // 2][half])
        if latent:
            o = o + _dot(jnp.exp(s_c - m).astype(BF16), vvar_c[c // 2][half])
        denom = jnp.exp(sink - m) + o[:, 128:256]
        o = o[:, 0:128] * (1.0 / denom)
        if half == 0:
            acc = o
        else:
            o_ref[:, c * 128:(c + 1) * 128] = (acc + o).astype(BF16)


def _attention(q, k, v, sink, batch, seq, tq, ctx_kv=None):
    latent = ctx_kv is not None
    nq = seq // tq
    kern = functools.partial(_attn_kernel, latent=latent, tq=tq, seq=seq)
    smem = pl.BlockSpec(memory_space=pltpu.SMEM)
    if latent:
        ck, cv, layer = ctx_kv
        past = ck.shape[2]
        wb = tq // WINDOW
        nwb = seq // WINDOW
        cur = pl.BlockSpec((tq, KV_WIDTH), lambda b, i: (b * nq + i, 0))
        prev = pl.BlockSpec((WINDOW, KV_WIDTH),
                            lambda b, i: (b * nwb + jnp.maximum(i * wb - 1, 0), 0))
        nxt = pl.BlockSpec((WINDOW, KV_WIDTH),
                           lambda b, i: (b * nwb + jnp.minimum((i + 1) * wb, nwb - 1), 0))
        cspec = pl.BlockSpec((1, 1, past, KV_WIDTH), lambda b, i: (b, layer, 0, 0))
        in_specs = [smem, pl.BlockSpec((tq, ATTN_WIDTH), lambda b, i: (b * nq + i, 0)),
                    prev, cur, nxt, prev, cur, nxt, cspec, cspec]
        args = (sink, q, k, k, k, v, v, v, ck, cv)
    else:
        cur = pl.BlockSpec((tq, KV_WIDTH), lambda b, i: (b * nq + i, 0))
        in_specs = [smem, pl.BlockSpec((tq, ATTN_WIDTH), lambda b, i: (b * nq + i, 0)), cur, cur]
        args = (sink, q, k, v)
    return pl.pallas_call(
        kern,
        out_shape=jax.ShapeDtypeStruct((batch * seq, ATTN_WIDTH), BF16),
        grid=(batch, nq),
        in_specs=in_specs,
        out_specs=pl.BlockSpec((tq, ATTN_WIDTH), lambda b, i: (b * nq + i, 0)),
        compiler_params=_params(2),
        name="attn_lat" if latent else "attn_ctx",
    )(*args)


def _gla_sweep(fwd, j, nt, tt, gin_ref, wg_ref, bg_ref, ng_ref, s0_ref, o_ref, sout_ref,
               st_scr, g_scr, ob_scr, qin_scr, kin_scr, kend_scr, a_scr, stc_scr, dec_scr):
    w = GLA_WIDTH
    ck = GLA_CHUNK
    nc = tt // ck
    tile = j if fwd else nt - 1 - j
    starts = [(ci if fwd else nc - 1 - ci) * ck for ci in range(nc)]

    rr = lax.broadcasted_iota(jnp.int32, (w, w), 0) // GLA_DK
    cc = lax.broadcasted_iota(jnp.int32, (w, w), 1) // GLA_DK
    block_diag = rr == cc
    bd_bf = block_diag.astype(F32).astype(BF16)
    tr = lax.broadcasted_iota(jnp.int32, (ck, 2 * ck), 0)
    tc = lax.broadcasted_iota(jnp.int32, (ck, 2 * ck), 1) % ck
    tri2 = ((tr >= tc) if fwd else (tr <= tc)).astype(F32).astype(BF16)
    ar = lax.broadcasted_iota(jnp.int32, (ck, w), 0)
    ac = lax.broadcasted_iota(jnp.int32, (ck, w), 1) % ck
    causal = (ar >= ac) if fwd else (ar <= ac)

    @pl.when(j == 0)
    def _():
        s0 = s0_ref[0, 0]
        st_scr[...] = jnp.where(block_diag, jnp.concatenate([s0] * GLA_HEADS, axis=0), 0.0)

    zpad = gin_ref[:, 4 * w:4 * w + GZ_PAD]
    zh = zpad.astype(BF16).astype(F32)
    zcat = zh + pltpu.roll(zpad - zh, 32, 1) + pltpu.roll(zh, 64, 1)
    logit = _dot(zcat.astype(BF16), wg_ref[0]) + bg_ref[0]
    g_scr[...] = _log_sigmoid(logit) * (1.0 / GLA_GATE_NORM)

    tile_rows = pl.multiple_of(tile * tt, ck)

    def decayed(ci, c0):
        rows = pl.ds(c0, ck)
        g = g_scr[rows, :]
        gh, gl = _split_bf16(g)
        b = _dot(tri2, jnp.concatenate([gh, gl], axis=0))
        btot = jnp.sum(g, axis=0, keepdims=True)
        k = gin_ref[rows, w:2 * w]
        qin_scr[rows, :] = (gin_ref[rows, 0:w] * (GLA_DK ** -0.5) * jnp.exp(b)).astype(BF16)
        kin_scr[rows, :] = (k * jnp.exp(-b)).astype(BF16)
        kend_scr[rows, :] = (k * jnp.exp(btot - b)).astype(BF16)
        dec_scr[ci] = jnp.broadcast_to(jnp.exp(btot), (8, w))

    def scores(ci, c0):
        rows = pl.ds(c0, ck)
        k_bd = jnp.concatenate([kin_scr[rows, :]] * GLA_HEADS, axis=0) * bd_bf
        a_scr[rows, :] = jnp.where(causal, _dot_nt(qin_scr[rows, :], k_bd), 0.0).astype(BF16)

    def state(ci, c0):
        rows = pl.ds(c0, ck)
        st = st_scr[...]
        stc_scr[ci] = st.astype(BF16)
        delta_t = _dot(gin_ref[rows, 2 * w:3 * w].T.astype(BF16), kend_scr[rows, :])
        st_scr[...] = st * dec_scr[ci][0:1, :] + jnp.where(block_diag, delta_t, 0.0)

    def output(ci, c0):
        rows = pl.ds(c0, ck)
        v_bd = jnp.concatenate([gin_ref[rows, 2 * w:3 * w].astype(BF16)] * GLA_HEADS,
                               axis=0) * bd_bf
        o = _dot(a_scr[rows, :], v_bd) + _dot_nt(qin_scr[rows, :], stc_scr[ci])
        if fwd:
            o_ref[rows, :] = o
        else:
            ob_scr[pl.ds(tile_rows + c0, ck), :] = o

    stages = (decayed, scores, state, output)
    for step in range(nc + GLA_SKEW * (len(stages) - 1)):
        for si, stage in enumerate(stages):
            ci = step - GLA_SKEW * si
            if 0 <= ci < nc:
                stage(ci, starts[ci])

    if fwd:
        o = o_ref[...] + ob_scr[pl.ds(tile_rows, tt), :]
        ms = _dot((o * o).astype(BF16), bd_bf) * (1.0 / GLA_DK)
        r = gin_ref[:, 3 * w:4 * w]
        o_ref[...] = o * lax.rsqrt(ms + EPS) * ng_ref[...] * _silu(r)

    @pl.when(j == nt - 1)
    def _():
        st = st_scr[...]
        sout_ref[0, 0] = sum(st[h * GLA_DK:(h + 1) * GLA_DK, :] for h in range(1, GLA_HEADS)) \
            + st[0:GLA_DK, :]


def _gla_kernel(gin_ref, wg_ref, bg_ref, ng_ref, s0_ref, o_ref, sout_ref,
                st_scr, g_scr, ob_scr, qin_scr, kin_scr, kend_scr, a_scr, stc_scr, dec_scr,
                *, nt, tt):
    sweep = pl.program_id(1)
    j = pl.program_id(2)
    args = (j, nt, tt, gin_ref, wg_ref, bg_ref, ng_ref, s0_ref, o_ref, sout_ref,
            st_scr, g_scr, ob_scr, qin_scr, kin_scr, kend_scr, a_scr, stc_scr, dec_scr)

    @pl.when(sweep == 0)
    def _():
        _gla_sweep(False, *args)

    @pl.when(sweep == 1)
    def _():
        _gla_sweep(True, *args)


def _gla(gin, wg, bg, norm_g, s0, batch, seq, tt):
    nt = seq // tt
    w = GLA_WIDTH
    per_batch_state = s0.shape[0] == batch
    tile_of = lambda s, j: j * (2 * s - 1) + (1 - s) * (nt - 1)
    return pl.pallas_call(
        functools.partial(_gla_kernel, nt=nt, tt=tt),
        out_shape=[jax.ShapeDtypeStruct((batch * seq, w), F32),
                   jax.ShapeDtypeStruct((batch, 2, GLA_DK, w), F32)],
        grid=(batch, 2, nt),
        in_specs=[
            pl.BlockSpec((tt, GLA_IN), lambda b, s, j: (b * nt + tile_of(s, j), 0)),
            pl.BlockSpec((1, GZ_PAD, w), lambda b, s, j: (s, 0, 0)),
            pl.BlockSpec((1, 1, w), lambda b, s, j: (s, 0, 0)),
            _const_spec((1, w)),
            pl.BlockSpec((1, 1, GLA_DK, w),
                         lambda b, s, j: (b if per_batch_state else 0, s, 0, 0)),
        ],
        out_specs=[
            pl.BlockSpec((tt, w), lambda b, s, j: (b * nt + j * s, 0)),
            pl.BlockSpec((1, 1, GLA_DK, w), lambda b, s, j: (b, s, 0, 0)),
        ],
        scratch_shapes=[pltpu.VMEM((w, w), F32), pltpu.VMEM((tt, w), F32),
                        pltpu.VMEM((seq, w), F32)]
        + [pltpu.VMEM((tt, w), BF16)] * 4
        + [pltpu.VMEM((tt // GLA_CHUNK, w, w), BF16),
           pltpu.VMEM((tt // GLA_CHUNK, 8, w), F32)],
        compiler_params=_params(3),
        name="gla",
    )(gin, wg, bg, norm_g.reshape(1, w), s0)


def _lru_sweep(fwd, j, nt, tt, xp_ref, xc_ref, xn_ref, lg_ref, cw_ref, cb_ref,
               wg_ref, bg_ref, lam_ref, h0_ref, o_ref, sout_ref,
               carry_scr, a_scr, ab_scr, hl_scr, pl_scr, h_scr, hb_scr, xcv_scr):
    w = LRU_WIDTH
    seg_len = tt // 8
    tile = j if fwd else nt - 1 - j

    @pl.when(j == 0)
    def _():
        carry_scr[...] = jnp.broadcast_to(h0_ref[0, 0], (8, w))

    all_rows = pl.ds(pl.multiple_of(tile * tt, 8), tt)
    if fwd:
        xc = xcv_scr[all_rows, :]
    else:
        pre = jnp.where(tile > 0, xp_ref[...], 0.0)
        post = jnp.where(tile < nt - 1, xn_ref[...], 0.0)
        xe = jnp.concatenate([pre, xc_ref[...], post], axis=0)
        ne = tt + 16
        xc = cb_ref[...] + cw_ref[2:3, :] * xe[8:8 + tt]
        xc = xc + cw_ref[0:1, :] * pltpu.roll(xe, 2, 0)[8:8 + tt]
        xc = xc + cw_ref[1:2, :] * pltpu.roll(xe, 1, 0)[8:8 + tt]
        xc = xc + cw_ref[3:4, :] * pltpu.roll(xe, ne - 1, 0)[8:8 + tt]
        xcv_scr[all_rows, :] = xc

    ri = _dot(xc.astype(BF16), wg_ref[0]) + bg_ref[0]
    r = _sigmoid(ri[:, 0:w])
    gate_i = _sigmoid(ri[:, w:2 * w])
    log_a = r * ((-LRU_C) * _softplus(-lam_ref[0]))
    a = jnp.exp(log_a)
    bt = jnp.sqrt(1.0 - a * a) * (gate_i * xc)

    nl = w // 128
    pitch = seg_len + LRU_SEG_PAD
    for lt in range(nl):
        for k in range(8):
            dst = slice(k * pitch, k * pitch + seg_len)
            src = slice(k * seg_len, (k + 1) * seg_len)
            a_scr[lt, dst, :] = a[src, lt * 128:(lt + 1) * 128]
            ab_scr[lt, dst, :] = bt[src, lt * 128:(lt + 1) * 128]

    def step(ii, hp):
        i = ii if fwd else seg_len - 1 - ii
        seg_rows = pl.ds(i, 8, stride=pitch)
        rows = pl.ds(pl.multiple_of(i * 8, 8), 8)
        out = []
        for lt in range(nl):
            h, p = hp[lt]
            a_i = a_scr[lt, seg_rows, :]
            h = a_i * h + ab_scr[lt, seg_rows, :]
            p = a_i * p
            hl_scr[lt, rows, :] = h
            pl_scr[lt, rows, :] = p
            out.append((h, p))
        return tuple(out)

    init = tuple((jnp.zeros((8, 128), F32), jnp.ones((8, 128), F32)) for _ in range(nl))
    ends = lax.fori_loop(0, seg_len, step, init, unroll=8)
    h_end = jnp.concatenate([e[0] for e in ends], axis=1)
    p_end = jnp.concatenate([e[1] for e in ends], axis=1)

    c = carry_scr[0:1, :]
    c_in = [None] * 8
    for k in (range(8) if fwd else range(7, -1, -1)):
        c_in[k] = c
        c = h_end[k:k + 1, :] + p_end[k:k + 1, :] * c
    carry_scr[...] = jnp.broadcast_to(c, (8, w))

    for k in range(8):
        for i0 in range(0, seg_len, 8):
            src = pl.ds(i0 * 8 + k, 8, stride=8)
            r0 = k * seg_len + i0
            for lt in range(nl):
                c_k = c_in[k][:, lt * 128:(lt + 1) * 128]
                h_scr[lt, r0:r0 + 8, :] = hl_scr[lt, src, :] + pl_scr[lt, src, :] * c_k
    h_all = jnp.concatenate([h_scr[lt] for lt in range(nl)], axis=1)
    if fwd:
        o_ref[...] = (h_all + hb_scr[all_rows, :]) * _gelu_tanh(lg_ref[...])
    else:
        hb_scr[all_rows, :] = h_all

    @pl.when(j == nt - 1)
    def _():
        sout_ref[0, 0] = carry_scr[...]


def _lru_kernel(xp_ref, xc_ref, xn_ref, lg_ref, cw_ref, cb_ref, wg_ref, bg_ref,
                lam_ref, h0_ref, o_ref, sout_ref, carry_scr, a_scr, ab_scr, hl_scr, pl_scr,
                h_scr, hb_scr, xcv_scr, *, nt, tt):
    sweep = pl.program_id(1)
    j = pl.program_id(2)
    args = (j, nt, tt, xp_ref, xc_ref, xn_ref, lg_ref, cw_ref, cb_ref, wg_ref, bg_ref,
            lam_ref, h0_ref, o_ref, sout_ref, carry_scr, a_scr, ab_scr, hl_scr, pl_scr,
            h_scr, hb_scr, xcv_scr)

    @pl.when(sweep == 0)
    def _():
        _lru_sweep(False, *args)

    @pl.when(sweep == 1)
    def _():
        _lru_sweep(True, *args)


def _lru(lx, lg, conv_w, conv_b, wg, bg, lam, h0, batch, seq, tt):
    nt = seq // tt
    w = LRU_WIDTH
    t8 = tt // 8
    n8 = seq // 8
    per_batch_state = h0.shape[0] == batch
    tile_of = lambda s, j: j * (2 * s - 1) + (1 - s) * (nt - 1)
    return pl.pallas_call(
        functools.partial(_lru_kernel, nt=nt, tt=tt),
        out_shape=[jax.ShapeDtypeStruct((batch * seq, w), F32),
                   jax.ShapeDtypeStruct((batch, 2, 8, w), F32)],
        grid=(batch, 2, nt),
        in_specs=[
            pl.BlockSpec((8, w), lambda b, s, j:
                         (b * n8 + jnp.maximum(tile_of(s, j) * t8 - 1, 0), 0)),
            pl.BlockSpec((tt, w), lambda b, s, j: (b * nt + tile_of(s, j), 0)),
            pl.BlockSpec((8, w), lambda b, s, j:
                         (b * n8 + jnp.minimum((tile_of(s, j) + 1) * t8, n8 - 1), 0)),
            pl.BlockSpec((tt, w), lambda b, s, j: (b * nt + j * s, 0)),
            _const_spec((4, w)),
            _const_spec((1, w)),
            pl.BlockSpec((1, w, 2 * w), lambda b, s, j: (s, 0, 0)),
            pl.BlockSpec((1, 1, 2 * w), lambda b, s, j: (s, 0, 0)),
            pl.BlockSpec((1, 1, w), lambda b, s, j: (s, 0, 0)),
            pl.BlockSpec((1, 1, 1, w),
                         lambda b, s, j: (b if per_batch_state else 0, s, 0, 0)),
        ],
        out_specs=[
            pl.BlockSpec((tt, w), lambda b, s, j: (b * nt + j * s, 0)),
            pl.BlockSpec((1, 1, 8, w), lambda b, s, j: (b, s, 0, 0)),
        ],
        scratch_shapes=[pltpu.VMEM((8, w), F32)]
        + [pltpu.VMEM((w // 128, tt + 8 * LRU_SEG_PAD, 128), F32)] * 2
        + [pltpu.VMEM((w // 128, tt, 128), F32)] * 3
        + [pltpu.VMEM((seq, w), F32)] * 2,
        compiler_params=_params(3),
        name="lru",
    )(lx, lx, lx, lg, conv_w, conv_b.reshape(1, w), wg, bg, lam, h0)


def _mlp_kernel(x_ref, att_ref, gla_ref, lru_ref, mod_ref, n2_ref, fn_ref,
                wo_ref, w1_ref, w2_ref, o_ref, *, final, ff_chunk):
    d = D_MODEL
    mix = _dot(att_ref[...], wo_ref[0, 0:ATTN_WIDTH, :])
    mix = mix + _dot(gla_ref[...].astype(BF16), wo_ref[0, ATTN_WIDTH:ATTN_WIDTH + GLA_WIDTH, :])
    mix = mix + _dot(lru_ref[...].astype(BF16), wo_ref[0, ATTN_WIDTH + GLA_WIDTH:d, :])
    g1 = mod_ref[0, 0, :, 2 * d:3 * d]
    sh2 = mod_ref[0, 0, :, 3 * d:4 * d]
    sc2 = mod_ref[0, 0, :, 4 * d:5 * d]
    g2 = mod_ref[0, 0, :, 5 * d:6 * d]
    x = x_ref[...] + g1 * mix
    ms = jnp.mean(x * x, axis=-1, keepdims=True)
    h = (x * lax.rsqrt(ms + EPS) * n2_ref[...] * (1.0 + sc2) + sh2).astype(BF16)
    y = None
    for c in range(D_FF // ff_chunk):
        cols = slice(c * ff_chunk, (c + 1) * ff_chunk)
        u = jnp.maximum(_dot(h, w1_ref[0, :, cols]), 0.0)
        part = _dot((u * u).astype(BF16), w2_ref[0, cols, :])
        y = part if y is None else y + part
    x = x + g2 * y
    if final:
        ms = jnp.mean(x * x, axis=-1, keepdims=True)
        x = x * lax.rsqrt(ms + EPS) * fn_ref[...]
    o_ref[...] = x


def _out_mlp(x2d, att, gla, lru, mod, norm2, final_norm, wo, w1, w2, layer,
             rows_per_mod, mod_row0, tt, final):
    n, d = x2d.shape
    nt = n // tt
    mod_map = lambda i: (layer, mod_row0 + (i * tt) // rows_per_mod, 0, 0)
    row = lambda wd: pl.BlockSpec((tt, wd), lambda i: (i, 0))
    resident = lambda shape: pl.BlockSpec((1,) + shape, lambda i: (layer, 0, 0),
                                          pipeline_mode=pl.Buffered(1))
    return pl.pallas_call(
        functools.partial(_mlp_kernel, final=final, ff_chunk=1024),
        out_shape=jax.ShapeDtypeStruct((n, d), F32),
        grid=(nt,),
        in_specs=[row(d), row(ATTN_WIDTH), row(GLA_WIDTH), row(LRU_WIDTH),
                  pl.BlockSpec((1, 1, 1, 6 * d), mod_map),
                  _const_spec((1, d)), _const_spec((1, d)),
                  resident((d, d)), resident((d, D_FF)), resident((D_FF, d))],
        out_specs=row(d),
        compiler_params=_params(1),
        name="out_mlp",
    )(x2d, att, gla, lru, mod, norm2.reshape(1, d), final_norm.reshape(1, d), wo, w1, w2)


def _prep_w_in(w_in):
    wz = w_in[..., C_A_END:C_B].astype(BF16)
    pad = jnp.zeros(w_in.shape[:-1] + (GZ_PAD - (C_B - C_A_END),), BF16)
    return (w_in[..., :C_A_END].astype(BF16), jnp.concatenate([wz, pad], axis=-1),
            w_in[..., C_B:].astype(BF16))


def _prep_gate_w(gw):
    out = []
    nz = 2 * GLA_GATE_RANK
    for dr in (1, 0):
        wf = jnp.zeros((nz, GLA_WIDTH), F32)
        wf = wf.at[dr * GLA_GATE_RANK:(dr + 1) * GLA_GATE_RANK].set(gw[dr])
        hi, lo = _split_bf16(wf)
        out.append(jnp.concatenate([hi, hi, lo, jnp.zeros((GZ_PAD - 3 * nz, GLA_WIDTH), BF16)],
                                   axis=0))
    return jnp.stack(out)


def _block_diag(w4):
    n, a, b = w4.shape
    eye = jnp.eye(n, dtype=w4.dtype)
    return jnp.einsum("nab,nm->namb", w4, eye).reshape(n * a, n * b)


def _rope_tables(seq):
    rows = seq // GRID_W
    nf = HEAD_DIM // 4
    inv = ROPE_BASE ** (-jnp.arange(nf, dtype=F32) / nf)
    ang_r = jnp.arange(rows, dtype=F32)[:, None] * inv
    ang_c = jnp.arange(GRID_W, dtype=F32)[:, None] * inv
    by_row = lambda t: jnp.repeat(t, GRID_W, axis=0)
    by_col = lambda t: jnp.tile(t, (rows, 1))
    cos_r, sin_r = by_row(jnp.cos(ang_r)), by_row(jnp.sin(ang_r))
    cos_c, sin_c = by_col(jnp.cos(ang_c)), by_col(jnp.sin(ang_c))
    cos = jnp.concatenate([cos_r, cos_r, cos_c, cos_c], axis=1)
    sin = jnp.concatenate([-sin_r, sin_r, -sin_c, sin_c], axis=1)
    return jnp.concatenate([cos] * 2, axis=1), jnp.concatenate([sin] * 2, axis=1)


def _gla_state_to_internal(s):
    b = s.shape[0]
    return jnp.transpose(s, (0, 1, 4, 2, 3)).reshape(b, 2, GLA_DK, GLA_WIDTH)


def kernel(x_prompt, x_sample, c, cache_k, cache_v, state_gla, state_lru, c_ctx, w_mod, b_mod, norm1, norm2, w_in, attn_sink, gla_gate_w, gla_gate_b, gla_norm, lru_conv_w, lru_conv_b, lru_wa, lru_ba, lru_wx, lru_bx, lru_lambda, w_out, w_mlp1, w_mlp2, final_norm):
    depth = w_in.shape[0]
    bc, sc_len, d = x_prompt.shape
    bl, sl_len, _ = x_sample.shape
    past = cache_k.shape[2]

    cond = jnp.concatenate([c_ctx[None], c, jnp.zeros((MOD_ROWS - 1 - bl, d), F32)], axis=0)
    mod = _ada_mod(cond, w_mod, b_mod)

    flip_dir = lambda a: a[:, ::-1]
    rope_tabs = _rope_tables(sl_len)
    ck = cache_k.reshape(bl, depth, past, KV_WIDTH)
    cv = cache_v.reshape(bl, depth, past, KV_WIDTH)
    gla_s0_ctx = jnp.zeros((1, 2, GLA_DK, GLA_WIDTH), F32)
    lru_s0_ctx = jnp.zeros((1, 2, 1, LRU_WIDTH), F32)

    xp = x_prompt.reshape(bc * sc_len, d)
    xs = x_sample.reshape(bl * sl_len, d)
    w_in_b = _prep_w_in(w_in)
    wo = w_out.astype(BF16)
    w1 = w_mlp1.astype(BF16)
    w2 = w_mlp2.astype(BF16)
    ks, vs, sgs, sls = [], [], [], []
    for l in range(depth):
        gate_w = _prep_gate_w(gla_gate_w[l])
        gate_b = gla_gate_b[l][::-1].reshape(2, 1, GLA_WIDTH)
        lru_wg = jnp.stack([jnp.concatenate([_block_diag(lru_wa[l, dr]), _block_diag(lru_wx[l, dr])],
                                            axis=1) for dr in (1, 0)]).astype(BF16)
        lru_bg = jnp.stack([jnp.concatenate([lru_ba[l, dr], lru_bx[l, dr]]) for dr in (1, 0)]
                           ).reshape(2, 1, 2 * LRU_WIDTH)
        lam = lru_lambda[l][::-1].reshape(2, 1, LRU_WIDTH)
        final = l == depth - 1

        for is_lat in (False, True):
            if is_lat:
                x2d, batch, seq, row0, rpm = xs, bl, sl_len, 1, sl_len
                tabs, tt_proj, tq, tt_rec = rope_tabs, 512, 256, 512
                gla_s0 = flip_dir(_gla_state_to_internal(state_gla[:, l]))
                lru_s0 = flip_dir(state_lru[:, l]).reshape(bl, 2, 1, LRU_WIDTH)
            else:
                x2d, batch, seq, row0, rpm = xp, bc, sc_len, 0, bc * sc_len
                tabs, tt_proj, tq, tt_rec = None, 512, sc_len, sc_len
                gla_s0, lru_s0 = gla_s0_ctx, lru_s0_ctx
            q, k, v, gin, lx, lg = _in_proj(x2d, mod, norm1[l], w_in_b, l, tabs, rpm, row0, tt_proj)
            att = _attention(q, k, v, attn_sink[l], batch, seq, tq,
                             ctx_kv=(ck, cv, l) if is_lat else None)
            gla, sg = _gla(gin, gate_w, gate_b, gla_norm[l], gla_s0, batch, seq, tt_rec)
            lru, sl = _lru(lx, lg, lru_conv_w[l], lru_conv_b[l], lru_wg, lru_bg, lam, lru_s0,
                           batch, seq, tt_rec)
            x2d = _out_mlp(x2d, att, gla, lru, mod, norm2[l], final_norm, wo, w1, w2, l,
                           rpm, row0, 512, final)
            if is_lat:
                xs = x2d
            else:
                xp = x2d
                ks.append(k.reshape(bc, sc_len, KV_WIDTH))
                vs.append(v.reshape(bc, sc_len, KV_WIDTH))
                sgs.append(sg)
                sls.append(sl[:, :, 0, :])

    y_prompt = xp.reshape(bc, sc_len, d)
    y_sample = xs.reshape(bl, sl_len, d)
    kv_shape = (bc, depth, sc_len, KV_WIDTH // HEAD_DIM, HEAD_DIM)
    new_k = jnp.stack(ks, axis=1).reshape(kv_shape)
    new_v = jnp.stack(vs, axis=1).reshape(kv_shape)
    sg_all = jnp.stack(sgs, axis=1)[:, :, ::-1]
    new_sg = jnp.transpose(
        sg_all.reshape(bc, depth, 2, GLA_DK, GLA_HEADS, GLA_DK), (0, 1, 2, 4, 5, 3))
    new_sl = jnp.stack(sls, axis=1)[:, :, ::-1]
    return (y_prompt, y_sample, new_k, new_v, new_sg, new_sl)
```

```python
import functools

import jax
import jax.numpy as jnp
from jax import lax
from jax.experimental import pallas as pl
from jax.experimental.pallas import tpu as pltpu

F32 = jnp.float32
BF16 = jnp.bfloat16

D_MODEL = 1024
GRID_W = 64
EPS = 1e-6
HEAD_DIM = 64
ATTN_WIDTH = 512
ATTN_HEADS = 8
KV_WIDTH = 128
WINDOW = 128
ATTN_LOOKAHEAD = 1
ROPE_BASE = 10000.0
NEG_INF = -1e30
GLA_WIDTH = 256
GLA_DK = 64
GLA_HEADS = 4
GLA_GATE_RANK = 16
GLA_GATE_NORM = 16.0
GLA_CHUNK = 64
GLA_SKEW = 2
LRU_WIDTH = 256
LRU_C = 8.0
LRU_SEG_PAD = 8
D_FF = 4096
MOD_ROWS = 8
GZ_PAD = 128
C_Q, C_K, C_V, C_GLA, C_A_END, C_B = 0, 512, 640, 768, 1536, 1568
GLA_IN = 4 * GLA_WIDTH + GZ_PAD
VMEM_LIMIT = 56 * 1024 * 1024

NT_DIMS = (((1,), (1,)), ((), ()))


def _split_bf16(a):
    hi = a.astype(BF16)
    lo = (a - hi.astype(F32)).astype(BF16)
    return hi, lo


def _dot(a, b):
    return jnp.dot(a, b, preferred_element_type=F32)


def _dot_nt(a, b):
    return lax.dot_general(a, b, NT_DIMS, preferred_element_type=F32)


def _dot_x3(a, b):
    ah, al = _split_bf16(a)
    bh, bl = _split_bf16(b)
    return _dot(ah, bh) + (_dot(ah, bl) + _dot(al, bh))


def _softplus(y):
    return jnp.maximum(y, 0.0) + jnp.log1p(jnp.exp(-jnp.abs(y)))


def _log_sigmoid(y):
    return jnp.minimum(y, 0.0) - jnp.log(1.0 + jnp.exp(-jnp.abs(y)))


def _sigmoid(y):
    return 0.5 * jnp.tanh(0.5 * y) + 0.5


def _silu(y):
    return y * _sigmoid(y)


def _gelu_tanh(y):
    c = 0.7978845608028654
    hy = 0.5 * y
    return hy * jnp.tanh(y * (c + (c * 0.044715) * (y * y))) + hy


def _params(n_axes):
    return pltpu.CompilerParams(
        dimension_semantics=("arbitrary",) * n_axes, vmem_limit_bytes=VMEM_LIMIT)


def _const_spec(shape):
    nd = len(shape)
    return pl.BlockSpec(shape, lambda *_: (0,) * nd)


def _ada_kernel(c_ref, w_ref, b_ref, o_ref):
    s = _silu(c_ref[...])
    res = _dot_x3(s, w_ref[0]) + b_ref[0]
    for r in range(MOD_ROWS):
        o_ref[0, r] = res[r:r + 1, :]


def _ada_mod(cond, w_mod, b_mod):
    depth, d, n = w_mod.shape
    tn = 1536
    return pl.pallas_call(
        _ada_kernel,
        out_shape=jax.ShapeDtypeStruct((depth, MOD_ROWS, 1, n), F32),
        grid=(depth, n // tn),
        in_specs=[
            pl.BlockSpec((MOD_ROWS, d), lambda l, j: (0, 0)),
            pl.BlockSpec((1, d, tn), lambda l, j: (l, 0, j)),
            pl.BlockSpec((1, 1, tn), lambda l, j: (l, 0, j)),
        ],
        out_specs=pl.BlockSpec((1, MOD_ROWS, 1, tn), lambda l, j: (l, 0, 0, j)),
        compiler_params=_params(2),
        name="ada_mod",
    )(cond, w_mod, b_mod.reshape(depth, 1, n))


def _rope(x, cos, sin_signed):
    w = x.shape[1]
    lane = lax.broadcasted_iota(jnp.int32, (1, w), 1)
    first = (lane % 32) < 16
    swapped = jnp.where(first, pltpu.roll(x, w - 16, 1), pltpu.roll(x, 16, 1))
    return x * cos + swapped * sin_signed


def _half_variants(a):
    lane = lax.broadcasted_iota(jnp.int32, (1, 128), 1)
    low = lane < HEAD_DIM
    ar = pltpu.roll(a, HEAD_DIM, 1)
    z = jnp.zeros_like(a)
    parts = [jnp.where(low, a, z), jnp.where(low, z, ar),
             jnp.where(low, ar, z), jnp.where(low, z, a)]
    return jnp.concatenate([t.astype(BF16) for t in parts], axis=1)


def _inproj_kernel(*refs, rope):
    if rope:
        (x_ref, mod_ref, n1_ref, wa_ref, wz_ref, wb_ref, cos_ref, sin_ref,
         q_ref, kvar_ref, vvar_ref, gla_ref, lx_ref, lg_ref) = refs
    else:
        (x_ref, mod_ref, n1_ref, wa_ref, wz_ref, wb_ref,
         q_ref, kvar_ref, vvar_ref, k_ref, v_ref, gla_ref, lx_ref, lg_ref) = refs
    d = D_MODEL
    x = x_ref[...]
    ms = jnp.mean(x * x, axis=-1, keepdims=True)
    xn = x * lax.rsqrt(ms + EPS) * n1_ref[...]
    sh = mod_ref[0, 0, :, 0:d]
    sc = mod_ref[0, 0, :, d:2 * d]
    h = (xn * (1.0 + sc) + sh).astype(BF16)
    pa = _dot(h, wa_ref[0])
    pb = _dot(h, wb_ref[0])
    q = pa[:, C_Q:C_K]
    k = pa[:, C_K:C_V]
    v = pa[:, C_V:C_GLA]
    if rope:
        cos = cos_ref[...]
        sin = sin_ref[...]
        k = _rope(k, cos, sin)
        q = _rope(q, jnp.concatenate([cos] * 4, axis=1), jnp.concatenate([sin] * 4, axis=1))
    q_ref[...] = (q * (HEAD_DIM ** -0.5)).astype(BF16)
    kvar_ref[...] = _half_variants(k)
    vvar_ref[...] = _half_variants(v)
    if not rope:
        k_ref[...] = k
        v_ref[...] = v
    w = GLA_WIDTH
    gla_ref[:, 0:3 * w] = pa[:, C_GLA:C_A_END]
    gla_ref[:, 3 * w:4 * w] = pb[:, 0:w]
    gla_ref[:, 4 * w:4 * w + GZ_PAD] = _dot(h, wz_ref[0])
    lx_ref[...] = pb[:, w:w + LRU_WIDTH]
    lg_ref[...] = pb[:, w + LRU_WIDTH:w + 2 * LRU_WIDTH]


def _in_proj(x2d, mod, norm1, w_abz, layer, rope_tabs, rows_per_mod, mod_row0, tt):
    n, d = x2d.shape
    nt = n // tt
    rope = rope_tabs is not None
    mod_map = lambda i: (layer, mod_row0 + (i * tt) // rows_per_mod, 0, 0)
    by_layer = lambda arr: pl.BlockSpec((1,) + arr.shape[1:], lambda i: (layer, 0, 0))
    in_specs = [
        pl.BlockSpec((tt, d), lambda i: (i, 0)),
        pl.BlockSpec((1, 1, 1, 6 * d), mod_map),
        _const_spec((1, d)),
    ] + [by_layer(arr) for arr in w_abz]
    args = [x2d, mod, norm1.reshape(1, d)] + list(w_abz)
    if rope:
        t = rope_tabs[0].shape[0]
        tpb = t // tt
        in_specs += [pl.BlockSpec((tt, 128), lambda i: (i % tpb, 0))] * 2
        args += list(rope_tabs)
    outs = [(ATTN_WIDTH, BF16), (4 * KV_WIDTH, BF16), (4 * KV_WIDTH, BF16)]
    if not rope:
        outs += [(KV_WIDTH, F32), (KV_WIDTH, F32)]
    outs += [(GLA_IN, F32), (LRU_WIDTH, F32), (LRU_WIDTH, F32)]
    widths = [o[0] for o in outs]
    dtypes = [o[1] for o in outs]
    return pl.pallas_call(
        functools.partial(_inproj_kernel, rope=rope),
        out_shape=[jax.ShapeDtypeStruct((n, wd), dt) for wd, dt in zip(widths, dtypes)],
        grid=(nt,),
        in_specs=in_specs,
        out_specs=[pl.BlockSpec((tt, wd), lambda i: (i, 0)) for wd in widths],
        compiler_params=_params(1),
        name="in_proj_lat" if rope else "in_proj_ctx",
    )(*args)


def _attn_kernel(*refs, latent, tq, seq):
    if latent:
        (sink_ref, q_ref, band_ref, kp_ref, kc_ref, kn_ref, vp_ref, vc_ref, vn_ref,
         ck_ref, cv_ref, o_ref) = refs
        i = pl.program_id(1)
        k_loc = jnp.concatenate([kp_ref[...], kc_ref[...], kn_ref[...]], axis=0)
        v_loc = jnp.concatenate([vp_ref[...], vc_ref[...], vn_ref[...]], axis=0)
        nk = tq + 2 * WINDOW
        col = lax.broadcasted_iota(jnp.int32, (1, nk), 1)
        kpos = i * tq - WINDOW + col
        bias = band_ref[...] + jnp.where((kpos >= 0) & (kpos < seq), 0.0, NEG_INF)
        k_ctx = ck_ref[0, 0]
        v_ctx = cv_ref[0, 0]
    else:
        sink_ref, q_ref, kc_ref, vc_ref, o_ref = refs
        k_loc = kc_ref[...]
        v_loc = vc_ref[...]
        bias = None
    block = lambda t, kv, half: t[:, (2 * kv + half) * 128:(2 * kv + half + 1) * 128]
    with_ones = lambda t: jnp.concatenate([t, jnp.ones_like(t)], axis=1)

    def scores(head):
        c, half = divmod(head, 2)
        qc = q_ref[:, c * 128:(c + 1) * 128]
        s = _dot_nt(qc, block(k_loc, c // 2, half))
        if bias is not None:
            s = s + bias
        return s, (_dot_nt(qc, block(k_ctx, c // 2, half)) if latent else None)

    ready = [scores(h) for h in range(ATTN_LOOKAHEAD)]
    acc = None
    for head in range(ATTN_HEADS):
        c, half = divmod(head, 2)
        s, s_c = ready.pop(0)
        if head + ATTN_LOOKAHEAD < ATTN_HEADS:
            ready.append(scores(head + ATTN_LOOKAHEAD))
        sink = sink_ref[head]
        m = jnp.maximum(jnp.max(s, axis=-1, keepdims=True), sink)
        if latent:
            m = jnp.maximum(m, jnp.max(s_c, axis=-1, keepdims=True))
        o = _dot(jnp.exp(s - m).astype(BF16), with_ones(block(v_loc, c // 2, half)))
        if latent:
            o = o + _dot(jnp.exp(s_c - m).astype(BF16), with_ones(block(v_ctx, c // 2, half)))
        denom = jnp.exp(sink - m) + o[:, 128:256]
        o = o[:, 0:128] * (1.0 / denom)
        if half == 0:
            acc = o
        else:
            o_ref[:, c * 128:(c + 1) * 128] = (acc + o).astype(BF16)


def _attention(q, k, v, sink, batch, seq, tq, ctx_kv=None):
    latent = ctx_kv is not None
    nq = seq // tq
    kern = functools.partial(_attn_kernel, latent=latent, tq=tq, seq=seq)
    smem = pl.BlockSpec(memory_space=pltpu.SMEM)
    kvw = 4 * KV_WIDTH
    if latent:
        ck, cv, layer = ctx_kv
        past = ck.shape[2]
        wb = tq // WINDOW
        nwb = seq // WINDOW
        nk = tq + 2 * WINDOW
        r = lax.broadcasted_iota(jnp.int32, (tq, nk), 0)
        j = lax.broadcasted_iota(jnp.int32, (tq, nk), 1)
        band = jnp.where((j >= r) & (j <= r + 2 * WINDOW), 0.0, NEG_INF).astype(F32)
        cur = pl.BlockSpec((tq, kvw), lambda b, i: (b * nq + i, 0))
        prev = pl.BlockSpec((WINDOW, kvw),
                            lambda b, i: (b * nwb + jnp.maximum(i * wb - 1, 0), 0))
        nxt = pl.BlockSpec((WINDOW, kvw),
                           lambda b, i: (b * nwb + jnp.minimum((i + 1) * wb, nwb - 1), 0))
        cspec = pl.BlockSpec((1, 1, past, kvw), lambda b, i: (b, layer, 0, 0))
        in_specs = [smem, pl.BlockSpec((tq, ATTN_WIDTH), lambda b, i: (b * nq + i, 0)),
                    _const_spec((tq, nk)), prev, cur, nxt, prev, cur, nxt, cspec, cspec]
        args = (sink, q, band, k, k, k, v, v, v, ck, cv)
    else:
        cur = pl.BlockSpec((tq, kvw), lambda b, i: (b * nq + i, 0))
        in_specs = [smem, pl.BlockSpec((tq, ATTN_WIDTH), lambda b, i: (b * nq + i, 0)), cur, cur]
        args = (sink, q, k, v)
    return pl.pallas_call(
        kern,
        out_shape=jax.ShapeDtypeStruct((batch * seq, ATTN_WIDTH), BF16),
        grid=(batch, nq),
        in_specs=in_specs,
        out_specs=pl.BlockSpec((tq, ATTN_WIDTH), lambda b, i: (b * nq + i, 0)),
        compiler_params=_params(2),
        name="attn_lat" if latent else "attn_ctx",
    )(*args)


def _gla_sweep(fwd, j, nt, tt, seq_rows, gin_ref, wg_ref, bg_ref, ng_ref, s0_ref, o_ref, sout_ref,
               st_scr, g_scr, ob_scr, qin_scr, kin_scr, kend_scr, a_scr, stc_scr, dec_scr):
    w = GLA_WIDTH
    ck = GLA_CHUNK
    nc = tt // ck
    cps = seq_rows // ck
    whole_seqs = nt == 1
    tile = j if fwd else nt - 1 - j
    starts = [(ci if fwd else nc - 1 - ci) * ck for ci in range(nc)]

    rr = lax.broadcasted_iota(jnp.int32, (w, w), 0) // GLA_DK
    cc = lax.broadcasted_iota(jnp.int32, (w, w), 1) // GLA_DK
    block_diag = rr == cc
    bd_bf = block_diag.astype(F32).astype(BF16)
    tr = lax.broadcasted_iota(jnp.int32, (ck, 2 * ck), 0)
    tc = lax.broadcasted_iota(jnp.int32, (ck, 2 * ck), 1) % ck
    tri2 = ((tr >= tc) if fwd else (tr <= tc)).astype(F32).astype(BF16)
    ar = lax.broadcasted_iota(jnp.int32, (ck, w), 0)
    ac = lax.broadcasted_iota(jnp.int32, (ck, w), 1) % ck
    causal = (ar >= ac) if fwd else (ar <= ac)

    if not whole_seqs:
        @pl.when(j == 0)
        def _():
            st_scr[...] = s0_ref[0, 0]

    zpad = gin_ref[:, 4 * w:4 * w + GZ_PAD]
    zh = zpad.astype(BF16).astype(F32)
    zcat = zh + pltpu.roll(zpad - zh, 32, 1) + pltpu.roll(zh, 64, 1)
    logit = _dot(zcat.astype(BF16), wg_ref[0]) + bg_ref[0]
    g_scr[...] = _log_sigmoid(logit) * (1.0 / GLA_GATE_NORM)

    tile_rows = pl.multiple_of(tile * tt, ck)

    def decayed(ci, c0):
        rows = pl.ds(c0, ck)
        g = g_scr[rows, :]
        gh, gl = _split_bf16(g)
        b = _dot(tri2, jnp.concatenate([gh, gl], axis=0))
        btot = jnp.sum(g, axis=0, keepdims=True)
        k = gin_ref[rows, w:2 * w]
        qin_scr[rows, :] = (gin_ref[rows, 0:w] * (GLA_DK ** -0.5) * jnp.exp(b)).astype(BF16)
        kin_scr[rows, :] = (k * jnp.exp(-b)).astype(BF16)
        kend_scr[rows, :] = (k * jnp.exp(btot - b)).astype(BF16)
        dec_scr[ci] = jnp.broadcast_to(jnp.exp(btot), (8, w))

    def scores(ci, c0):
        rows = pl.ds(c0, ck)
        k_bd = jnp.concatenate([kin_scr[rows, :]] * GLA_HEADS, axis=0) * bd_bf
        a_scr[rows, :] = jnp.where(causal, _dot_nt(qin_scr[rows, :], k_bd), 0.0).astype(BF16)

    def state(ci, c0):
        rows = pl.ds(c0, ck)
        seq_i = c0 // seq_rows
        first = whole_seqs and ci % cps == 0
        st = s0_ref[min(seq_i, s0_ref.shape[0] - 1), 0] if first else st_scr[...]
        stc_scr[ci] = jnp.concatenate([st.astype(BF16)] * GLA_HEADS, axis=0) * bd_bf
        vt = gin_ref[rows, 2 * w:3 * w].T
        vt_heads = jnp.concatenate([vt[h * GLA_DK:(h + 1) * GLA_DK, :] for h in range(GLA_HEADS)],
                                   axis=1).astype(BF16)
        kend_bd = jnp.concatenate([kend_scr[rows, :]] * GLA_HEADS, axis=0) * bd_bf
        st = st * dec_scr[ci][0:1, :] + _dot(vt_heads, kend_bd)
        st_scr[...] = st
        if whole_seqs and ci % cps == cps - 1:
            sout_ref[seq_i, 0] = st

    def output(ci, c0):
        rows = pl.ds(c0, ck)
        v_bd = jnp.concatenate([gin_ref[rows, 2 * w:3 * w].astype(BF16)] * GLA_HEADS,
                               axis=0) * bd_bf
        o = _dot(a_scr[rows, :], v_bd) + _dot_nt(qin_scr[rows, :], stc_scr[ci])
        if fwd:
            o_ref[rows, :] = o
        else:
            ob_scr[pl.ds(tile_rows + c0, ck), :] = o

    stages = (decayed, scores, state, output)
    for step in range(nc + GLA_SKEW * (len(stages) - 1)):
        for si, stage in enumerate(stages):
            ci = step - GLA_SKEW * si
            if 0 <= ci < nc:
                stage(ci, starts[ci])

    if fwd:
        o = o_ref[...] + ob_scr[pl.ds(tile_rows, tt), :]
        ms = _dot((o * o).astype(BF16), bd_bf) * (1.0 / GLA_DK)
        r = gin_ref[:, 3 * w:4 * w]
        o_ref[...] = o * lax.rsqrt(ms + EPS) * ng_ref[...] * _silu(r)

    if not whole_seqs:
        @pl.when(j == nt - 1)
        def _():
            sout_ref[0, 0] = st_scr[...]


def _gla_kernel(gin_ref, wg_ref, bg_ref, ng_ref, s0_ref, o_ref, sout_ref,
                st_scr, g_scr, ob_scr, qin_scr, kin_scr, kend_scr, a_scr, stc_scr, dec_scr,
                *, nt, tt, seq_rows):
    sweep = pl.program_id(1)
    j = pl.program_id(2)
    args = (j, nt, tt, seq_rows, gin_ref, wg_ref, bg_ref, ng_ref, s0_ref, o_ref, sout_ref,
            st_scr, g_scr, ob_scr, qin_scr, kin_scr, kend_scr, a_scr, stc_scr, dec_scr)

    @pl.when(sweep == 0)
    def _():
        _gla_sweep(False, *args)

    @pl.when(sweep == 1)
    def _():
        _gla_sweep(True, *args)


def _gla(gin, wg, bg, norm_g, s0, batch, seq, tt):
    nt = max(seq // tt, 1)
    group = max(tt // seq, 1)
    w = GLA_WIDTH
    per_batch_state = s0.shape[0] == batch
    s0_group = group if per_batch_state else 1
    tile_of = lambda s, j: j * (2 * s - 1) + (1 - s) * (nt - 1)
    return pl.pallas_call(
        functools.partial(_gla_kernel, nt=nt, tt=tt, seq_rows=min(seq, tt)),
        out_shape=[jax.ShapeDtypeStruct((batch * seq, w), F32),
                   jax.ShapeDtypeStruct((batch, 2, GLA_DK, w), F32)],
        grid=(batch // group, 2, nt),
        in_specs=[
            pl.BlockSpec((tt, GLA_IN), lambda b, s, j: (b * nt + tile_of(s, j), 0)),
            pl.BlockSpec((1, GZ_PAD, w), lambda b, s, j: (1 - s, 0, 0)),
            pl.BlockSpec((1, 1, w), lambda b, s, j: (1 - s, 0, 0)),
            _const_spec((1, w)),
            pl.BlockSpec((s0_group, 1, GLA_DK, w),
                         lambda b, s, j: (b if per_batch_state else 0, 1 - s, 0, 0)),
        ],
        out_specs=[
            pl.BlockSpec((tt, w), lambda b, s, j: (b * nt + j * s, 0)),
            pl.BlockSpec((group, 1, GLA_DK, w), lambda b, s, j: (b, 1 - s, 0, 0)),
        ],
        scratch_shapes=[pltpu.VMEM((GLA_DK, w), F32), pltpu.VMEM((tt, w), F32),
                        pltpu.VMEM((nt * tt, w), F32)]
        + [pltpu.VMEM((tt, w), BF16)] * 4
        + [pltpu.VMEM((tt // GLA_CHUNK, w, w), BF16),
           pltpu.VMEM((tt // GLA_CHUNK, 8, w), F32)],
        compiler_params=_params(3),
        name="gla",
    )(gin, wg, bg, norm_g.reshape(1, w), s0)


def _lru_sweep(fwd, j, nt, tt, xp_ref, xc_ref, xn_ref, lg_ref, cw_ref, cb_ref,
               wg_ref, bg_ref, lam_ref, h0_ref, o_ref, sout_ref,
               carry_scr, a_scr, ab_scr, hl_scr, pl_scr, h_scr, hb_scr, xcv_scr):
    w = LRU_WIDTH
    seg_len = tt // 8
    tile = j if fwd else nt - 1 - j

    @pl.when(j == 0)
    def _():
        carry_scr[...] = jnp.broadcast_to(h0_ref[0, 0], (8, w))

    all_rows = pl.ds(pl.multiple_of(tile * tt, 8), tt)
    if fwd:
        xc = xcv_scr[all_rows, :]
    else:
        pre = jnp.where(tile > 0, xp_ref[...], 0.0)
        post = jnp.where(tile < nt - 1, xn_ref[...], 0.0)
        xe = jnp.concatenate([pre, xc_ref[...], post], axis=0)
        ne = tt + 16
        xc = cb_ref[...] + cw_ref[2:3, :] * xe[8:8 + tt]
        xc = xc + cw_ref[0:1, :] * pltpu.roll(xe, 2, 0)[8:8 + tt]
        xc = xc + cw_ref[1:2, :] * pltpu.roll(xe, 1, 0)[8:8 + tt]
        xc = xc + cw_ref[3:4, :] * pltpu.roll(xe, ne - 1, 0)[8:8 + tt]
        xcv_scr[all_rows, :] = xc

    ri = _dot(xc.astype(BF16), wg_ref[0]) + bg_ref[0]
    t_a = jnp.tanh(ri[:, 0:w])
    t_x = jnp.tanh(ri[:, w:2 * w])
    c2 = (-0.5 * LRU_C) * _softplus(-lam_ref[0])
    a = jnp.exp(t_a * c2 + c2)
    hx = 0.5 * xc
    bt = jnp.sqrt(1.0 - a * a) * (t_x * hx + hx)

    nl = w // 128
    pitch = seg_len + LRU_SEG_PAD
    for lt in range(nl):
        for k in range(8):
            dst = slice(k * pitch, k * pitch + seg_len)
            src = slice(k * seg_len, (k + 1) * seg_len)
            a_scr[lt, dst, :] = a[src, lt * 128:(lt + 1) * 128]
            ab_scr[lt, dst, :] = bt[src, lt * 128:(lt + 1) * 128]

    def step(ii, hp):
        i = ii if fwd else seg_len - 1 - ii
        seg_rows = pl.ds(i, 8, stride=pitch)
        rows = pl.ds(pl.multiple_of(i * 8, 8), 8)
        out = []
        for lt in range(nl):
            h, p = hp[lt]
            a_i = a_scr[lt, seg_rows, :]
            h = a_i * h + ab_scr[lt, seg_rows, :]
            p = a_i * p
            hl_scr[lt, rows, :] = h
            pl_scr[lt, rows, :] = p
            out.append((h, p))
        return tuple(out)

    init = tuple((jnp.zeros((8, 128), F32), jnp.ones((8, 128), F32)) for _ in range(nl))
    ends = lax.fori_loop(0, seg_len, step, init, unroll=8)
    h_end = jnp.concatenate([e[0] for e in ends], axis=1)
    p_end = jnp.concatenate([e[1] for e in ends], axis=1)

    c = carry_scr[0:1, :]
    c_in = [None] * 8
    for k in (range(8) if fwd else range(7, -1, -1)):
        c_in[k] = c
        c = h_end[k:k + 1, :] + p_end[k:k + 1, :] * c
    carry_scr[...] = jnp.broadcast_to(c, (8, w))

    for k in range(8):
        for i0 in range(0, seg_len, 8):
            src = pl.ds(i0 * 8 + k, 8, stride=8)
            r0 = k * seg_len + i0
            for lt in range(nl):
                c_k = c_in[k][:, lt * 128:(lt + 1) * 128]
                h_scr[lt, r0:r0 + 8, :] = hl_scr[lt, src, :] + pl_scr[lt, src, :] * c_k
    h_all = jnp.concatenate([h_scr[lt] for lt in range(nl)], axis=1)
    if fwd:
        o_ref[...] = (h_all + hb_scr[all_rows, :]) * _gelu_tanh(lg_ref[...])
    else:
        hb_scr[all_rows, :] = h_all

    @pl.when(j == nt - 1)
    def _():
        sout_ref[0, 0] = carry_scr[...]


def _lru_kernel(xp_ref, xc_ref, xn_ref, lg_ref, cw_ref, cb_ref, wg_ref, bg_ref,
                lam_ref, h0_ref, o_ref, sout_ref, carry_scr, a_scr, ab_scr, hl_scr, pl_scr,
                h_scr, hb_scr, xcv_scr, *, nt, tt):
    sweep = pl.program_id(1)
    j = pl.program_id(2)
    args = (j, nt, tt, xp_ref, xc_ref, xn_ref, lg_ref, cw_ref, cb_ref, wg_ref, bg_ref,
            lam_ref, h0_ref, o_ref, sout_ref, carry_scr, a_scr, ab_scr, hl_scr, pl_scr,
            h_scr, hb_scr, xcv_scr)

    @pl.when(sweep == 0)
    def _():
        _lru_sweep(False, *args)

    @pl.when(sweep == 1)
    def _():
        _lru_sweep(True, *args)


def _lru(lx, lg, conv_w, conv_b, wg, bg, lam, h0, batch, seq, tt):
    nt = seq // tt
    w = LRU_WIDTH
    t8 = tt // 8
    n8 = seq // 8
    per_batch_state = h0.shape[0] == batch
    tile_of = lambda s, j: j * (2 * s - 1) + (1 - s) * (nt - 1)
    return pl.pallas_call(
        functools.partial(_lru_kernel, nt=nt, tt=tt),
        out_shape=[jax.ShapeDtypeStruct((batch * seq, w), F32),
                   jax.ShapeDtypeStruct((batch, 2, 8, w), F32)],
        grid=(batch, 2, nt),
        in_specs=[
            pl.BlockSpec((8, w), lambda b, s, j:
                         (b * n8 + jnp.maximum(tile_of(s, j) * t8 - 1, 0), 0)),
            pl.BlockSpec((tt, w), lambda b, s, j: (b * nt + tile_of(s, j), 0)),
            pl.BlockSpec((8, w), lambda b, s, j:
                         (b * n8 + jnp.minimum((tile_of(s, j) + 1) * t8, n8 - 1), 0)),
            pl.BlockSpec((tt, w), lambda b, s, j: (b * nt + j * s, 0)),
            _const_spec((4, w)),
            _const_spec((1, w)),
            pl.BlockSpec((1, w, 2 * w), lambda b, s, j: (1 - s, 0, 0)),
            pl.BlockSpec((1, 1, 2 * w), lambda b, s, j: (1 - s, 0, 0)),
            pl.BlockSpec((1, 1, w), lambda b, s, j: (1 - s, 0, 0)),
            pl.BlockSpec((1, 1, 1, w),
                         lambda b, s, j: (b if per_batch_state else 0, 1 - s, 0, 0)),
        ],
        out_specs=[
            pl.BlockSpec((tt, w), lambda b, s, j: (b * nt + j * s, 0)),
            pl.BlockSpec((1, 1, 8, w), lambda b, s, j: (b, 1 - s, 0, 0)),
        ],
        scratch_shapes=[pltpu.VMEM((8, w), F32)]
        + [pltpu.VMEM((w // 128, tt + 8 * LRU_SEG_PAD, 128), F32)] * 2
        + [pltpu.VMEM((w // 128, tt, 128), F32)] * 3
        + [pltpu.VMEM((seq, w), F32)] * 2,
        compiler_params=_params(3),
        name="lru",
    )(lx, lx, lx, lg, conv_w, conv_b.reshape(1, w), wg, bg, lam, h0)


def _mlp_kernel(x_ref, att_ref, gla_ref, lru_ref, mod_ref, n2_ref, fn_ref,
                wo_ref, w1_ref, w2_ref, o_ref, *, final, ff_chunk):
    d = D_MODEL
    mix = _dot(att_ref[...], wo_ref[0, 0:ATTN_WIDTH, :])
    mix = mix + _dot(gla_ref[...].astype(BF16), wo_ref[0, ATTN_WIDTH:ATTN_WIDTH + GLA_WIDTH, :])
    mix = mix + _dot(lru_ref[...].astype(BF16), wo_ref[0, ATTN_WIDTH + GLA_WIDTH:d, :])
    g1 = mod_ref[0, 0, :, 2 * d:3 * d]
    sh2 = mod_ref[0, 0, :, 3 * d:4 * d]
    sc2 = mod_ref[0, 0, :, 4 * d:5 * d]
    g2 = mod_ref[0, 0, :, 5 * d:6 * d]
    x = x_ref[...] + g1 * mix
    ms = jnp.mean(x * x, axis=-1, keepdims=True)
    h = (x * lax.rsqrt(ms + EPS) * n2_ref[...] * (1.0 + sc2) + sh2).astype(BF16)
    y = None
    for c in range(D_FF // ff_chunk):
        cols = slice(c * ff_chunk, (c + 1) * ff_chunk)
        u = jnp.maximum(_dot(h, w1_ref[0, :, cols]), 0.0)
        part = _dot((u * u).astype(BF16), w2_ref[0, cols, :])
        y = part if y is None else y + part
    x = x + g2 * y
    if final:
        ms = jnp.mean(x * x, axis=-1, keepdims=True)
        x = x * lax.rsqrt(ms + EPS) * fn_ref[...]
    o_ref[...] = x


def _out_mlp(x2d, att, gla, lru, mod, norm2, final_norm, wo, w1, w2, layer,
             rows_per_mod, mod_row0, tt, final):
    n, d = x2d.shape
    nt = n // tt
    mod_map = lambda i: (layer, mod_row0 + (i * tt) // rows_per_mod, 0, 0)
    row = lambda wd: pl.BlockSpec((tt, wd), lambda i: (i, 0))
    resident = lambda shape: pl.BlockSpec((1,) + shape, lambda i: (layer, 0, 0),
                                          pipeline_mode=pl.Buffered(1))
    return pl.pallas_call(
        functools.partial(_mlp_kernel, final=final, ff_chunk=1024),
        out_shape=jax.ShapeDtypeStruct((n, d), F32),
        grid=(nt,),
        in_specs=[row(d), row(ATTN_WIDTH), row(GLA_WIDTH), row(LRU_WIDTH),
                  pl.BlockSpec((1, 1, 1, 6 * d), mod_map),
                  _const_spec((1, d)), _const_spec((1, d)),
                  resident((d, d)), resident((d, D_FF)), resident((D_FF, d))],
        out_specs=row(d),
        compiler_params=_params(1),
        name="out_mlp",
    )(x2d, att, gla, lru, mod, norm2.reshape(1, d), final_norm.reshape(1, d), wo, w1, w2)


def _prep_w_in(w_in):
    wz = w_in[..., C_A_END:C_B].astype(BF16)
    pad = jnp.zeros(w_in.shape[:-1] + (GZ_PAD - (C_B - C_A_END),), BF16)
    return (w_in[..., :C_A_END].astype(BF16), jnp.concatenate([wz, pad], axis=-1),
            w_in[..., C_B:].astype(BF16))


def _prep_gate_w(gw):
    out = []
    nz = 2 * GLA_GATE_RANK
    for dr in (0, 1):
        wf = jnp.zeros((nz, GLA_WIDTH), F32)
        wf = wf.at[dr * GLA_GATE_RANK:(dr + 1) * GLA_GATE_RANK].set(gw[dr])
        hi, lo = _split_bf16(wf)
        out.append(jnp.concatenate([hi, hi, lo, jnp.zeros((GZ_PAD - 3 * nz, GLA_WIDTH), BF16)],
                                   axis=0))
    return jnp.stack(out)


def _block_diag(w4):
    n, a, b = w4.shape
    eye = jnp.eye(n, dtype=w4.dtype)
    return jnp.einsum("nab,nm->namb", w4, eye).reshape(n * a, n * b)


def _rope_tables(seq):
    rows = seq // GRID_W
    nf = HEAD_DIM // 4
    inv = ROPE_BASE ** (-jnp.arange(nf, dtype=F32) / nf)
    ang_r = jnp.arange(rows, dtype=F32)[:, None] * inv
    ang_c = jnp.arange(GRID_W, dtype=F32)[:, None] * inv
    by_row = lambda t: jnp.repeat(t, GRID_W, axis=0)
    by_col = lambda t: jnp.tile(t, (rows, 1))
    cos_r, sin_r = by_row(jnp.cos(ang_r)), by_row(jnp.sin(ang_r))
    cos_c, sin_c = by_col(jnp.cos(ang_c)), by_col(jnp.sin(ang_c))
    cos = jnp.concatenate([cos_r, cos_r, cos_c, cos_c], axis=1)
    sin = jnp.concatenate([-sin_r, sin_r, -sin_c, sin_c], axis=1)
    return jnp.concatenate([cos] * 2, axis=1), jnp.concatenate([sin] * 2, axis=1)


def _half_variants_of_cache(cache):
    a = cache.astype(BF16)
    z = jnp.zeros_like(a[..., 0, :])
    return jnp.concatenate([a[..., 0, :], z, z, a[..., 0, :], a[..., 1, :], z, z, a[..., 1, :]],
                           axis=-1)


def _gla_state_to_internal(s):
    b = s.shape[0]
    return jnp.transpose(s, (0, 1, 4, 2, 3)).reshape(b, 2, GLA_DK, GLA_WIDTH)


def kernel(x_prompt, x_sample, c, cache_k, cache_v, state_gla, state_lru, c_ctx, w_mod, b_mod, norm1, norm2, w_in, attn_sink, gla_gate_w, gla_gate_b, gla_norm, lru_conv_w, lru_conv_b, lru_wa, lru_ba, lru_wx, lru_bx, lru_lambda, w_out, w_mlp1, w_mlp2, final_norm):
    depth = w_in.shape[0]
    bc, sc_len, d = x_prompt.shape
    bl, sl_len, _ = x_sample.shape
    past = cache_k.shape[2]

    cond = jnp.concatenate([c_ctx[None], c, jnp.zeros((MOD_ROWS - 1 - bl, d), F32)], axis=0)
    mod = _ada_mod(cond, w_mod, b_mod)

    rope_tabs = _rope_tables(sl_len)
    ck = _half_variants_of_cache(cache_k)
    cv = _half_variants_of_cache(cache_v)
    gla_s0_ctx = jnp.zeros((1, 2, GLA_DK, GLA_WIDTH), F32)
    lru_s0_ctx = jnp.zeros((1, 2, 1, LRU_WIDTH), F32)

    xp = x_prompt.reshape(bc * sc_len, d)
    xs = x_sample.reshape(bl * sl_len, d)
    w_in_b = _prep_w_in(w_in)
    wo = w_out.astype(BF16)
    w1 = w_mlp1.astype(BF16)
    w2 = w_mlp2.astype(BF16)
    ks, vs, sgs, sls = [], [], [], []
    for l in range(depth):
        gate_w = _prep_gate_w(gla_gate_w[l])
        gate_b = gla_gate_b[l].reshape(2, 1, GLA_WIDTH)
        lru_wg = (0.5 * jnp.stack([jnp.concatenate(
            [_block_diag(lru_wa[l, dr]), _block_diag(lru_wx[l, dr])], axis=1)
            for dr in (0, 1)])).astype(BF16)
        lru_bg = (0.5 * jnp.stack([jnp.concatenate([lru_ba[l, dr], lru_bx[l, dr]])
                                   for dr in (0, 1)])).reshape(2, 1, 2 * LRU_WIDTH)
        lam = lru_lambda[l].reshape(2, 1, LRU_WIDTH)
        final = l == depth - 1

        for is_lat in (False, True):
            if is_lat:
                x2d, batch, seq, row0, rpm = xs, bl, sl_len, 1, sl_len
                tabs, tt_proj, tq, tt_gla, tt_lru = rope_tabs, 512, 256, 1024, 1024
                gla_s0 = _gla_state_to_internal(state_gla[:, l])
                lru_s0 = state_lru[:, l].reshape(bl, 2, 1, LRU_WIDTH)
            else:
                x2d, batch, seq, row0, rpm = xp, bc, sc_len, 0, bc * sc_len
                tabs, tt_proj, tq, tt_gla, tt_lru = None, 512, sc_len, 4 * sc_len, sc_len
                gla_s0, lru_s0 = gla_s0_ctx, lru_s0_ctx
            proj = _in_proj(x2d, mod, norm1[l], w_in_b, l, tabs, rpm, row0, tt_proj)
            q, kvar, vvar = proj[:3]
            gin, lx, lg = proj[-3:]
            att = _attention(q, kvar, vvar, attn_sink[l], batch, seq, tq,
                             ctx_kv=(ck, cv, l) if is_lat else None)
            gla, sg = _gla(gin, gate_w, gate_b, gla_norm[l], gla_s0, batch, seq, tt_gla)
            lru, sl = _lru(lx, lg, lru_conv_w[l], lru_conv_b[l], lru_wg, lru_bg, lam, lru_s0,
                           batch, seq, tt_lru)
            x2d = _out_mlp(x2d, att, gla, lru, mod, norm2[l], final_norm, wo, w1, w2, l,
                           rpm, row0, 512, final)
            if is_lat:
                xs = x2d
            else:
                xp = x2d
                ks.append(proj[3].reshape(bc, sc_len, KV_WIDTH))
                vs.append(proj[4].reshape(bc, sc_len, KV_WIDTH))
                sgs.append(sg)
                sls.append(sl[:, :, 0, :])

    y_prompt = xp.reshape(bc, sc_len, d)
    y_sample = xs.reshape(bl, sl_len, d)
    kv_shape = (bc, depth, sc_len, KV_WIDTH // HEAD_DIM, HEAD_DIM)
    new_k = jnp.stack(ks, axis=1).reshape(kv_shape)
    new_v = jnp.stack(vs, axis=1).reshape(kv_shape)
    sg_all = jnp.stack(sgs, axis=1)
    new_sg = jnp.transpose(
        sg_all.reshape(bc, depth, 2, GLA_DK, GLA_HEADS, GLA_DK), (0, 1, 2, 4, 5, 3))
    new_sl = jnp.stack(sls, axis=1)
    return (y_prompt, y_sample, new_k, new_v, new_sg, new_sl)
```

```python
import functools

import jax
import jax.numpy as jnp
from jax import lax
from jax.experimental import pallas as pl
from jax.experimental.pallas import tpu as pltpu

F32 = jnp.float32
BF16 = jnp.bfloat16

D_MODEL = 1024
GRID_W = 64
EPS = 1e-6
HEAD_DIM = 64
ATTN_WIDTH = 512
ATTN_HEADS = 8
KV_WIDTH = 128
WINDOW = 128
ATTN_LOOKAHEAD = 1
ATTN_SUB = 256
ROPE_BASE = 10000.0
NEG_INF = -1e30
GLA_WIDTH = 256
GLA_DK = 64
GLA_HEADS = 4
GLA_GATE_RANK = 16
GLA_GATE_NORM = 16.0
GLA_CHUNK = 64
GLA_SKEW = 2
LRU_WIDTH = 256
LRU_C = 8.0
LRU_SEG_PAD = 8
D_FF = 4096
MOD_ROWS = 8
GZ_PAD = 128
C_Q, C_K, C_V, C_GLA, C_A_END, C_B = 0, 512, 640, 768, 1536, 1568
GLA_IN = 4 * GLA_WIDTH + GZ_PAD
VMEM_LIMIT = 56 * 1024 * 1024

NT_DIMS = (((1,), (1,)), ((), ()))


def _split_bf16(a):
    hi = a.astype(BF16)
    lo = (a - hi.astype(F32)).astype(BF16)
    return hi, lo


def _dot(a, b):
    return jnp.dot(a, b, preferred_element_type=F32)


def _dot_nt(a, b):
    return lax.dot_general(a, b, NT_DIMS, preferred_element_type=F32)


def _dot_x3(a, b):
    ah, al = _split_bf16(a)
    bh, bl = _split_bf16(b)
    return _dot(ah, bh) + (_dot(ah, bl) + _dot(al, bh))


def _softplus(y):
    return jnp.maximum(y, 0.0) + jnp.log1p(jnp.exp(-jnp.abs(y)))


def _log_sigmoid(y):
    return jnp.minimum(y, 0.0) - jnp.log(1.0 + jnp.exp(-jnp.abs(y)))


def _sigmoid(y):
    return 0.5 * jnp.tanh(0.5 * y) + 0.5


def _silu(y):
    return y * _sigmoid(y)


def _gelu_tanh(y):
    c = 0.7978845608028654
    hy = 0.5 * y
    return hy * jnp.tanh(y * (c + (c * 0.044715) * (y * y))) + hy


def _params(n_axes):
    return pltpu.CompilerParams(
        dimension_semantics=("arbitrary",) * n_axes, vmem_limit_bytes=VMEM_LIMIT)


def _const_spec(shape):
    nd = len(shape)
    return pl.BlockSpec(shape, lambda *_: (0,) * nd)


def _ada_kernel(c_ref, w_ref, b_ref, o_ref):
    s = _silu(c_ref[...])
    res = _dot_x3(s, w_ref[0]) + b_ref[0]
    for r in range(MOD_ROWS):
        o_ref[0, r] = res[r:r + 1, :]


def _ada_mod(cond, w_mod, b_mod):
    depth, d, n = w_mod.shape
    tn = 1536
    return pl.pallas_call(
        _ada_kernel,
        out_shape=jax.ShapeDtypeStruct((depth, MOD_ROWS, 1, n), F32),
        grid=(depth, n // tn),
        in_specs=[
            pl.BlockSpec((MOD_ROWS, d), lambda l, j: (0, 0)),
            pl.BlockSpec((1, d, tn), lambda l, j: (l, 0, j)),
            pl.BlockSpec((1, 1, tn), lambda l, j: (l, 0, j)),
        ],
        out_specs=pl.BlockSpec((1, MOD_ROWS, 1, tn), lambda l, j: (l, 0, 0, j)),
        compiler_params=_params(2),
        name="ada_mod",
    )(cond, w_mod, b_mod.reshape(depth, 1, n))


def _rope(x, cos, sin_signed):
    w = x.shape[1]
    lane = lax.broadcasted_iota(jnp.int32, (1, w), 1)
    first = (lane % 32) < 16
    swapped = jnp.where(first, pltpu.roll(x, w - 16, 1), pltpu.roll(x, 16, 1))
    return x * cos + swapped * sin_signed


def _half_variants(a):
    lane = lax.broadcasted_iota(jnp.int32, (1, 128), 1)
    low = lane < HEAD_DIM
    ar = pltpu.roll(a, HEAD_DIM, 1)
    z = jnp.zeros_like(a)
    parts = [jnp.where(low, a, z), jnp.where(low, z, ar),
             jnp.where(low, ar, z), jnp.where(low, z, a)]
    return jnp.concatenate([t.astype(BF16) for t in parts], axis=1)


def _inproj_kernel(*refs, rope):
    if rope:
        (x_ref, mod_ref, n1_ref, wa_ref, wz_ref, wb_ref, cos_ref, sin_ref,
         q_ref, kvar_ref, vvar_ref, gla_ref, lx_ref, lg_ref) = refs
    else:
        (x_ref, mod_ref, n1_ref, wa_ref, wz_ref, wb_ref,
         q_ref, kvar_ref, vvar_ref, k_ref, v_ref, gla_ref, lx_ref, lg_ref) = refs
    d = D_MODEL
    x = x_ref[...]
    ms = jnp.mean(x * x, axis=-1, keepdims=True)
    xn = x * lax.rsqrt(ms + EPS) * n1_ref[...]
    sh = mod_ref[0, 0, :, 0:d]
    sc = mod_ref[0, 0, :, d:2 * d]
    h = (xn * (1.0 + sc) + sh).astype(BF16)
    pa = _dot(h, wa_ref[0])
    pb = _dot(h, wb_ref[0])
    q = pa[:, C_Q:C_K]
    k = pa[:, C_K:C_V]
    v = pa[:, C_V:C_GLA]
    if rope:
        cos = cos_ref[...]
        sin = sin_ref[...]
        k = _rope(k, cos, sin)
        q = _rope(q, jnp.concatenate([cos] * 4, axis=1), jnp.concatenate([sin] * 4, axis=1))
    q_ref[...] = (q * (HEAD_DIM ** -0.5)).astype(BF16)
    kvar_ref[...] = _half_variants(k)
    vvar_ref[...] = _half_variants(v)
    if not rope:
        k_ref[...] = k
        v_ref[...] = v
    w = GLA_WIDTH
    gla_ref[:, 0:3 * w] = pa[:, C_GLA:C_A_END]
    gla_ref[:, 3 * w:4 * w] = pb[:, 0:w]
    gla_ref[:, 4 * w:4 * w + GZ_PAD] = _dot(h, wz_ref[0])
    lx_ref[...] = pb[:, w:w + LRU_WIDTH]
    lg_ref[...] = pb[:, w + LRU_WIDTH:w + 2 * LRU_WIDTH]


def _in_proj(x2d, mod, norm1, w_abz, layer, rope_tabs, rows_per_mod, mod_row0, tt):
    n, d = x2d.shape
    nt = n // tt
    rope = rope_tabs is not None
    mod_map = lambda i: (layer, mod_row0 + (i * tt) // rows_per_mod, 0, 0)
    by_layer = lambda arr: pl.BlockSpec((1,) + arr.shape[1:], lambda i: (layer, 0, 0))
    in_specs = [
        pl.BlockSpec((tt, d), lambda i: (i, 0)),
        pl.BlockSpec((1, 1, 1, 6 * d), mod_map),
        _const_spec((1, d)),
    ] + [by_layer(arr) for arr in w_abz]
    args = [x2d, mod, norm1.reshape(1, d)] + list(w_abz)
    if rope:
        t = rope_tabs[0].shape[0]
        tpb = t // tt
        in_specs += [pl.BlockSpec((tt, 128), lambda i: (i % tpb, 0))] * 2
        args += list(rope_tabs)
    outs = [(ATTN_WIDTH, BF16), (4 * KV_WIDTH, BF16), (4 * KV_WIDTH, BF16)]
    if not rope:
        outs += [(KV_WIDTH, F32), (KV_WIDTH, F32)]
    outs += [(GLA_IN, F32), (LRU_WIDTH, F32), (LRU_WIDTH, F32)]
    widths = [o[0] for o in outs]
    dtypes = [o[1] for o in outs]
    return pl.pallas_call(
        functools.partial(_inproj_kernel, rope=rope),
        out_shape=[jax.ShapeDtypeStruct((n, wd), dt) for wd, dt in zip(widths, dtypes)],
        grid=(nt,),
        in_specs=in_specs,
        out_specs=[pl.BlockSpec((tt, wd), lambda i: (i, 0)) for wd in widths],
        compiler_params=_params(1),
        name="in_proj_lat" if rope else "in_proj_ctx",
    )(*args)


def _attn_kernel(*refs, latent, tq, sub, seq):
    n_units = tq // sub
    if latent:
        (sink_ref, q_ref, band_ref, kp_ref, kc_ref, kn_ref, vp_ref, vc_ref, vn_ref,
         ck_ref, cv_ref, o_ref) = refs
        i = pl.program_id(1)
        k_loc = jnp.concatenate([kp_ref[...], kc_ref[...], kn_ref[...]], axis=0)
        v_loc = jnp.concatenate([vp_ref[...], vc_ref[...], vn_ref[...]], axis=0)
        nk = sub + 2 * WINDOW
        k_ctx = ck_ref[0, 0]
        v_ctx = cv_ref[0, 0]
        key_rows = [slice(u * sub, u * sub + nk) for u in range(n_units)]
        biases = []
        for u in range(n_units):
            kpos = i * tq + u * sub - WINDOW + lax.broadcasted_iota(jnp.int32, (1, nk), 1)
            biases.append(band_ref[...] + jnp.where((kpos >= 0) & (kpos < seq), 0.0, NEG_INF))
    else:
        sink_ref, q_ref, kc_ref, vc_ref, o_ref = refs
        k_loc = kc_ref[...]
        v_loc = vc_ref[...]
        key_rows = [slice(u * sub, (u + 1) * sub) for u in range(n_units)]
        biases = [None] * n_units
    block = lambda t, kv, half: t[:, (2 * kv + half) * 128:(2 * kv + half + 1) * 128]
    with_ones = lambda t: jnp.concatenate([t, jnp.ones_like(t)], axis=1)

    def scores(item):
        u, head = divmod(item, ATTN_HEADS)
        c, half = divmod(head, 2)
        qc = q_ref[u * sub:(u + 1) * sub, c * 128:(c + 1) * 128]
        s = _dot_nt(qc, block(k_loc[key_rows[u]], c // 2, half))
        if biases[u] is not None:
            s = s + biases[u]
        return s, (_dot_nt(qc, block(k_ctx, c // 2, half)) if latent else None)

    n_items = n_units * ATTN_HEADS
    ready = [scores(it) for it in range(ATTN_LOOKAHEAD)]
    acc = None
    for item in range(n_items):
        u, head = divmod(item, ATTN_HEADS)
        c, half = divmod(head, 2)
        s, s_c = ready.pop(0)
        if item + ATTN_LOOKAHEAD < n_items:
            ready.append(scores(item + ATTN_LOOKAHEAD))
        sink = sink_ref[head]
        m = jnp.maximum(jnp.max(s, axis=-1, keepdims=True), sink)
        if latent:
            m = jnp.maximum(m, jnp.max(s_c, axis=-1, keepdims=True))
        o = _dot(jnp.exp(s - m).astype(BF16), with_ones(block(v_loc[key_rows[u]], c // 2, half)))
        if latent:
            o = o + _dot(jnp.exp(s_c - m).astype(BF16), with_ones(block(v_ctx, c // 2, half)))
        denom = jnp.exp(sink - m) + o[:, 128:256]
        o = o[:, 0:128] * (1.0 / denom)
        if half == 0:
            acc = o
        else:
            o_ref[u * sub:(u + 1) * sub, c * 128:(c + 1) * 128] = (acc + o).astype(BF16)


def _attention(q, k, v, sink, batch, seq, tq, sub, ctx_kv=None):
    latent = ctx_kv is not None
    nq = max(seq // tq, 1)
    group = max(tq // seq, 1)
    kern = functools.partial(_attn_kernel, latent=latent, tq=tq, sub=sub, seq=seq)
    smem = pl.BlockSpec(memory_space=pltpu.SMEM)
    kvw = 4 * KV_WIDTH
    if latent:
        ck, cv, layer = ctx_kv
        past = ck.shape[2]
        wb = tq // WINDOW
        nwb = seq // WINDOW
        nk = sub + 2 * WINDOW
        r = lax.broadcasted_iota(jnp.int32, (sub, nk), 0)
        j = lax.broadcasted_iota(jnp.int32, (sub, nk), 1)
        band = jnp.where((j >= r) & (j <= r + 2 * WINDOW), 0.0, NEG_INF).astype(F32)
        cur = pl.BlockSpec((tq, kvw), lambda b, i: (b * nq + i, 0))
        prev = pl.BlockSpec((WINDOW, kvw),
                            lambda b, i: (b * nwb + jnp.maximum(i * wb - 1, 0), 0))
        nxt = pl.BlockSpec((WINDOW, kvw),
                           lambda b, i: (b * nwb + jnp.minimum((i + 1) * wb, nwb - 1), 0))
        cspec = pl.BlockSpec((1, 1, past, kvw), lambda b, i: (b, layer, 0, 0))
        in_specs = [smem, pl.BlockSpec((tq, ATTN_WIDTH), lambda b, i: (b * nq + i, 0)),
                    _const_spec((sub, nk)), prev, cur, nxt, prev, cur, nxt, cspec, cspec]
        args = (sink, q, band, k, k, k, v, v, v, ck, cv)
    else:
        cur = pl.BlockSpec((tq, kvw), lambda b, i: (b * nq + i, 0))
        in_specs = [smem, pl.BlockSpec((tq, ATTN_WIDTH), lambda b, i: (b * nq + i, 0)), cur, cur]
        args = (sink, q, k, v)
    return pl.pallas_call(
        kern,
        out_shape=jax.ShapeDtypeStruct((batch * seq, ATTN_WIDTH), BF16),
        grid=(batch // group, nq),
        in_specs=in_specs,
        out_specs=pl.BlockSpec((tq, ATTN_WIDTH), lambda b, i: (b * nq + i, 0)),
        compiler_params=_params(2),
        name="attn_lat" if latent else "attn_ctx",
    )(*args)


def _gla_sweep(fwd, j, nt, tt, seq_rows, gin_ref, wg_ref, bg_ref, ng_ref, s0_ref, o_ref, sout_ref,
               st_scr, g_scr, ob_scr, qin_scr, kin_scr, kend_scr, a_scr, stc_scr, dec_scr):
    w = GLA_WIDTH
    ck = GLA_CHUNK
    nc = tt // ck
    cps = seq_rows // ck
    whole_seqs = nt == 1
    tile = j if fwd else nt - 1 - j
    starts = [(ci if fwd else nc - 1 - ci) * ck for ci in range(nc)]

    rr = lax.broadcasted_iota(jnp.int32, (w, w), 0) // GLA_DK
    cc = lax.broadcasted_iota(jnp.int32, (w, w), 1) // GLA_DK
    block_diag = rr == cc
    bd_bf = block_diag.astype(F32).astype(BF16)
    tr = lax.broadcasted_iota(jnp.int32, (ck, 2 * ck), 0)
    tc = lax.broadcasted_iota(jnp.int32, (ck, 2 * ck), 1) % ck
    tri2 = ((tr >= tc) if fwd else (tr <= tc)).astype(F32).astype(BF16)
    ar = lax.broadcasted_iota(jnp.int32, (ck, w), 0)
    ac = lax.broadcasted_iota(jnp.int32, (ck, w), 1) % ck
    causal = (ar >= ac) if fwd else (ar <= ac)

    if not whole_seqs:
        @pl.when(j == 0)
        def _():
            st_scr[...] = s0_ref[0, 0]

    zpad = gin_ref[:, 4 * w:4 * w + GZ_PAD]
    zh = zpad.astype(BF16).astype(F32)
    zcat = zh + pltpu.roll(zpad - zh, 32, 1) + pltpu.roll(zh, 64, 1)
    logit = _dot(zcat.astype(BF16), wg_ref[0]) + bg_ref[0]
    g_scr[...] = _log_sigmoid(logit) * (1.0 / GLA_GATE_NORM)

    tile_rows = pl.multiple_of(tile * tt, ck)

    def decayed(ci, c0):
        rows = pl.ds(c0, ck)
        g = g_scr[rows, :]
        gh, gl = _split_bf16(g)
        b = _dot(tri2, jnp.concatenate([gh, gl], axis=0))
        btot = jnp.sum(g, axis=0, keepdims=True)
        k = gin_ref[rows, w:2 * w]
        qin_scr[rows, :] = (gin_ref[rows, 0:w] * (GLA_DK ** -0.5) * jnp.exp(b)).astype(BF16)
        kin_scr[rows, :] = (k * jnp.exp(-b)).astype(BF16)
        kend_scr[rows, :] = (k * jnp.exp(btot - b)).astype(BF16)
        dec_scr[ci] = jnp.broadcast_to(jnp.exp(btot), (8, w))

    def scores(ci, c0):
        rows = pl.ds(c0, ck)
        k_bd = jnp.concatenate([kin_scr[rows, :]] * GLA_HEADS, axis=0) * bd_bf
        a_scr[rows, :] = jnp.where(causal, _dot_nt(qin_scr[rows, :], k_bd), 0.0).astype(BF16)

    def state(ci, c0):
        rows = pl.ds(c0, ck)
        seq_i = c0 // seq_rows
        first = whole_seqs and ci % cps == 0
        st = s0_ref[min(seq_i, s0_ref.shape[0] - 1), 0] if first else st_scr[...]
        stc_scr[ci] = jnp.concatenate([st.astype(BF16)] * GLA_HEADS, axis=0) * bd_bf
        vt = gin_ref[rows, 2 * w:3 * w].T
        vt_heads = jnp.concatenate([vt[h * GLA_DK:(h + 1) * GLA_DK, :] for h in range(GLA_HEADS)],
                                   axis=1).astype(BF16)
        kend_bd = jnp.concatenate([kend_scr[rows, :]] * GLA_HEADS, axis=0) * bd_bf
        st = st * dec_scr[ci][0:1, :] + _dot(vt_heads, kend_bd)
        st_scr[...] = st
        if whole_seqs and ci % cps == cps - 1:
            sout_ref[seq_i, 0] = st

    def output(ci, c0):
        rows = pl.ds(c0, ck)
        v_bd = jnp.concatenate([gin_ref[rows, 2 * w:3 * w].astype(BF16)] * GLA_HEADS,
                               axis=0) * bd_bf
        o = _dot(a_scr[rows, :], v_bd) + _dot_nt(qin_scr[rows, :], stc_scr[ci])
        if fwd:
            o_ref[rows, :] = o
        else:
            ob_scr[pl.ds(tile_rows + c0, ck), :] = o

    stages = (decayed, scores, state, output)
    for step in range(nc + GLA_SKEW * (len(stages) - 1)):
        for si, stage in enumerate(stages):
            ci = step - GLA_SKEW * si
            if 0 <= ci < nc:
                stage(ci, starts[ci])

    if fwd:
        o = o_ref[...] + ob_scr[pl.ds(tile_rows, tt), :]
        ms = _dot((o * o).astype(BF16), bd_bf) * (1.0 / GLA_DK)
        r = gin_ref[:, 3 * w:4 * w]
        o_ref[...] = o * lax.rsqrt(ms + EPS) * ng_ref[...] * _silu(r)

    if not whole_seqs:
        @pl.when(j == nt - 1)
        def _():
            sout_ref[0, 0] = st_scr[...]


def _gla_kernel(gin_ref, wg_ref, bg_ref, ng_ref, s0_ref, o_ref, sout_ref,
                st_scr, g_scr, ob_scr, qin_scr, kin_scr, kend_scr, a_scr, stc_scr, dec_scr,
                *, nt, tt, seq_rows):
    sweep = pl.program_id(1)
    j = pl.program_id(2)
    args = (j, nt, tt, seq_rows, gin_ref, wg_ref, bg_ref, ng_ref, s0_ref, o_ref, sout_ref,
            st_scr, g_scr, ob_scr, qin_scr, kin_scr, kend_scr, a_scr, stc_scr, dec_scr)

    @pl.when(sweep == 0)
    def _():
        _gla_sweep(False, *args)

    @pl.when(sweep == 1)
    def _():
        _gla_sweep(True, *args)


def _gla(gin, wg, bg, norm_g, s0, batch, seq, tt):
    nt = max(seq // tt, 1)
    group = max(tt // seq, 1)
    w = GLA_WIDTH
    per_batch_state = s0.shape[0] == batch
    s0_group = group if per_batch_state else 1
    tile_of = lambda s, j: j * (2 * s - 1) + (1 - s) * (nt - 1)
    return pl.pallas_call(
        functools.partial(_gla_kernel, nt=nt, tt=tt, seq_rows=min(seq, tt)),
        out_shape=[jax.ShapeDtypeStruct((batch * seq, w), F32),
                   jax.ShapeDtypeStruct((batch, 2, GLA_DK, w), F32)],
        grid=(batch // group, 2, nt),
        in_specs=[
            pl.BlockSpec((tt, GLA_IN), lambda b, s, j: (b * nt + tile_of(s, j), 0)),
            pl.BlockSpec((1, GZ_PAD, w), lambda b, s, j: (1 - s, 0, 0)),
            pl.BlockSpec((1, 1, w), lambda b, s, j: (1 - s, 0, 0)),
            _const_spec((1, w)),
            pl.BlockSpec((s0_group, 1, GLA_DK, w),
                         lambda b, s, j: (b if per_batch_state else 0, 1 - s, 0, 0)),
        ],
        out_specs=[
            pl.BlockSpec((tt, w), lambda b, s, j: (b * nt + j * s, 0)),
            pl.BlockSpec((group, 1, GLA_DK, w), lambda b, s, j: (b, 1 - s, 0, 0)),
        ],
        scratch_shapes=[pltpu.VMEM((GLA_DK, w), F32), pltpu.VMEM((tt, w), F32),
                        pltpu.VMEM((nt * tt, w), F32)]
        + [pltpu.VMEM((tt, w), BF16)] * 4
        + [pltpu.VMEM((tt // GLA_CHUNK, w, w), BF16),
           pltpu.VMEM((tt // GLA_CHUNK, 8, w), F32)],
        compiler_params=_params(3),
        name="gla",
    )(gin, wg, bg, norm_g.reshape(1, w), s0)


def _lru_sweep(fwd, j, nt, tt, seq_rows, xp_ref, xc_ref, xn_ref, lg_ref, cw_ref, cb_ref,
               wg_ref, bg_ref, lam_ref, h0_ref, o_ref, sout_ref,
               carry_scr, a_scr, ab_scr, hl_scr, pl_scr, h_scr, hb_scr, xcv_scr):
    w = LRU_WIDTH
    seg_len = tt // 8
    tile = j if fwd else nt - 1 - j
    whole_seqs = nt == 1
    segs_per_seq = seq_rows // seg_len if whole_seqs else 8

    if not whole_seqs:
        @pl.when(j == 0)
        def _():
            carry_scr[...] = jnp.broadcast_to(h0_ref[0, 0], (8, w))

    all_rows = pl.ds(pl.multiple_of(tile * tt, 8), tt)
    if fwd:
        xc = xcv_scr[all_rows, :]
    else:
        pre = jnp.where(tile > 0, xp_ref[...], 0.0)
        post = jnp.where(tile < nt - 1, xn_ref[...], 0.0)
        xe = jnp.concatenate([pre, xc_ref[...], post], axis=0)
        ne = tt + 16
        taps = [pltpu.roll(xe, 2, 0)[8:8 + tt], pltpu.roll(xe, 1, 0)[8:8 + tt], xe[8:8 + tt],
                pltpu.roll(xe, ne - 1, 0)[8:8 + tt]]
        if whole_seqs and seq_rows < tt:
            pos = lax.broadcasted_iota(jnp.int32, (tt, 1), 0) % seq_rows
            taps[0] = jnp.where(pos >= 2, taps[0], 0.0)
            taps[1] = jnp.where(pos >= 1, taps[1], 0.0)
            taps[3] = jnp.where(pos < seq_rows - 1, taps[3], 0.0)
        xc = cb_ref[...] + cw_ref[2:3, :] * taps[2]
        xc = xc + cw_ref[0:1, :] * taps[0]
        xc = xc + cw_ref[1:2, :] * taps[1]
        xc = xc + cw_ref[3:4, :] * taps[3]
        xcv_scr[all_rows, :] = xc

    ri = _dot(xc.astype(BF16), wg_ref[0]) + bg_ref[0]
    t_a = jnp.tanh(ri[:, 0:w])
    t_x = jnp.tanh(ri[:, w:2 * w])
    c2 = (-0.5 * LRU_C) * _softplus(-lam_ref[0])
    a = jnp.exp(t_a * c2 + c2)
    hx = 0.5 * xc
    bt = jnp.sqrt(1.0 - a * a) * (t_x * hx + hx)

    nl = w // 128
    pitch = seg_len + LRU_SEG_PAD
    for lt in range(nl):
        for k in range(8):
            dst = slice(k * pitch, k * pitch + seg_len)
            src = slice(k * seg_len, (k + 1) * seg_len)
            a_scr[lt, dst, :] = a[src, lt * 128:(lt + 1) * 128]
            ab_scr[lt, dst, :] = bt[src, lt * 128:(lt + 1) * 128]

    def step(ii, hp):
        i = ii if fwd else seg_len - 1 - ii
        seg_rows = pl.ds(i, 8, stride=pitch)
        rows = pl.ds(pl.multiple_of(i * 8, 8), 8)
        out = []
        for lt in range(nl):
            h, p = hp[lt]
            a_i = a_scr[lt, seg_rows, :]
            h = a_i * h + ab_scr[lt, seg_rows, :]
            p = a_i * p
            hl_scr[lt, rows, :] = h
            pl_scr[lt, rows, :] = p
            out.append((h, p))
        return tuple(out)

    init = tuple((jnp.zeros((8, 128), F32), jnp.ones((8, 128), F32)) for _ in range(nl))
    ends = lax.fori_loop(0, seg_len, step, init, unroll=8)
    h_end = jnp.concatenate([e[0] for e in ends], axis=1)
    p_end = jnp.concatenate([e[1] for e in ends], axis=1)

    c = None if whole_seqs else carry_scr[0:1, :]
    c_in = [None] * 8
    for n, k in enumerate(range(8) if fwd else range(7, -1, -1)):
        seq_i = k // segs_per_seq
        if whole_seqs and n % segs_per_seq == 0:
            c = h0_ref[min(seq_i, h0_ref.shape[0] - 1), 0]
        c_in[k] = c
        c = h_end[k:k + 1, :] + p_end[k:k + 1, :] * c
        if whole_seqs and n % segs_per_seq == segs_per_seq - 1:
            sout_ref[seq_i, 0] = jnp.broadcast_to(c, (8, w))
    if not whole_seqs:
        carry_scr[...] = jnp.broadcast_to(c, (8, w))

    for k in range(8):
        for i0 in range(0, seg_len, 8):
            src = pl.ds(i0 * 8 + k, 8, stride=8)
            r0 = k * seg_len + i0
            for lt in range(nl):
                c_k = c_in[k][:, lt * 128:(lt + 1) * 128]
                h_scr[lt, r0:r0 + 8, :] = hl_scr[lt, src, :] + pl_scr[lt, src, :] * c_k
    h_all = jnp.concatenate([h_scr[lt] for lt in range(nl)], axis=1)
    if fwd:
        o_ref[...] = (h_all + hb_scr[all_rows, :]) * _gelu_tanh(lg_ref[...])
    else:
        hb_scr[all_rows, :] = h_all

    if not whole_seqs:
        @pl.when(j == nt - 1)
        def _():
            sout_ref[0, 0] = carry_scr[...]


def _lru_kernel(xp_ref, xc_ref, xn_ref, lg_ref, cw_ref, cb_ref, wg_ref, bg_ref,
                lam_ref, h0_ref, o_ref, sout_ref, carry_scr, a_scr, ab_scr, hl_scr, pl_scr,
                h_scr, hb_scr, xcv_scr, *, nt, tt, seq_rows):
    sweep = pl.program_id(1)
    j = pl.program_id(2)
    args = (j, nt, tt, seq_rows, xp_ref, xc_ref, xn_ref, lg_ref, cw_ref, cb_ref, wg_ref, bg_ref,
            lam_ref, h0_ref, o_ref, sout_ref, carry_scr, a_scr, ab_scr, hl_scr, pl_scr,
            h_scr, hb_scr, xcv_scr)

    @pl.when(sweep == 0)
    def _():
        _lru_sweep(False, *args)

    @pl.when(sweep == 1)
    def _():
        _lru_sweep(True, *args)


def _lru(lx, lg, conv_w, conv_b, wg, bg, lam, h0, batch, seq, tt):
    nt = max(seq // tt, 1)
    group = max(tt // seq, 1)
    w = LRU_WIDTH
    t8 = tt // 8
    n8 = nt * t8
    per_batch_state = h0.shape[0] == batch
    h0_group = group if per_batch_state else 1
    tile_of = lambda s, j: j * (2 * s - 1) + (1 - s) * (nt - 1)
    return pl.pallas_call(
        functools.partial(_lru_kernel, nt=nt, tt=tt, seq_rows=min(seq, tt)),
        out_shape=[jax.ShapeDtypeStruct((batch * seq, w), F32),
                   jax.ShapeDtypeStruct((batch, 2, 8, w), F32)],
        grid=(batch // group, 2, nt),
        in_specs=[
            pl.BlockSpec((8, w), lambda b, s, j:
                         (b * n8 + jnp.maximum(tile_of(s, j) * t8 - 1, 0), 0)),
            pl.BlockSpec((tt, w), lambda b, s, j: (b * nt + tile_of(s, j), 0)),
            pl.BlockSpec((8, w), lambda b, s, j:
                         (b * n8 + jnp.minimum((tile_of(s, j) + 1) * t8, n8 - 1), 0)),
            pl.BlockSpec((tt, w), lambda b, s, j: (b * nt + j * s, 0)),
            _const_spec((4, w)),
            _const_spec((1, w)),
            pl.BlockSpec((1, w, 2 * w), lambda b, s, j: (1 - s, 0, 0)),
            pl.BlockSpec((1, 1, 2 * w), lambda b, s, j: (1 - s, 0, 0)),
            pl.BlockSpec((1, 1, w), lambda b, s, j: (1 - s, 0, 0)),
            pl.BlockSpec((h0_group, 1, 1, w),
                         lambda b, s, j: (b if per_batch_state else 0, 1 - s, 0, 0)),
        ],
        out_specs=[
            pl.BlockSpec((tt, w), lambda b, s, j: (b * nt + j * s, 0)),
            pl.BlockSpec((group, 1, 8, w), lambda b, s, j: (b, 1 - s, 0, 0)),
        ],
        scratch_shapes=[pltpu.VMEM((8, w), F32)]
        + [pltpu.VMEM((w // 128, tt + 8 * LRU_SEG_PAD, 128), F32)] * 2
        + [pltpu.VMEM((w // 128, tt, 128), F32)] * 3
        + [pltpu.VMEM((nt * tt, w), F32)] * 2,
        compiler_params=_params(3),
        name="lru",
    )(lx, lx, lx, lg, conv_w, conv_b.reshape(1, w), wg, bg, lam, h0)


def _mlp_kernel(x_ref, att_ref, gla_ref, lru_ref, mod_ref, n2_ref, fn_ref,
                wo_ref, w1_ref, w2_ref, o_ref, *, final, ff_chunk):
    d = D_MODEL
    mix = _dot(att_ref[...], wo_ref[0, 0:ATTN_WIDTH, :])
    mix = mix + _dot(gla_ref[...].astype(BF16), wo_ref[0, ATTN_WIDTH:ATTN_WIDTH + GLA_WIDTH, :])
    mix = mix + _dot(lru_ref[...].astype(BF16), wo_ref[0, ATTN_WIDTH + GLA_WIDTH:d, :])
    g1 = mod_ref[0, 0, :, 2 * d:3 * d]
    sh2 = mod_ref[0, 0, :, 3 * d:4 * d]
    sc2 = mod_ref[0, 0, :, 4 * d:5 * d]
    g2 = mod_ref[0, 0, :, 5 * d:6 * d]
    x = x_ref[...] + g1 * mix
    ms = jnp.mean(x * x, axis=-1, keepdims=True)
    h = (x * lax.rsqrt(ms + EPS) * n2_ref[...] * (1.0 + sc2) + sh2).astype(BF16)
    y = None
    for c in range(D_FF // ff_chunk):
        cols = slice(c * ff_chunk, (c + 1) * ff_chunk)
        u = jnp.maximum(_dot(h, w1_ref[0, :, cols]), 0.0)
        part = _dot((u * u).astype(BF16), w2_ref[0, cols, :])
        y = part if y is None else y + part
    x = x + g2 * y
    if final:
        ms = jnp.mean(x * x, axis=-1, keepdims=True)
        x = x * lax.rsqrt(ms + EPS) * fn_ref[...]
    o_ref[...] = x


def _out_mlp(x2d, att, gla, lru, mod, norm2, final_norm, wo, w1, w2, layer,
             rows_per_mod, mod_row0, tt, final):
    n, d = x2d.shape
    nt = n // tt
    mod_map = lambda i: (layer, mod_row0 + (i * tt) // rows_per_mod, 0, 0)
    row = lambda wd: pl.BlockSpec((tt, wd), lambda i: (i, 0))
    resident = lambda shape: pl.BlockSpec((1,) + shape, lambda i: (layer, 0, 0),
                                          pipeline_mode=pl.Buffered(1))
    return pl.pallas_call(
        functools.partial(_mlp_kernel, final=final, ff_chunk=1024),
        out_shape=jax.ShapeDtypeStruct((n, d), F32),
        grid=(nt,),
        in_specs=[row(d), row(ATTN_WIDTH), row(GLA_WIDTH), row(LRU_WIDTH),
                  pl.BlockSpec((1, 1, 1, 6 * d), mod_map),
                  _const_spec((1, d)), _const_spec((1, d)),
                  resident((d, d)), resident((d, D_FF)), resident((D_FF, d))],
        out_specs=row(d),
        compiler_params=_params(1),
        name="out_mlp",
    )(x2d, att, gla, lru, mod, norm2.reshape(1, d), final_norm.reshape(1, d), wo, w1, w2)


def _prep_w_in(w_in):
    wz = w_in[..., C_A_END:C_B].astype(BF16)
    pad = jnp.zeros(w_in.shape[:-1] + (GZ_PAD - (C_B - C_A_END),), BF16)
    return (w_in[..., :C_A_END].astype(BF16), jnp.concatenate([wz, pad], axis=-1),
            w_in[..., C_B:].astype(BF16))


def _prep_gate_w(gw):
    out = []
    nz = 2 * GLA_GATE_RANK
    for dr in (0, 1):
        wf = jnp.zeros((nz, GLA_WIDTH), F32)
        wf = wf.at[dr * GLA_GATE_RANK:(dr + 1) * GLA_GATE_RANK].set(gw[dr])
        hi, lo = _split_bf16(wf)
        out.append(jnp.concatenate([hi, hi, lo, jnp.zeros((GZ_PAD - 3 * nz, GLA_WIDTH), BF16)],
                                   axis=0))
    return jnp.stack(out)


def _block_diag(w4):
    n, a, b = w4.shape
    eye = jnp.eye(n, dtype=w4.dtype)
    return jnp.einsum("nab,nm->namb", w4, eye).reshape(n * a, n * b)


def _rope_tables(seq):
    rows = seq // GRID_W
    nf = HEAD_DIM // 4
    inv = ROPE_BASE ** (-jnp.arange(nf, dtype=F32) / nf)
    ang_r = jnp.arange(rows, dtype=F32)[:, None] * inv
    ang_c = jnp.arange(GRID_W, dtype=F32)[:, None] * inv
    by_row = lambda t: jnp.repeat(t, GRID_W, axis=0)
    by_col = lambda t: jnp.tile(t, (rows, 1))
    cos_r, sin_r = by_row(jnp.cos(ang_r)), by_row(jnp.sin(ang_r))
    cos_c, sin_c = by_col(jnp.cos(ang_c)), by_col(jnp.sin(ang_c))
    cos = jnp.concatenate([cos_r, cos_r, cos_c, cos_c], axis=1)
    sin = jnp.concatenate([-sin_r, sin_r, -sin_c, sin_c], axis=1)
    return jnp.concatenate([cos] * 2, axis=1), jnp.concatenate([sin] * 2, axis=1)


def _half_variants_of_cache(cache):
    a = cache.astype(BF16)
    z = jnp.zeros_like(a[..., 0, :])
    return jnp.concatenate([a[..., 0, :], z, z, a[..., 0, :], a[..., 1, :], z, z, a[..., 1, :]],
                           axis=-1)


def _gla_state_to_internal(s):
    b = s.shape[0]
    return jnp.transpose(s, (0, 1, 4, 2, 3)).reshape(b, 2, GLA_DK, GLA_WIDTH)


def kernel(x_prompt, x_sample, c, cache_k, cache_v, state_gla, state_lru, c_ctx, w_mod, b_mod, norm1, norm2, w_in, attn_sink, gla_gate_w, gla_gate_b, gla_norm, lru_conv_w, lru_conv_b, lru_wa, lru_ba, lru_wx, lru_bx, lru_lambda, w_out, w_mlp1, w_mlp2, final_norm):
    depth = w_in.shape[0]
    bc, sc_len, d = x_prompt.shape
    bl, sl_len, _ = x_sample.shape
    past = cache_k.shape[2]

    cond = jnp.concatenate([c_ctx[None], c, jnp.zeros((MOD_ROWS - 1 - bl, d), F32)], axis=0)
    mod = _ada_mod(cond, w_mod, b_mod)

    rope_tabs = _rope_tables(sl_len)
    ck = _half_variants_of_cache(cache_k)
    cv = _half_variants_of_cache(cache_v)
    gla_s0_ctx = jnp.zeros((1, 2, GLA_DK, GLA_WIDTH), F32)
    lru_s0_ctx = jnp.zeros((1, 2, 1, LRU_WIDTH), F32)

    xp = x_prompt.reshape(bc * sc_len, d)
    xs = x_sample.reshape(bl * sl_len, d)
    w_in_b = _prep_w_in(w_in)
    wo = w_out.astype(BF16)
    w1 = w_mlp1.astype(BF16)
    w2 = w_mlp2.astype(BF16)
    ks, vs, sgs, sls = [], [], [], []
    for l in range(depth):
        gate_w = _prep_gate_w(gla_gate_w[l])
        gate_b = gla_gate_b[l].reshape(2, 1, GLA_WIDTH)
        lru_wg = (0.5 * jnp.stack([jnp.concatenate(
            [_block_diag(lru_wa[l, dr]), _block_diag(lru_wx[l, dr])], axis=1)
            for dr in (0, 1)])).astype(BF16)
        lru_bg = (0.5 * jnp.stack([jnp.concatenate([lru_ba[l, dr], lru_bx[l, dr]])
                                   for dr in (0, 1)])).reshape(2, 1, 2 * LRU_WIDTH)
        lam = lru_lambda[l].reshape(2, 1, LRU_WIDTH)
        final = l == depth - 1

        for is_lat in (False, True):
            if is_lat:
                x2d, batch, seq, row0, rpm = xs, bl, sl_len, 1, sl_len
                tabs, tt_proj, tq, tt_gla, tt_lru = rope_tabs, 1024, 512, 1024, 1024
                gla_s0 = _gla_state_to_internal(state_gla[:, l])
                lru_s0 = state_lru[:, l].reshape(bl, 2, 1, LRU_WIDTH)
            else:
                x2d, batch, seq, row0, rpm = xp, bc, sc_len, 0, bc * sc_len
                tabs, tt_proj, tq, tt_gla, tt_lru = None, 1024, 2 * sc_len, 4 * sc_len, 4 * sc_len
                gla_s0, lru_s0 = gla_s0_ctx, lru_s0_ctx
            proj = _in_proj(x2d, mod, norm1[l], w_in_b, l, tabs, rpm, row0, tt_proj)
            q, kvar, vvar = proj[:3]
            gin, lx, lg = proj[-3:]
            att = _attention(q, kvar, vvar, attn_sink[l], batch, seq, tq, ATTN_SUB,
                             ctx_kv=(ck, cv, l) if is_lat else None)
            gla, sg = _gla(gin, gate_w, gate_b, gla_norm[l], gla_s0, batch, seq, tt_gla)
            lru, sl = _lru(lx, lg, lru_conv_w[l], lru_conv_b[l], lru_wg, lru_bg, lam, lru_s0,
                           batch, seq, tt_lru)
            x2d = _out_mlp(x2d, att, gla, lru, mod, norm2[l], final_norm, wo, w1, w2, l,
                           rpm, row0, 512, final)
            if is_lat:
                xs = x2d
            else:
                xp = x2d
                ks.append(proj[3].reshape(bc, sc_len, KV_WIDTH))
                vs.append(proj[4].reshape(bc, sc_len, KV_WIDTH))
                sgs.append(sg)
                sls.append(sl[:, :, 0, :])

    y_prompt = xp.reshape(bc, sc_len, d)
    y_sample = xs.reshape(bl, sl_len, d)
    kv_shape = (bc, depth, sc_len, KV_WIDTH // HEAD_DIM, HEAD_DIM)
    new_k = jnp.stack(ks, axis=1).reshape(kv_shape)
    new_v = jnp.stack(vs, axis=1).reshape(kv_shape)
    sg_all = jnp.stack(sgs, axis=1)
    new_sg = jnp.transpose(
        sg_all.reshape(bc, depth, 2, GLA_DK, GLA_HEADS, GLA_DK), (0, 1, 2, 4, 5, 3))
    new_sl = jnp.stack(sls, axis=1)
    return (y_prompt, y_sample, new_k, new_v, new_sg, new_sl)
```

```python
import functools

import jax
import jax.numpy as jnp
from jax import lax
from jax.experimental import pallas as pl
from jax.experimental.pallas import tpu as pltpu

F32 = jnp.float32
BF16 = jnp.bfloat16

D_MODEL = 1024
GRID_W = 64
EPS = 1e-6
HEAD_DIM = 64
ATTN_WIDTH = 512
ATTN_HEADS = 8
KV_WIDTH = 128
WINDOW = 128
ATTN_LOOKAHEAD = 1
ATTN_SUB = 256
ROPE_BASE = 10000.0
NEG_INF = -1e30
GLA_WIDTH = 256
GLA_DK = 64
GLA_HEADS = 4
GLA_GATE_RANK = 16
GLA_GATE_NORM = 16.0
GLA_CHUNK = 64
GLA_SKEW = 2
LRU_WIDTH = 256
LRU_C = 8.0
LRU_SEG_PAD = 8
LRU_SEGMENTS = 16
D_FF = 4096
MOD_ROWS = 8
GZ_PAD = 128
C_Q, C_K, C_V, C_GLA, C_A_END, C_B = 0, 512, 640, 768, 1536, 1568
GLA_IN = 4 * GLA_WIDTH + GZ_PAD
VMEM_LIMIT = 56 * 1024 * 1024

NT_DIMS = (((1,), (1,)), ((), ()))


def _split_bf16(a):
    hi = a.astype(BF16)
    lo = (a - hi.astype(F32)).astype(BF16)
    return hi, lo


def _dot(a, b):
    return jnp.dot(a, b, preferred_element_type=F32)


def _dot_nt(a, b):
    return lax.dot_general(a, b, NT_DIMS, preferred_element_type=F32)


def _dot_x3(a, b):
    ah, al = _split_bf16(a)
    bh, bl = _split_bf16(b)
    return _dot(ah, bh) + (_dot(ah, bl) + _dot(al, bh))


def _softplus(y):
    return jnp.maximum(y, 0.0) + jnp.log1p(jnp.exp(-jnp.abs(y)))


def _log_sigmoid(y):
    return jnp.minimum(y, 0.0) - jnp.log(1.0 + jnp.exp(-jnp.abs(y)))


def _sigmoid(y):
    return 0.5 * jnp.tanh(0.5 * y) + 0.5


def _silu(y):
    return y * _sigmoid(y)


def _gelu_tanh(y):
    c = 0.7978845608028654
    hy = 0.5 * y
    return hy * jnp.tanh(y * (c + (c * 0.044715) * (y * y))) + hy


def _params(n_axes):
    return pltpu.CompilerParams(
        dimension_semantics=("arbitrary",) * n_axes, vmem_limit_bytes=VMEM_LIMIT)


def _const_spec(shape):
    nd = len(shape)
    return pl.BlockSpec(shape, lambda *_: (0,) * nd)


def _ada_kernel(c_ref, w_ref, b_ref, o_ref):
    s = _silu(c_ref[...])
    res = _dot_x3(s, w_ref[0]) + b_ref[0]
    for r in range(MOD_ROWS):
        o_ref[0, r] = res[r:r + 1, :]


def _ada_mod(cond, w_mod, b_mod):
    depth, d, n = w_mod.shape
    tn = 1536
    return pl.pallas_call(
        _ada_kernel,
        out_shape=jax.ShapeDtypeStruct((depth, MOD_ROWS, 1, n), F32),
        grid=(depth, n // tn),
        in_specs=[
            pl.BlockSpec((MOD_ROWS, d), lambda l, j: (0, 0)),
            pl.BlockSpec((1, d, tn), lambda l, j: (l, 0, j)),
            pl.BlockSpec((1, 1, tn), lambda l, j: (l, 0, j)),
        ],
        out_specs=pl.BlockSpec((1, MOD_ROWS, 1, tn), lambda l, j: (l, 0, 0, j)),
        compiler_params=_params(2),
        name="ada_mod",
    )(cond, w_mod, b_mod.reshape(depth, 1, n))


def _rope(x, cos, sin_signed):
    w = x.shape[1]
    lane = lax.broadcasted_iota(jnp.int32, (1, w), 1)
    first = (lane % 32) < 16
    swapped = jnp.where(first, pltpu.roll(x, w - 16, 1), pltpu.roll(x, 16, 1))
    return x * cos + swapped * sin_signed


def _half_variants(a):
    lane = lax.broadcasted_iota(jnp.int32, (1, 128), 1)
    low = lane < HEAD_DIM
    ar = pltpu.roll(a, HEAD_DIM, 1)
    z = jnp.zeros_like(a)
    parts = [jnp.where(low, a, z), jnp.where(low, z, ar),
             jnp.where(low, ar, z), jnp.where(low, z, a)]
    return jnp.concatenate([t.astype(BF16) for t in parts], axis=1)


def _inproj_kernel(*refs, rope):
    if rope:
        (x_ref, mod_ref, n1_ref, wa_ref, wz_ref, wb_ref, cos_ref, sin_ref,
         q_ref, kvar_ref, vvar_ref, gla_ref, lx_ref, lg_ref) = refs
    else:
        (x_ref, mod_ref, n1_ref, wa_ref, wz_ref, wb_ref,
         q_ref, kvar_ref, vvar_ref, k_ref, v_ref, gla_ref, lx_ref, lg_ref) = refs
    d = D_MODEL
    x = x_ref[...]
    ms = jnp.mean(x * x, axis=-1, keepdims=True)
    xn = x * lax.rsqrt(ms + EPS) * n1_ref[...]
    sh = mod_ref[0, 0, :, 0:d]
    sc = mod_ref[0, 0, :, d:2 * d]
    h = (xn * (1.0 + sc) + sh).astype(BF16)
    p_att = _dot(h, wa_ref[0, :, C_Q:C_GLA])
    p_gla = _dot(h, wa_ref[0, :, C_GLA:C_A_END])
    q = p_att[:, C_Q:C_K]
    k = p_att[:, C_K:C_V]
    v = p_att[:, C_V:C_GLA]
    if rope:
        cos = cos_ref[...]
        sin = sin_ref[...]
        k = _rope(k, cos, sin)
        q = _rope(q, jnp.concatenate([cos] * 4, axis=1), jnp.concatenate([sin] * 4, axis=1))
    q_ref[...] = (q * (HEAD_DIM ** -0.5)).astype(BF16)
    kvar_ref[...] = _half_variants(k)
    vvar_ref[...] = _half_variants(v)
    if not rope:
        k_ref[...] = k
        v_ref[...] = v
    w = GLA_WIDTH
    pb = _dot(h, wb_ref[0])
    gla_ref[:, 0:3 * w] = p_gla
    gla_ref[:, 3 * w:4 * w] = pb[:, 0:w]
    gla_ref[:, 4 * w:4 * w + GZ_PAD] = _dot(h, wz_ref[0])
    lx_ref[...] = pb[:, w:w + LRU_WIDTH]
    lg_ref[...] = pb[:, w + LRU_WIDTH:w + 2 * LRU_WIDTH]


def _in_proj(x2d, mod, norm1, w_abz, layer, rope_tabs, rows_per_mod, mod_row0, tt):
    n, d = x2d.shape
    nt = n // tt
    rope = rope_tabs is not None
    mod_map = lambda i: (layer, mod_row0 + (i * tt) // rows_per_mod, 0, 0)
    by_layer = lambda arr: pl.BlockSpec((1,) + arr.shape[1:], lambda i: (layer, 0, 0))
    in_specs = [
        pl.BlockSpec((tt, d), lambda i: (i, 0)),
        pl.BlockSpec((1, 1, 1, 6 * d), mod_map),
        _const_spec((1, d)),
    ] + [by_layer(arr) for arr in w_abz]
    args = [x2d, mod, norm1.reshape(1, d)] + list(w_abz)
    if rope:
        t = rope_tabs[0].shape[0]
        tpb = t // tt
        in_specs += [pl.BlockSpec((tt, 128), lambda i: (i % tpb, 0))] * 2
        args += list(rope_tabs)
    outs = [(ATTN_WIDTH, BF16), (4 * KV_WIDTH, BF16), (4 * KV_WIDTH, BF16)]
    if not rope:
        outs += [(KV_WIDTH, F32), (KV_WIDTH, F32)]
    outs += [(GLA_IN, F32), (LRU_WIDTH, F32), (LRU_WIDTH, F32)]
    widths = [o[0] for o in outs]
    dtypes = [o[1] for o in outs]
    return pl.pallas_call(
        functools.partial(_inproj_kernel, rope=rope),
        out_shape=[jax.ShapeDtypeStruct((n, wd), dt) for wd, dt in zip(widths, dtypes)],
        grid=(nt,),
        in_specs=in_specs,
        out_specs=[pl.BlockSpec((tt, wd), lambda i: (i, 0)) for wd in widths],
        compiler_params=_params(1),
        name="in_proj_lat" if rope else "in_proj_ctx",
    )(*args)


def _attn_kernel(*refs, latent, tq, sub, seq):
    n_units = tq // sub
    if latent:
        (sink_ref, q_ref, band_ref, kp_ref, kc_ref, kn_ref, vp_ref, vc_ref, vn_ref,
         ck_ref, cv_ref, o_ref) = refs
        i = pl.program_id(1)
        k_loc = jnp.concatenate([kp_ref[...], kc_ref[...], kn_ref[...]], axis=0)
        v_loc = jnp.concatenate([vp_ref[...], vc_ref[...], vn_ref[...]], axis=0)
        nk = sub + 2 * WINDOW
        k_ctx = ck_ref[0, 0]
        v_ctx = cv_ref[0, 0]
        key_rows = [slice(u * sub, u * sub + nk) for u in range(n_units)]
        biases = []
        for u in range(n_units):
            kpos = i * tq + u * sub - WINDOW + lax.broadcasted_iota(jnp.int32, (1, nk), 1)
            biases.append(band_ref[...] + jnp.where((kpos >= 0) & (kpos < seq), 0.0, NEG_INF))
    else:
        sink_ref, q_ref, kc_ref, vc_ref, o_ref = refs
        k_loc = kc_ref[...]
        v_loc = vc_ref[...]
        key_rows = [slice(u * sub, (u + 1) * sub) for u in range(n_units)]
        biases = [None] * n_units
    block = lambda t, kv, half: t[:, (2 * kv + half) * 128:(2 * kv + half + 1) * 128]
    with_ones = lambda t: jnp.concatenate([t, jnp.ones_like(t)], axis=1)

    def scores(item):
        u, head = divmod(item, ATTN_HEADS)
        c, half = divmod(head, 2)
        qc = q_ref[u * sub:(u + 1) * sub, c * 128:(c + 1) * 128]
        s = _dot_nt(qc, block(k_loc[key_rows[u]], c // 2, half))
        if biases[u] is not None:
            s = s + biases[u]
        return s, (_dot_nt(qc, block(k_ctx, c // 2, half)) if latent else None)

    n_items = n_units * ATTN_HEADS
    ready = [scores(it) for it in range(ATTN_LOOKAHEAD)]
    acc = None
    for item in range(n_items):
        u, head = divmod(item, ATTN_HEADS)
        c, half = divmod(head, 2)
        s, s_c = ready.pop(0)
        if item + ATTN_LOOKAHEAD < n_items:
            ready.append(scores(item + ATTN_LOOKAHEAD))
        sink = sink_ref[head]
        m = jnp.maximum(jnp.max(s, axis=-1, keepdims=True), sink)
        if latent:
            m = jnp.maximum(m, jnp.max(s_c, axis=-1, keepdims=True))
        o = _dot(jnp.exp(s - m).astype(BF16), with_ones(block(v_loc[key_rows[u]], c // 2, half)))
        if latent:
            o = o + _dot(jnp.exp(s_c - m).astype(BF16), with_ones(block(v_ctx, c // 2, half)))
        denom = jnp.exp(sink - m) + o[:, 128:256]
        o = o[:, 0:128] * (1.0 / denom)
        if half == 0:
            acc = o
        else:
            o_ref[u * sub:(u + 1) * sub, c * 128:(c + 1) * 128] = (acc + o).astype(BF16)


def _attention(q, k, v, sink, batch, seq, tq, sub, ctx_kv=None):
    latent = ctx_kv is not None
    nq = max(seq // tq, 1)
    group = max(tq // seq, 1)
    kern = functools.partial(_attn_kernel, latent=latent, tq=tq, sub=sub, seq=seq)
    smem = pl.BlockSpec(memory_space=pltpu.SMEM)
    kvw = 4 * KV_WIDTH
    if latent:
        ck, cv, layer = ctx_kv
        past = ck.shape[2]
        wb = tq // WINDOW
        nwb = seq // WINDOW
        nk = sub + 2 * WINDOW
        r = lax.broadcasted_iota(jnp.int32, (sub, nk), 0)
        j = lax.broadcasted_iota(jnp.int32, (sub, nk), 1)
        band = jnp.where((j >= r) & (j <= r + 2 * WINDOW), 0.0, NEG_INF).astype(F32)
        cur = pl.BlockSpec((tq, kvw), lambda b, i: (b * nq + i, 0))
        prev = pl.BlockSpec((WINDOW, kvw),
                            lambda b, i: (b * nwb + jnp.maximum(i * wb - 1, 0), 0))
        nxt = pl.BlockSpec((WINDOW, kvw),
                           lambda b, i: (b * nwb + jnp.minimum((i + 1) * wb, nwb - 1), 0))
        cspec = pl.BlockSpec((1, 1, past, kvw), lambda b, i: (b, layer, 0, 0))
        in_specs = [smem, pl.BlockSpec((tq, ATTN_WIDTH), lambda b, i: (b * nq + i, 0)),
                    _const_spec((sub, nk)), prev, cur, nxt, prev, cur, nxt, cspec, cspec]
        args = (sink, q, band, k, k, k, v, v, v, ck, cv)
    else:
        cur = pl.BlockSpec((tq, kvw), lambda b, i: (b * nq + i, 0))
        in_specs = [smem, pl.BlockSpec((tq, ATTN_WIDTH), lambda b, i: (b * nq + i, 0)), cur, cur]
        args = (sink, q, k, v)
    return pl.pallas_call(
        kern,
        out_shape=jax.ShapeDtypeStruct((batch * seq, ATTN_WIDTH), BF16),
        grid=(batch // group, nq),
        in_specs=in_specs,
        out_specs=pl.BlockSpec((tq, ATTN_WIDTH), lambda b, i: (b * nq + i, 0)),
        compiler_params=_params(2),
        name="attn_lat" if latent else "attn_ctx",
    )(*args)


def _gla_sweep(fwd, j, nt, tt, seq_rows, gin_ref, wg_ref, bg_ref, ng_ref, s0_ref, o_ref, sout_ref,
               st_scr, g_scr, ob_scr, qin_scr, kin_scr, kend_scr, a_scr, stc_scr, dec_scr):
    w = GLA_WIDTH
    ck = GLA_CHUNK
    nc = tt // ck
    cps = seq_rows // ck
    whole_seqs = nt == 1
    tile = j if fwd else nt - 1 - j
    starts = [(ci if fwd else nc - 1 - ci) * ck for ci in range(nc)]

    rr = lax.broadcasted_iota(jnp.int32, (w, w), 0) // GLA_DK
    cc = lax.broadcasted_iota(jnp.int32, (w, w), 1) // GLA_DK
    block_diag = rr == cc
    bd_bf = block_diag.astype(F32).astype(BF16)
    tr = lax.broadcasted_iota(jnp.int32, (ck, 2 * ck), 0)
    tc = lax.broadcasted_iota(jnp.int32, (ck, 2 * ck), 1) % ck
    tri2 = ((tr >= tc) if fwd else (tr <= tc)).astype(F32).astype(BF16)
    ar = lax.broadcasted_iota(jnp.int32, (ck, w), 0)
    ac = lax.broadcasted_iota(jnp.int32, (ck, w), 1) % ck
    causal = (ar >= ac) if fwd else (ar <= ac)

    if not whole_seqs:
        @pl.when(j == 0)
        def _():
            st_scr[...] = s0_ref[0, 0]

    zpad = gin_ref[:, 4 * w:4 * w + GZ_PAD]
    zh = zpad.astype(BF16).astype(F32)
    zcat = zh + pltpu.roll(zpad - zh, 32, 1) + pltpu.roll(zh, 64, 1)
    logit = _dot(zcat.astype(BF16), wg_ref[0]) + bg_ref[0]
    g_scr[...] = _log_sigmoid(logit) * (1.0 / GLA_GATE_NORM)

    tile_rows = pl.multiple_of(tile * tt, ck)

    def decayed(ci, c0):
        rows = pl.ds(c0, ck)
        g = g_scr[rows, :]
        gh, gl = _split_bf16(g)
        b = _dot(tri2, jnp.concatenate([gh, gl], axis=0))
        btot = jnp.sum(g, axis=0, keepdims=True)
        k = gin_ref[rows, w:2 * w]
        qin_scr[rows, :] = (gin_ref[rows, 0:w] * (GLA_DK ** -0.5) * jnp.exp(b)).astype(BF16)
        kin_scr[rows, :] = (k * jnp.exp(-b)).astype(BF16)
        kend_scr[rows, :] = (k * jnp.exp(btot - b)).astype(BF16)
        dec_scr[ci] = jnp.broadcast_to(jnp.exp(btot), (8, w))

    def scores(ci, c0):
        rows = pl.ds(c0, ck)
        k_bd = jnp.concatenate([kin_scr[rows, :]] * GLA_HEADS, axis=0) * bd_bf
        a_scr[rows, :] = jnp.where(causal, _dot_nt(qin_scr[rows, :], k_bd), 0.0).astype(BF16)

    def state(ci, c0):
        rows = pl.ds(c0, ck)
        seq_i = c0 // seq_rows
        first = whole_seqs and ci % cps == 0
        st = s0_ref[min(seq_i, s0_ref.shape[0] - 1), 0] if first else st_scr[...]
        stc_scr[ci] = jnp.concatenate([st.astype(BF16)] * GLA_HEADS, axis=0) * bd_bf
        vt = gin_ref[rows, 2 * w:3 * w].T
        vt_heads = jnp.concatenate([vt[h * GLA_DK:(h + 1) * GLA_DK, :] for h in range(GLA_HEADS)],
                                   axis=1).astype(BF16)
        kend_bd = jnp.concatenate([kend_scr[rows, :]] * GLA_HEADS, axis=0) * bd_bf
        st = st * dec_scr[ci][0:1, :] + _dot(vt_heads, kend_bd)
        st_scr[...] = st
        if whole_seqs and ci % cps == cps - 1:
            sout_ref[seq_i, 0] = st

    def output(ci, c0):
        rows = pl.ds(c0, ck)
        v_bd = jnp.concatenate([gin_ref[rows, 2 * w:3 * w].astype(BF16)] * GLA_HEADS,
                               axis=0) * bd_bf
        o = _dot(a_scr[rows, :], v_bd) + _dot_nt(qin_scr[rows, :], stc_scr[ci])
        if fwd:
            o_ref[rows, :] = o
        else:
            ob_scr[pl.ds(tile_rows + c0, ck), :] = o

    stages = (decayed, scores, state, output)
    for step in range(nc + GLA_SKEW * (len(stages) - 1)):
        for si, stage in enumerate(stages):
            ci = step - GLA_SKEW * si
            if 0 <= ci < nc:
                stage(ci, starts[ci])

    if fwd:
        o = o_ref[...] + ob_scr[pl.ds(tile_rows, tt), :]
        ms = _dot((o * o).astype(BF16), bd_bf) * (1.0 / GLA_DK)
        r = gin_ref[:, 3 * w:4 * w]
        o_ref[...] = o * lax.rsqrt(ms + EPS) * ng_ref[...] * _silu(r)

    if not whole_seqs:
        @pl.when(j == nt - 1)
        def _():
            sout_ref[0, 0] = st_scr[...]


def _gla_kernel(gin_ref, wg_ref, bg_ref, ng_ref, s0_ref, o_ref, sout_ref,
                st_scr, g_scr, ob_scr, qin_scr, kin_scr, kend_scr, a_scr, stc_scr, dec_scr,
                *, nt, tt, seq_rows):
    sweep = pl.program_id(1)
    j = pl.program_id(2)
    args = (j, nt, tt, seq_rows, gin_ref, wg_ref, bg_ref, ng_ref, s0_ref, o_ref, sout_ref,
            st_scr, g_scr, ob_scr, qin_scr, kin_scr, kend_scr, a_scr, stc_scr, dec_scr)

    @pl.when(sweep == 0)
    def _():
        _gla_sweep(False, *args)

    @pl.when(sweep == 1)
    def _():
        _gla_sweep(True, *args)


def _gla(gin, wg, bg, norm_g, s0, batch, seq, tt):
    nt = max(seq // tt, 1)
    group = max(tt // seq, 1)
    w = GLA_WIDTH
    per_batch_state = s0.shape[0] == batch
    s0_group = group if per_batch_state else 1
    tile_of = lambda s, j: j * (2 * s - 1) + (1 - s) * (nt - 1)
    return pl.pallas_call(
        functools.partial(_gla_kernel, nt=nt, tt=tt, seq_rows=min(seq, tt)),
        out_shape=[jax.ShapeDtypeStruct((batch * seq, w), F32),
                   jax.ShapeDtypeStruct((batch, 2, GLA_DK, w), F32)],
        grid=(batch // group, 2, nt),
        in_specs=[
            pl.BlockSpec((tt, GLA_IN), lambda b, s, j: (b * nt + tile_of(s, j), 0)),
            pl.BlockSpec((1, GZ_PAD, w), lambda b, s, j: (1 - s, 0, 0)),
            pl.BlockSpec((1, 1, w), lambda b, s, j: (1 - s, 0, 0)),
            _const_spec((1, w)),
            pl.BlockSpec((s0_group, 1, GLA_DK, w),
                         lambda b, s, j: (b if per_batch_state else 0, 1 - s, 0, 0)),
        ],
        out_specs=[
            pl.BlockSpec((tt, w), lambda b, s, j: (b * nt + j * s, 0)),
            pl.BlockSpec((group, 1, GLA_DK, w), lambda b, s, j: (b, 1 - s, 0, 0)),
        ],
        scratch_shapes=[pltpu.VMEM((GLA_DK, w), F32), pltpu.VMEM((tt, w), F32),
                        pltpu.VMEM((nt * tt, w), F32)]
        + [pltpu.VMEM((tt, w), BF16)] * 4
        + [pltpu.VMEM((tt // GLA_CHUNK, w, w), BF16),
           pltpu.VMEM((tt // GLA_CHUNK, 8, w), F32)],
        compiler_params=_params(3),
        name="gla",
    )(gin, wg, bg, norm_g.reshape(1, w), s0)


def _lru_sweep(fwd, j, nt, tt, seq_rows, xp_ref, xc_ref, xn_ref, lg_ref, cw_ref, cb_ref,
               wg_ref, bg_ref, lam_ref, h0_ref, o_ref, sout_ref,
               carry_scr, a_scr, ab_scr, hl_scr, pl_scr, h_scr, hb_scr, xcv_scr):
    w = LRU_WIDTH
    n_seg = LRU_SEGMENTS
    n_groups = n_seg // 8
    seg_len = tt // n_seg
    tile = j if fwd else nt - 1 - j
    whole_seqs = nt == 1
    segs_per_seq = seq_rows // seg_len if whole_seqs else n_seg

    if not whole_seqs:
        @pl.when(j == 0)
        def _():
            carry_scr[...] = jnp.broadcast_to(h0_ref[0, 0], (8, w))

    all_rows = pl.ds(pl.multiple_of(tile * tt, 8), tt)
    if fwd:
        xc = xcv_scr[all_rows, :]
    else:
        pre = jnp.where(tile > 0, xp_ref[...], 0.0)
        post = jnp.where(tile < nt - 1, xn_ref[...], 0.0)
        xe = jnp.concatenate([pre, xc_ref[...], post], axis=0)
        ne = tt + 16
        taps = [pltpu.roll(xe, 2, 0)[8:8 + tt], pltpu.roll(xe, 1, 0)[8:8 + tt], xe[8:8 + tt],
                pltpu.roll(xe, ne - 1, 0)[8:8 + tt]]
        if whole_seqs and seq_rows < tt:
            pos = lax.broadcasted_iota(jnp.int32, (tt, 1), 0) % seq_rows
            taps[0] = jnp.where(pos >= 2, taps[0], 0.0)
            taps[1] = jnp.where(pos >= 1, taps[1], 0.0)
            taps[3] = jnp.where(pos < seq_rows - 1, taps[3], 0.0)
        xc = cb_ref[...] + cw_ref[2:3, :] * taps[2]
        xc = xc + cw_ref[0:1, :] * taps[0]
        xc = xc + cw_ref[1:2, :] * taps[1]
        xc = xc + cw_ref[3:4, :] * taps[3]
        xcv_scr[all_rows, :] = xc

    ri = _dot(xc.astype(BF16), wg_ref[0]) + bg_ref[0]
    t_a = jnp.tanh(ri[:, 0:w])
    t_x = jnp.tanh(ri[:, w:2 * w])
    c2 = (-0.5 * LRU_C) * _softplus(-lam_ref[0])
    a = jnp.exp(t_a * c2 + c2)
    hx = 0.5 * xc
    bt = jnp.sqrt(1.0 - a * a) * (t_x * hx + hx)

    nl = w // 128
    pitch = seg_len + LRU_SEG_PAD
    for lt in range(nl):
        for k in range(n_seg):
            dst = slice(k * pitch, k * pitch + seg_len)
            src = slice(k * seg_len, (k + 1) * seg_len)
            a_scr[lt, dst, :] = a[src, lt * 128:(lt + 1) * 128]
            ab_scr[lt, dst, :] = bt[src, lt * 128:(lt + 1) * 128]

    chains = [(lt, g) for lt in range(nl) for g in range(n_groups)]

    def step(ii, hp):
        i = ii if fwd else seg_len - 1 - ii
        rows = pl.ds(pl.multiple_of(i * 8, 8), 8)
        out = []
        for ch, (lt, g) in enumerate(chains):
            h, p = hp[ch]
            seg_rows = pl.ds(g * 8 * pitch + i, 8, stride=pitch)
            a_i = a_scr[lt, seg_rows, :]
            h = a_i * h + ab_scr[lt, seg_rows, :]
            p = a_i * p
            hl_scr[ch, rows, :] = h
            pl_scr[ch, rows, :] = p
            out.append((h, p))
        return tuple(out)

    init = tuple((jnp.zeros((8, 128), F32), jnp.ones((8, 128), F32)) for _ in chains)
    ends = lax.fori_loop(0, seg_len, step, init, unroll=8)
    h_end = jnp.concatenate([jnp.concatenate([ends[lt * n_groups + g][0] for lt in range(nl)],
                                             axis=1) for g in range(n_groups)], axis=0)
    p_end = jnp.concatenate([jnp.concatenate([ends[lt * n_groups + g][1] for lt in range(nl)],
                                             axis=1) for g in range(n_groups)], axis=0)

    c = None if whole_seqs else carry_scr[0:1, :]
    c_in = [None] * n_seg
    for n, k in enumerate(range(n_seg) if fwd else range(n_seg - 1, -1, -1)):
        seq_i = k // segs_per_seq
        if whole_seqs and n % segs_per_seq == 0:
            c = h0_ref[min(seq_i, h0_ref.shape[0] - 1), 0]
        c_in[k] = c
        c = h_end[k:k + 1, :] + p_end[k:k + 1, :] * c
        if whole_seqs and n % segs_per_seq == segs_per_seq - 1:
            sout_ref[seq_i, 0] = jnp.broadcast_to(c, (8, w))
    if not whole_seqs:
        carry_scr[...] = jnp.broadcast_to(c, (8, w))

    for k in range(n_seg):
        g, kk = divmod(k, 8)
        for i0 in range(0, seg_len, 8):
            src = pl.ds(i0 * 8 + kk, 8, stride=8)
            r0 = k * seg_len + i0
            for lt in range(nl):
                ch = lt * n_groups + g
                c_k = c_in[k][:, lt * 128:(lt + 1) * 128]
                h_scr[lt, r0:r0 + 8, :] = hl_scr[ch, src, :] + pl_scr[ch, src, :] * c_k
    h_all = jnp.concatenate([h_scr[lt] for lt in range(nl)], axis=1)
    if fwd:
        o_ref[...] = (h_all + hb_scr[all_rows, :]) * _gelu_tanh(lg_ref[...])
    else:
        hb_scr[all_rows, :] = h_all

    if not whole_seqs:
        @pl.when(j == nt - 1)
        def _():
            sout_ref[0, 0] = carry_scr[...]


def _lru_kernel(xp_ref, xc_ref, xn_ref, lg_ref, cw_ref, cb_ref, wg_ref, bg_ref,
                lam_ref, h0_ref, o_ref, sout_ref, carry_scr, a_scr, ab_scr, hl_scr, pl_scr,
                h_scr, hb_scr, xcv_scr, *, nt, tt, seq_rows):
    sweep = pl.program_id(1)
    j = pl.program_id(2)
    args = (j, nt, tt, seq_rows, xp_ref, xc_ref, xn_ref, lg_ref, cw_ref, cb_ref, wg_ref, bg_ref,
            lam_ref, h0_ref, o_ref, sout_ref, carry_scr, a_scr, ab_scr, hl_scr, pl_scr,
            h_scr, hb_scr, xcv_scr)

    @pl.when(sweep == 0)
    def _():
        _lru_sweep(False, *args)

    @pl.when(sweep == 1)
    def _():
        _lru_sweep(True, *args)


def _lru(lx, lg, conv_w, conv_b, wg, bg, lam, h0, batch, seq, tt):
    nt = max(seq // tt, 1)
    group = max(tt // seq, 1)
    w = LRU_WIDTH
    t8 = tt // 8
    n8 = nt * t8
    per_batch_state = h0.shape[0] == batch
    h0_group = group if per_batch_state else 1
    tile_of = lambda s, j: j * (2 * s - 1) + (1 - s) * (nt - 1)
    return pl.pallas_call(
        functools.partial(_lru_kernel, nt=nt, tt=tt, seq_rows=min(seq, tt)),
        out_shape=[jax.ShapeDtypeStruct((batch * seq, w), F32),
                   jax.ShapeDtypeStruct((batch, 2, 8, w), F32)],
        grid=(batch // group, 2, nt),
        in_specs=[
            pl.BlockSpec((8, w), lambda b, s, j:
                         (b * n8 + jnp.maximum(tile_of(s, j) * t8 - 1, 0), 0)),
            pl.BlockSpec((tt, w), lambda b, s, j: (b * nt + tile_of(s, j), 0)),
            pl.BlockSpec((8, w), lambda b, s, j:
                         (b * n8 + jnp.minimum((tile_of(s, j) + 1) * t8, n8 - 1), 0)),
            pl.BlockSpec((tt, w), lambda b, s, j: (b * nt + j * s, 0)),
            _const_spec((4, w)),
            _const_spec((1, w)),
            pl.BlockSpec((1, w, 2 * w), lambda b, s, j: (1 - s, 0, 0)),
            pl.BlockSpec((1, 1, 2 * w), lambda b, s, j: (1 - s, 0, 0)),
            pl.BlockSpec((1, 1, w), lambda b, s, j: (1 - s, 0, 0)),
            pl.BlockSpec((h0_group, 1, 1, w),
                         lambda b, s, j: (b if per_batch_state else 0, 1 - s, 0, 0)),
        ],
        out_specs=[
            pl.BlockSpec((tt, w), lambda b, s, j: (b * nt + j * s, 0)),
            pl.BlockSpec((group, 1, 8, w), lambda b, s, j: (b, 1 - s, 0, 0)),
        ],
        scratch_shapes=[pltpu.VMEM((8, w), F32)]
        + [pltpu.VMEM((w // 128, tt + LRU_SEGMENTS * LRU_SEG_PAD, 128), F32)] * 2
        + [pltpu.VMEM((w // 128 * (LRU_SEGMENTS // 8), 8 * tt // LRU_SEGMENTS, 128), F32)] * 2
        + [pltpu.VMEM((w // 128, tt, 128), F32)]
        + [pltpu.VMEM((nt * tt, w), F32)] * 2,
        compiler_params=_params(3),
        name="lru",
    )(lx, lx, lx, lg, conv_w, conv_b.reshape(1, w), wg, bg, lam, h0)


def _mlp_kernel(x_ref, att_ref, gla_ref, lru_ref, mod_ref, n2_ref, fn_ref,
                wo_ref, w1_ref, w2_ref, o_ref, *, final, ff_chunk):
    d = D_MODEL
    mix = _dot(att_ref[...], wo_ref[0, 0:ATTN_WIDTH, :])
    mix = mix + _dot(gla_ref[...].astype(BF16), wo_ref[0, ATTN_WIDTH:ATTN_WIDTH + GLA_WIDTH, :])
    mix = mix + _dot(lru_ref[...].astype(BF16), wo_ref[0, ATTN_WIDTH + GLA_WIDTH:d, :])
    g1 = mod_ref[0, 0, :, 2 * d:3 * d]
    sh2 = mod_ref[0, 0, :, 3 * d:4 * d]
    sc2 = mod_ref[0, 0, :, 4 * d:5 * d]
    g2 = mod_ref[0, 0, :, 5 * d:6 * d]
    x = x_ref[...] + g1 * mix
    ms = jnp.mean(x * x, axis=-1, keepdims=True)
    h = (x * lax.rsqrt(ms + EPS) * n2_ref[...] * (1.0 + sc2) + sh2).astype(BF16)
    y = None
    for c in range(D_FF // ff_chunk):
        cols = slice(c * ff_chunk, (c + 1) * ff_chunk)
        u = jnp.maximum(_dot(h, w1_ref[0, :, cols]), 0.0)
        part = _dot((u * u).astype(BF16), w2_ref[0, cols, :])
        y = part if y is None else y + part
    x = x + g2 * y
    if final:
        ms = jnp.mean(x * x, axis=-1, keepdims=True)
        x = x * lax.rsqrt(ms + EPS) * fn_ref[...]
    o_ref[...] = x


def _out_mlp(x2d, att, gla, lru, mod, norm2, final_norm, wo, w1, w2, layer,
             rows_per_mod, mod_row0, tt, final):
    n, d = x2d.shape
    nt = n // tt
    mod_map = lambda i: (layer, mod_row0 + (i * tt) // rows_per_mod, 0, 0)
    row = lambda wd: pl.BlockSpec((tt, wd), lambda i: (i, 0))
    resident = lambda shape: pl.BlockSpec((1,) + shape, lambda i: (layer, 0, 0),
                                          pipeline_mode=pl.Buffered(1))
    return pl.pallas_call(
        functools.partial(_mlp_kernel, final=final, ff_chunk=512),
        out_shape=jax.ShapeDtypeStruct((n, d), F32),
        grid=(nt,),
        in_specs=[row(d), row(ATTN_WIDTH), row(GLA_WIDTH), row(LRU_WIDTH),
                  pl.BlockSpec((1, 1, 1, 6 * d), mod_map),
                  _const_spec((1, d)), _const_spec((1, d)),
                  resident((d, d)), resident((d, D_FF)), resident((D_FF, d))],
        out_specs=row(d),
        compiler_params=_params(1),
        name="out_mlp",
    )(x2d, att, gla, lru, mod, norm2.reshape(1, d), final_norm.reshape(1, d), wo, w1, w2)


def _prep_w_in(w_in):
    wz = w_in[..., C_A_END:C_B].astype(BF16)
    pad = jnp.zeros(w_in.shape[:-1] + (GZ_PAD - (C_B - C_A_END),), BF16)
    return (w_in[..., :C_A_END].astype(BF16), jnp.concatenate([wz, pad], axis=-1),
            w_in[..., C_B:].astype(BF16))


def _prep_gate_w(gw):
    out = []
    nz = 2 * GLA_GATE_RANK
    for dr in (0, 1):
        wf = jnp.zeros((nz, GLA_WIDTH), F32)
        wf = wf.at[dr * GLA_GATE_RANK:(dr + 1) * GLA_GATE_RANK].set(gw[dr])
        hi, lo = _split_bf16(wf)
        out.append(jnp.concatenate([hi, hi, lo, jnp.zeros((GZ_PAD - 3 * nz, GLA_WIDTH), BF16)],
                                   axis=0))
    return jnp.stack(out)


def _block_diag(w4):
    n, a, b = w4.shape
    eye = jnp.eye(n, dtype=w4.dtype)
    return jnp.einsum("nab,nm->namb", w4, eye).reshape(n * a, n * b)


def _rope_tables(seq):
    rows = seq // GRID_W
    nf = HEAD_DIM // 4
    inv = ROPE_BASE ** (-jnp.arange(nf, dtype=F32) / nf)
    ang_r = jnp.arange(rows, dtype=F32)[:, None] * inv
    ang_c = jnp.arange(GRID_W, dtype=F32)[:, None] * inv
    by_row = lambda t: jnp.repeat(t, GRID_W, axis=0)
    by_col = lambda t: jnp.tile(t, (rows, 1))
    cos_r, sin_r = by_row(jnp.cos(ang_r)), by_row(jnp.sin(ang_r))
    cos_c, sin_c = by_col(jnp.cos(ang_c)), by_col(jnp.sin(ang_c))
    cos = jnp.concatenate([cos_r, cos_r, cos_c, cos_c], axis=1)
    sin = jnp.concatenate([-sin_r, sin_r, -sin_c, sin_c], axis=1)
    return jnp.concatenate([cos] * 2, axis=1), jnp.concatenate([sin] * 2, axis=1)


def _half_variants_of_cache(cache):
    a = cache.astype(BF16)
    z = jnp.zeros_like(a[..., 0, :])
    return jnp.concatenate([a[..., 0, :], z, z, a[..., 0, :], a[..., 1, :], z, z, a[..., 1, :]],
                           axis=-1)


def _gla_state_to_internal(s):
    b = s.shape[0]
    return jnp.transpose(s, (0, 1, 4, 2, 3)).reshape(b, 2, GLA_DK, GLA_WIDTH)


def kernel(x_prompt, x_sample, c, cache_k, cache_v, state_gla, state_lru, c_ctx, w_mod, b_mod, norm1, norm2, w_in, attn_sink, gla_gate_w, gla_gate_b, gla_norm, lru_conv_w, lru_conv_b, lru_wa, lru_ba, lru_wx, lru_bx, lru_lambda, w_out, w_mlp1, w_mlp2, final_norm):
    depth = w_in.shape[0]
    bc, sc_len, d = x_prompt.shape
    bl, sl_len, _ = x_sample.shape
    past = cache_k.shape[2]

    cond = jnp.concatenate([c_ctx[None], c, jnp.zeros((MOD_ROWS - 1 - bl, d), F32)], axis=0)
    mod = _ada_mod(cond, w_mod, b_mod)

    rope_tabs = _rope_tables(sl_len)
    ck = _half_variants_of_cache(cache_k)
    cv = _half_variants_of_cache(cache_v)
    gla_s0_ctx = jnp.zeros((1, 2, GLA_DK, GLA_WIDTH), F32)
    lru_s0_ctx = jnp.zeros((1, 2, 1, LRU_WIDTH), F32)

    xp = x_prompt.reshape(bc * sc_len, d)
    xs = x_sample.reshape(bl * sl_len, d)
    w_in_b = _prep_w_in(w_in)
    wo = w_out.astype(BF16)
    w1 = w_mlp1.astype(BF16)
    w2 = w_mlp2.astype(BF16)
    ks, vs, sgs, sls = [], [], [], []
    for l in range(depth):
        gate_w = _prep_gate_w(gla_gate_w[l])
        gate_b = gla_gate_b[l].reshape(2, 1, GLA_WIDTH)
        lru_wg = (0.5 * jnp.stack([jnp.concatenate(
            [_block_diag(lru_wa[l, dr]), _block_diag(lru_wx[l, dr])], axis=1)
            for dr in (0, 1)])).astype(BF16)
        lru_bg = (0.5 * jnp.stack([jnp.concatenate([lru_ba[l, dr], lru_bx[l, dr]])
                                   for dr in (0, 1)])).reshape(2, 1, 2 * LRU_WIDTH)
        lam = lru_lambda[l].reshape(2, 1, LRU_WIDTH)
        final = l == depth - 1

        for is_lat in (False, True):
            if is_lat:
                x2d, batch, seq, row0, rpm = xs, bl, sl_len, 1, sl_len
                tabs, tt_proj, tq, tt_gla, tt_lru = rope_tabs, 1024, 512, 1024, 1024
                gla_s0 = _gla_state_to_internal(state_gla[:, l])
                lru_s0 = state_lru[:, l].reshape(bl, 2, 1, LRU_WIDTH)
            else:
                x2d, batch, seq, row0, rpm = xp, bc, sc_len, 0, bc * sc_len
                tabs, tt_proj, tq, tt_gla, tt_lru = None, 1024, 2 * sc_len, 4 * sc_len, 4 * sc_len
                gla_s0, lru_s0 = gla_s0_ctx, lru_s0_ctx
            proj = _in_proj(x2d, mod, norm1[l], w_in_b, l, tabs, rpm, row0, tt_proj)
            q, kvar, vvar = proj[:3]
            gin, lx, lg = proj[-3:]
            att = _attention(q, kvar, vvar, attn_sink[l], batch, seq, tq, ATTN_SUB,
                             ctx_kv=(ck, cv, l) if is_lat else None)
            gla, sg = _gla(gin, gate_w, gate_b, gla_norm[l], gla_s0, batch, seq, tt_gla)
            lru, sl = _lru(lx, lg, lru_conv_w[l], lru_conv_b[l], lru_wg, lru_bg, lam, lru_s0,
                           batch, seq, tt_lru)
            x2d = _out_mlp(x2d, att, gla, lru, mod, norm2[l], final_norm, wo, w1, w2, l,
                           rpm, row0, 1024, final)
            if is_lat:
                xs = x2d
            else:
                xp = x2d
                ks.append(proj[3].reshape(bc, sc_len, KV_WIDTH))
                vs.append(proj[4].reshape(bc, sc_len, KV_WIDTH))
                sgs.append(sg)
                sls.append(sl[:, :, 0, :])

    y_prompt = xp.reshape(bc, sc_len, d)
    y_sample = xs.reshape(bl, sl_len, d)
    kv_shape = (bc, depth, sc_len, KV_WIDTH // HEAD_DIM, HEAD_DIM)
    new_k = jnp.stack(ks, axis=1).reshape(kv_shape)
    new_v = jnp.stack(vs, axis=1).reshape(kv_shape)
    sg_all = jnp.stack(sgs, axis=1)
    new_sg = jnp.transpose(
        sg_all.reshape(bc, depth, 2, GLA_DK, GLA_HEADS, GLA_DK), (0, 1, 2, 4, 5, 3))
    new_sl = jnp.stack(sls, axis=1)
    return (y_prompt, y_sample, new_k, new_v, new_sg, new_sl)
```

```python
import functools

import jax
import jax.numpy as jnp
from jax import lax
from jax.experimental import pallas as pl
from jax.experimental.pallas import tpu as pltpu

F32 = jnp.float32
BF16 = jnp.bfloat16
LANE = 128

D_MODEL = 1024
GRID_W = 64
EPS = 1e-6
HEAD_DIM = 64
ATTN_WIDTH = 512
ATTN_HEADS = 8
KV_WIDTH = 128
WINDOW = 128
ATTN_LOOKAHEAD = 1
ATTN_SUB = 256
ROPE_BASE = 10000.0
NEG_INF = -1e30
GLA_WIDTH = 256
GLA_DK = 64
GLA_HEADS = 4
GLA_GATE_RANK = 16
GLA_GATE_NORM = 16.0
GLA_CHUNK = 64
GLA_SKEW = 2
LRU_WIDTH = 256
LRU_C = 8.0
LRU_SEG_PAD = 8
LRU_SEGMENTS = 16
D_FF = 4096
MOD_ROWS = 8
GZ_PAD = 128
C_Q, C_K, C_V, C_GLA, C_A_END, C_B = 0, 512, 640, 768, 1536, 1568
GLA_IN = 4 * GLA_WIDTH + GZ_PAD
VMEM_LIMIT = 56 * 1024 * 1024

NT_DIMS = (((1,), (1,)), ((), ()))


def _split_bf16(a):
    hi = a.astype(BF16)
    lo = (a - hi.astype(F32)).astype(BF16)
    return hi, lo


def _dot(a, b):
    return jnp.dot(a, b, preferred_element_type=F32)


def _dot_nt(a, b):
    return lax.dot_general(a, b, NT_DIMS, preferred_element_type=F32)


def _dot_x3(a, b):
    ah, al = _split_bf16(a)
    bh, bl = _split_bf16(b)
    return _dot(ah, bh) + (_dot(ah, bl) + _dot(al, bh))


def _softplus(y):
    return jnp.maximum(y, 0.0) + jnp.log1p(jnp.exp(-jnp.abs(y)))


def _log_sigmoid(y):
    return jnp.minimum(y, 0.0) - jnp.log(1.0 + jnp.exp(-jnp.abs(y)))


def _sigmoid(y):
    return 0.5 * jnp.tanh(0.5 * y) + 0.5


def _silu(y):
    return y * _sigmoid(y)


def _gelu_tanh(y):
    c = 0.7978845608028654
    hy = 0.5 * y
    return hy * jnp.tanh(y * (c + (c * 0.044715) * (y * y))) + hy


def _params(n_axes):
    return pltpu.CompilerParams(
        dimension_semantics=("arbitrary",) * n_axes, vmem_limit_bytes=VMEM_LIMIT)


def _const_spec(shape):
    nd = len(shape)
    return pl.BlockSpec(shape, lambda *_: (0,) * nd)


def _ada_kernel(c_ref, w_ref, b_ref, o_ref):
    s = _silu(c_ref[...])
    res = _dot_x3(s, w_ref[0]) + b_ref[0]
    for r in range(MOD_ROWS):
        o_ref[0, r] = res[r:r + 1, :]


def _ada_mod(cond, w_mod, b_mod):
    depth, d, n = w_mod.shape
    tn = 1536
    return pl.pallas_call(
        _ada_kernel,
        out_shape=jax.ShapeDtypeStruct((depth, MOD_ROWS, 1, n), F32),
        grid=(depth, n // tn),
        in_specs=[
            pl.BlockSpec((MOD_ROWS, d), lambda l, j: (0, 0)),
            pl.BlockSpec((1, d, tn), lambda l, j: (l, 0, j)),
            pl.BlockSpec((1, 1, tn), lambda l, j: (l, 0, j)),
        ],
        out_specs=pl.BlockSpec((1, MOD_ROWS, 1, tn), lambda l, j: (l, 0, 0, j)),
        compiler_params=_params(2),
        name="ada_mod",
    )(cond, w_mod, b_mod.reshape(depth, 1, n))


def _rope(x, cos, sin_signed):
    w = x.shape[1]
    lane = lax.broadcasted_iota(jnp.int32, (1, w), 1)
    first = (lane % 32) < 16
    swapped = jnp.where(first, pltpu.roll(x, w - 16, 1), pltpu.roll(x, 16, 1))
    return x * cos + swapped * sin_signed


def _half_variants(a):
    lane = lax.broadcasted_iota(jnp.int32, (1, LANE), 1)
    low = lane < HEAD_DIM
    ar = pltpu.roll(a, HEAD_DIM, 1)
    z = jnp.zeros_like(a)
    parts = [jnp.where(low, a, z), jnp.where(low, z, ar),
             jnp.where(low, ar, z), jnp.where(low, z, a)]
    return jnp.concatenate([t.astype(BF16) for t in parts], axis=1)


def _inproj_kernel(*refs, rope):
    if rope:
        (x_ref, mod_ref, n1_ref, wa_ref, wz_ref, wb_ref, cos_ref, sin_ref,
         q_ref, k_ref, v_ref, gla_ref, lx_ref, lg_ref) = refs
    else:
        (x_ref, mod_ref, n1_ref, wa_ref, wz_ref, wb_ref,
         q_ref, k_ref, v_ref, gla_ref, lx_ref, lg_ref) = refs
    d = D_MODEL
    x = x_ref[...]
    ms = jnp.mean(x * x, axis=-1, keepdims=True)
    xn = x * lax.rsqrt(ms + EPS) * n1_ref[...]
    sh = mod_ref[0, 0, :, 0:d]
    sc = mod_ref[0, 0, :, d:2 * d]
    h = (xn * (1.0 + sc) + sh).astype(BF16)
    p_att = _dot(h, wa_ref[0, :, C_Q:C_GLA])
    p_gla = _dot(h, wa_ref[0, :, C_GLA:C_A_END])
    q = p_att[:, C_Q:C_K]
    k = p_att[:, C_K:C_V]
    v = p_att[:, C_V:C_GLA]
    if rope:
        cos = cos_ref[...]
        sin = sin_ref[...]
        k = _rope(k, cos, sin)
        q = _rope(q, jnp.concatenate([cos] * 4, axis=1), jnp.concatenate([sin] * 4, axis=1))
    q_ref[...] = (q * (HEAD_DIM ** -0.5)).astype(BF16)
    k_ref[...] = k
    v_ref[...] = v
    w = GLA_WIDTH
    pb = _dot(h, wb_ref[0])
    gla_ref[:, 0:3 * w] = p_gla
    gla_ref[:, 3 * w:4 * w] = pb[:, 0:w]
    gla_ref[:, 4 * w:4 * w + GZ_PAD] = _dot(h, wz_ref[0])
    lx_ref[...] = pb[:, w:w + LRU_WIDTH]
    lg_ref[...] = pb[:, w + LRU_WIDTH:w + 2 * LRU_WIDTH]


def _in_proj(x2d, mod, norm1, w_abz, layer, rope_tabs, rows_per_mod, mod_row0, tt):
    n, d = x2d.shape
    nt = n // tt
    rope = rope_tabs is not None
    mod_map = lambda i: (layer, mod_row0 + (i * tt) // rows_per_mod, 0, 0)
    by_layer = lambda arr: pl.BlockSpec((1,) + arr.shape[1:], lambda i: (layer, 0, 0))
    in_specs = [
        pl.BlockSpec((tt, d), lambda i: (i, 0)),
        pl.BlockSpec((1, 1, 1, 6 * d), mod_map),
        _const_spec((1, d)),
    ] + [by_layer(arr) for arr in w_abz]
    args = [x2d, mod, norm1.reshape(1, d)] + list(w_abz)
    if rope:
        t = rope_tabs[0].shape[0]
        tpb = t // tt
        in_specs += [pl.BlockSpec((tt, LANE), lambda i: (i % tpb, 0))] * 2
        args += list(rope_tabs)
    outs = [(ATTN_WIDTH, BF16), (KV_WIDTH, F32), (KV_WIDTH, F32),
            (GLA_IN, F32), (LRU_WIDTH, F32), (LRU_WIDTH, F32)]
    widths = [o[0] for o in outs]
    dtypes = [o[1] for o in outs]
    return pl.pallas_call(
        functools.partial(_inproj_kernel, rope=rope),
        out_shape=[jax.ShapeDtypeStruct((n, wd), dt) for wd, dt in zip(widths, dtypes)],
        grid=(nt,),
        in_specs=in_specs,
        out_specs=[pl.BlockSpec((tt, wd), lambda i: (i, 0)) for wd in widths],
        compiler_params=_params(1),
        name="in_proj_lat" if rope else "in_proj_ctx",
    )(*args)


def _attn_kernel(*refs, latent, tq, sub, seq):
    n_units = tq // sub
    if latent:
        (sink_ref, q_ref, band_ref, kp_ref, kc_ref, kn_ref, vp_ref, vc_ref, vn_ref,
         ck_ref, cv_ref, o_ref) = refs
        i = pl.program_id(1)
        k_loc = _half_variants(jnp.concatenate([kp_ref[...], kc_ref[...], kn_ref[...]], axis=0))
        v_loc = _half_variants(jnp.concatenate([vp_ref[...], vc_ref[...], vn_ref[...]], axis=0))
        nk = sub + 2 * WINDOW
        k_ctx = _half_variants(ck_ref[0, 0])
        v_ctx = _half_variants(cv_ref[0, 0])
        key_rows = [slice(u * sub, u * sub + nk) for u in range(n_units)]
        biases = []
        for u in range(n_units):
            kpos = i * tq + u * sub - WINDOW + lax.broadcasted_iota(jnp.int32, (1, nk), 1)
            biases.append(band_ref[...] + jnp.where((kpos >= 0) & (kpos < seq), 0.0, NEG_INF))
    else:
        sink_ref, q_ref, kc_ref, vc_ref, o_ref = refs
        k_loc = _half_variants(kc_ref[...])
        v_loc = _half_variants(vc_ref[...])
        key_rows = [slice(u * sub, (u + 1) * sub) for u in range(n_units)]
        biases = [None] * n_units
    block = lambda t, kv, half: t[:, (2 * kv + half) * LANE:(2 * kv + half + 1) * LANE]
    with_ones = lambda t: jnp.concatenate([t, jnp.ones_like(t)], axis=1)

    def scores(item):
        u, head = divmod(item, ATTN_HEADS)
        c, half = divmod(head, 2)
        qc = q_ref[u * sub:(u + 1) * sub, c * LANE:(c + 1) * LANE]
        s = _dot_nt(qc, block(k_loc[key_rows[u]], c // 2, half))
        if biases[u] is not None:
            s = s + biases[u]
        return s, (_dot_nt(qc, block(k_ctx, c // 2, half)) if latent else None)

    n_items = n_units * ATTN_HEADS
    ready = [scores(it) for it in range(ATTN_LOOKAHEAD)]
    acc = None
    for item in range(n_items):
        u, head = divmod(item, ATTN_HEADS)
        c, half = divmod(head, 2)
        s, s_c = ready.pop(0)
        if item + ATTN_LOOKAHEAD < n_items:
            ready.append(scores(item + ATTN_LOOKAHEAD))
        sink = sink_ref[head]
        m = jnp.maximum(jnp.max(s, axis=-1, keepdims=True), sink)
        if latent:
            m = jnp.maximum(m, jnp.max(s_c, axis=-1, keepdims=True))
        o = _dot(jnp.exp(s - m).astype(BF16), with_ones(block(v_loc[key_rows[u]], c // 2, half)))
        if latent:
            o = o + _dot(jnp.exp(s_c - m).astype(BF16), with_ones(block(v_ctx, c // 2, half)))
        denom = jnp.exp(sink - m) + o[:, LANE:2 * LANE]
        o = o[:, 0:LANE] * (1.0 / denom)
        if half == 0:
            acc = o
        else:
            o_ref[u * sub:(u + 1) * sub, c * LANE:(c + 1) * LANE] = (acc + o).astype(BF16)


def _attention(q, k, v, sink, batch, seq, tq, sub, ctx_kv=None):
    latent = ctx_kv is not None
    nq = max(seq // tq, 1)
    group = max(tq // seq, 1)
    kern = functools.partial(_attn_kernel, latent=latent, tq=tq, sub=sub, seq=seq)
    smem = pl.BlockSpec(memory_space=pltpu.SMEM)
    kvw = KV_WIDTH
    if latent:
        ck, cv, layer = ctx_kv
        past = ck.shape[2]
        wb = tq // WINDOW
        nwb = seq // WINDOW
        nk = sub + 2 * WINDOW
        r = lax.broadcasted_iota(jnp.int32, (sub, nk), 0)
        j = lax.broadcasted_iota(jnp.int32, (sub, nk), 1)
        band = jnp.where((j >= r) & (j <= r + 2 * WINDOW), 0.0, NEG_INF).astype(F32)
        cur = pl.BlockSpec((tq, kvw), lambda b, i: (b * nq + i, 0))
        prev = pl.BlockSpec((WINDOW, kvw),
                            lambda b, i: (b * nwb + jnp.maximum(i * wb - 1, 0), 0))
        nxt = pl.BlockSpec((WINDOW, kvw),
                           lambda b, i: (b * nwb + jnp.minimum((i + 1) * wb, nwb - 1), 0))
        cspec = pl.BlockSpec((1, 1, past, kvw), lambda b, i: (b, layer, 0, 0))
        in_specs = [smem, pl.BlockSpec((tq, ATTN_WIDTH), lambda b, i: (b * nq + i, 0)),
                    _const_spec((sub, nk)), prev, cur, nxt, prev, cur, nxt, cspec, cspec]
        args = (sink, q, band, k, k, k, v, v, v, ck, cv)
    else:
        cur = pl.BlockSpec((tq, kvw), lambda b, i: (b * nq + i, 0))
        in_specs = [smem, pl.BlockSpec((tq, ATTN_WIDTH), lambda b, i: (b * nq + i, 0)), cur, cur]
        args = (sink, q, k, v)
    return pl.pallas_call(
        kern,
        out_shape=jax.ShapeDtypeStruct((batch * seq, ATTN_WIDTH), BF16),
        grid=(batch // group, nq),
        in_specs=in_specs,
        out_specs=pl.BlockSpec((tq, ATTN_WIDTH), lambda b, i: (b * nq + i, 0)),
        compiler_params=_params(2),
        name="attn_lat" if latent else "attn_ctx",
    )(*args)


def _gla_sweep(fwd, j, nt, tt, seq_rows, gin_ref, wg_ref, bg_ref, ng_ref, s0_ref, o_ref, sout_ref,
               st_scr, g_scr, ob_scr, qin_scr, kin_scr, kend_scr, a_scr, stc_scr, dec_scr):
    w = GLA_WIDTH
    ck = GLA_CHUNK
    nc = tt // ck
    cps = seq_rows // ck
    whole_seqs = nt == 1
    tile = j if fwd else nt - 1 - j
    starts = [(ci if fwd else nc - 1 - ci) * ck for ci in range(nc)]

    rr = lax.broadcasted_iota(jnp.int32, (w, w), 0) // GLA_DK
    cc = lax.broadcasted_iota(jnp.int32, (w, w), 1) // GLA_DK
    block_diag = rr == cc
    bd_bf = block_diag.astype(F32).astype(BF16)
    tr = lax.broadcasted_iota(jnp.int32, (ck, 2 * ck), 0)
    tc = lax.broadcasted_iota(jnp.int32, (ck, 2 * ck), 1) % ck
    tri2 = ((tr >= tc) if fwd else (tr <= tc)).astype(F32).astype(BF16)
    ar = lax.broadcasted_iota(jnp.int32, (ck, w), 0)
    ac = lax.broadcasted_iota(jnp.int32, (ck, w), 1) % ck
    causal = (ar >= ac) if fwd else (ar <= ac)

    if not whole_seqs:
        @pl.when(j == 0)
        def _():
            st_scr[...] = s0_ref[0, 0]

    zpad = gin_ref[:, 4 * w:4 * w + GZ_PAD]
    zh = zpad.astype(BF16).astype(F32)
    zcat = zh + pltpu.roll(zpad - zh, 32, 1) + pltpu.roll(zh, 64, 1)
    logit = _dot(zcat.astype(BF16), wg_ref[0]) + bg_ref[0]
    g_scr[...] = _log_sigmoid(logit) * (1.0 / GLA_GATE_NORM)

    tile_rows = pl.multiple_of(tile * tt, ck)

    def decayed(ci, c0):
        rows = pl.ds(c0, ck)
        g = g_scr[rows, :]
        gh, gl = _split_bf16(g)
        b = _dot(tri2, jnp.concatenate([gh, gl], axis=0))
        btot = jnp.sum(g, axis=0, keepdims=True)
        k = gin_ref[rows, w:2 * w]
        qin_scr[rows, :] = (gin_ref[rows, 0:w] * (GLA_DK ** -0.5) * jnp.exp(b)).astype(BF16)
        kin_scr[rows, :] = (k * jnp.exp(-b)).astype(BF16)
        kend_scr[rows, :] = (k * jnp.exp(btot - b)).astype(BF16)
        dec_scr[ci] = jnp.broadcast_to(jnp.exp(btot), (8, w))

    def scores(ci, c0):
        rows = pl.ds(c0, ck)
        k_bd = jnp.concatenate([kin_scr[rows, :]] * GLA_HEADS, axis=0) * bd_bf
        a_scr[rows, :] = jnp.where(causal, _dot_nt(qin_scr[rows, :], k_bd), 0.0).astype(BF16)

    def state(ci, c0):
        rows = pl.ds(c0, ck)
        seq_i = c0 // seq_rows
        first = whole_seqs and ci % cps == 0
        st = s0_ref[min(seq_i, s0_ref.shape[0] - 1), 0] if first else st_scr[...]
        stc_scr[ci] = jnp.concatenate([st.astype(BF16)] * GLA_HEADS, axis=0) * bd_bf
        vt = gin_ref[rows, 2 * w:3 * w].T
        vt_heads = jnp.concatenate([vt[h * GLA_DK:(h + 1) * GLA_DK, :] for h in range(GLA_HEADS)],
                                   axis=1).astype(BF16)
        kend_bd = jnp.concatenate([kend_scr[rows, :]] * GLA_HEADS, axis=0) * bd_bf
        st = st * dec_scr[ci][0:1, :] + _dot(vt_heads, kend_bd)
        st_scr[...] = st
        if whole_seqs and ci % cps == cps - 1:
            sout_ref[seq_i, 0] = st

    def output(ci, c0):
        rows = pl.ds(c0, ck)
        v_bd = jnp.concatenate([gin_ref[rows, 2 * w:3 * w].astype(BF16)] * GLA_HEADS,
                               axis=0) * bd_bf
        o = _dot(a_scr[rows, :], v_bd) + _dot_nt(qin_scr[rows, :], stc_scr[ci])
        if fwd:
            o_ref[rows, :] = o
        else:
            ob_scr[pl.ds(tile_rows + c0, ck), :] = o

    stages = (decayed, scores, state, output)
    for step in range(nc + GLA_SKEW * (len(stages) - 1)):
        for si, stage in enumerate(stages):
            ci = step - GLA_SKEW * si
            if 0 <= ci < nc:
                stage(ci, starts[ci])

    if fwd:
        o = o_ref[...] + ob_scr[pl.ds(tile_rows, tt), :]
        ms = _dot((o * o).astype(BF16), bd_bf) * (1.0 / GLA_DK)
        r = gin_ref[:, 3 * w:4 * w]
        o_ref[...] = o * lax.rsqrt(ms + EPS) * ng_ref[...] * _silu(r)

    if not whole_seqs:
        @pl.when(j == nt - 1)
        def _():
            sout_ref[0, 0] = st_scr[...]


def _gla_kernel(gin_ref, wg_ref, bg_ref, ng_ref, s0_ref, o_ref, sout_ref,
                st_scr, g_scr, ob_scr, qin_scr, kin_scr, kend_scr, a_scr, stc_scr, dec_scr,
                *, nt, tt, seq_rows):
    sweep = pl.program_id(1)
    j = pl.program_id(2)
    args = (j, nt, tt, seq_rows, gin_ref, wg_ref, bg_ref, ng_ref, s0_ref, o_ref, sout_ref,
            st_scr, g_scr, ob_scr, qin_scr, kin_scr, kend_scr, a_scr, stc_scr, dec_scr)

    @pl.when(sweep == 0)
    def _():
        _gla_sweep(False, *args)

    @pl.when(sweep == 1)
    def _():
        _gla_sweep(True, *args)


def _gla(gin, wg, bg, norm_g, s0, batch, seq, tt):
    nt = max(seq // tt, 1)
    group = max(tt // seq, 1)
    w = GLA_WIDTH
    per_batch_state = s0.shape[0] == batch
    s0_group = group if per_batch_state else 1
    tile_of = lambda s, j: j * (2 * s - 1) + (1 - s) * (nt - 1)
    return pl.pallas_call(
        functools.partial(_gla_kernel, nt=nt, tt=tt, seq_rows=min(seq, tt)),
        out_shape=[jax.ShapeDtypeStruct((batch * seq, w), F32),
                   jax.ShapeDtypeStruct((batch, 2, GLA_DK, w), F32)],
        grid=(batch // group, 2, nt),
        in_specs=[
            pl.BlockSpec((tt, GLA_IN), lambda b, s, j: (b * nt + tile_of(s, j), 0)),
            pl.BlockSpec((1, GZ_PAD, w), lambda b, s, j: (1 - s, 0, 0)),
            pl.BlockSpec((1, 1, w), lambda b, s, j: (1 - s, 0, 0)),
            _const_spec((1, w)),
            pl.BlockSpec((s0_group, 1, GLA_DK, w),
                         lambda b, s, j: (b if per_batch_state else 0, 1 - s, 0, 0)),
        ],
        out_specs=[
            pl.BlockSpec((tt, w), lambda b, s, j: (b * nt + j * s, 0)),
            pl.BlockSpec((group, 1, GLA_DK, w), lambda b, s, j: (b, 1 - s, 0, 0)),
        ],
        scratch_shapes=[pltpu.VMEM((GLA_DK, w), F32), pltpu.VMEM((tt, w), F32),
                        pltpu.VMEM((nt * tt, w), F32)]
        + [pltpu.VMEM((tt, w), BF16)] * 4
        + [pltpu.VMEM((tt // GLA_CHUNK, w, w), BF16),
           pltpu.VMEM((tt // GLA_CHUNK, 8, w), F32)],
        compiler_params=_params(3),
        name="gla",
    )(gin, wg, bg, norm_g.reshape(1, w), s0)


def _lru_sweep(fwd, j, nt, tt, seq_rows, xp_ref, xc_ref, xn_ref, lg_ref, cw_ref, cb_ref,
               wg_ref, bg_ref, lam_ref, h0_ref, o_ref, sout_ref,
               carry_scr, a_scr, ab_scr, hl_scr, pl_scr, h_scr, hb_scr, xcv_scr):
    w = LRU_WIDTH
    n_seg = LRU_SEGMENTS
    n_groups = n_seg // 8
    seg_len = tt // n_seg
    tile = j if fwd else nt - 1 - j
    whole_seqs = nt == 1
    segs_per_seq = seq_rows // seg_len if whole_seqs else n_seg

    if not whole_seqs:
        @pl.when(j == 0)
        def _():
            carry_scr[...] = jnp.broadcast_to(h0_ref[0, 0], (8, w))

    all_rows = pl.ds(pl.multiple_of(tile * tt, 8), tt)
    if fwd:
        xc = xcv_scr[all_rows, :]
    else:
        pre = jnp.where(tile > 0, xp_ref[...], 0.0)
        post = jnp.where(tile < nt - 1, xn_ref[...], 0.0)
        xe = jnp.concatenate([pre, xc_ref[...], post], axis=0)
        ne = tt + 16
        taps = [pltpu.roll(xe, 2, 0)[8:8 + tt], pltpu.roll(xe, 1, 0)[8:8 + tt], xe[8:8 + tt],
                pltpu.roll(xe, ne - 1, 0)[8:8 + tt]]
        if whole_seqs and seq_rows < tt:
            pos = lax.broadcasted_iota(jnp.int32, (tt, 1), 0) % seq_rows
            taps[0] = jnp.where(pos >= 2, taps[0], 0.0)
            taps[1] = jnp.where(pos >= 1, taps[1], 0.0)
            taps[3] = jnp.where(pos < seq_rows - 1, taps[3], 0.0)
        xc = cb_ref[...] + cw_ref[2:3, :] * taps[2]
        xc = xc + cw_ref[0:1, :] * taps[0]
        xc = xc + cw_ref[1:2, :] * taps[1]
        xc = xc + cw_ref[3:4, :] * taps[3]
        xcv_scr[all_rows, :] = xc

    ri = _dot(xc.astype(BF16), wg_ref[0]) + bg_ref[0]
    t_a = jnp.tanh(ri[:, 0:w])
    t_x = jnp.tanh(ri[:, w:2 * w])
    c2 = (-0.5 * LRU_C) * _softplus(-lam_ref[0])
    a = jnp.exp((t_a + 1.0) * c2)
    hx = 0.5 * xc
    bt = jnp.sqrt(1.0 - a * a) * (t_x * hx + hx)

    nl = w // LANE
    pitch = seg_len + LRU_SEG_PAD
    for lt in range(nl):
        for k in range(n_seg):
            dst = slice(k * pitch, k * pitch + seg_len)
            src = slice(k * seg_len, (k + 1) * seg_len)
            a_scr[lt, dst, :] = a[src, lt * LANE:(lt + 1) * LANE]
            ab_scr[lt, dst, :] = bt[src, lt * LANE:(lt + 1) * LANE]

    chains = [(lt, g) for lt in range(nl) for g in range(n_groups)]

    def step(ii, hp):
        i = ii if fwd else seg_len - 1 - ii
        rows = pl.ds(pl.multiple_of(i * 8, 8), 8)
        out = []
        for ch, (lt, g) in enumerate(chains):
            h, p = hp[ch]
            seg_rows = pl.ds(g * 8 * pitch + i, 8, stride=pitch)
            a_i = a_scr[lt, seg_rows, :]
            h = a_i * h + ab_scr[lt, seg_rows, :]
            p = a_i * p
            hl_scr[ch, rows, :] = h
            pl_scr[ch, rows, :] = p
            out.append((h, p))
        return tuple(out)

    init = tuple((jnp.zeros((8, LANE), F32), jnp.ones((8, LANE), F32)) for _ in chains)
    ends = lax.fori_loop(0, seg_len, step, init, unroll=8)
    h_end = jnp.concatenate([jnp.concatenate([ends[lt * n_groups + g][0] for lt in range(nl)],
                                             axis=1) for g in range(n_groups)], axis=0)
    p_end = jnp.concatenate([jnp.concatenate([ends[lt * n_groups + g][1] for lt in range(nl)],
                                             axis=1) for g in range(n_groups)], axis=0)

    c = None if whole_seqs else carry_scr[0:1, :]
    c_in = [None] * n_seg
    for n, k in enumerate(range(n_seg) if fwd else range(n_seg - 1, -1, -1)):
        seq_i = k // segs_per_seq
        if whole_seqs and n % segs_per_seq == 0:
            c = h0_ref[min(seq_i, h0_ref.shape[0] - 1), 0]
        c_in[k] = c
        c = h_end[k:k + 1, :] + p_end[k:k + 1, :] * c
        if whole_seqs and n % segs_per_seq == segs_per_seq - 1:
            sout_ref[seq_i, 0] = jnp.broadcast_to(c, (8, w))
    if not whole_seqs:
        carry_scr[...] = jnp.broadcast_to(c, (8, w))

    for k in range(n_seg):
        g, kk = divmod(k, 8)
        for i0 in range(0, seg_len, 8):
            src = pl.ds(i0 * 8 + kk, 8, stride=8)
            r0 = k * seg_len + i0
            for lt in range(nl):
                ch = lt * n_groups + g
                c_k = c_in[k][:, lt * LANE:(lt + 1) * LANE]
                h_scr[lt, r0:r0 + 8, :] = hl_scr[ch, src, :] + pl_scr[ch, src, :] * c_k
    h_all = jnp.concatenate([h_scr[lt] for lt in range(nl)], axis=1)
    if fwd:
        o_ref[...] = (h_all + hb_scr[all_rows, :]) * _gelu_tanh(lg_ref[...])
    else:
        hb_scr[all_rows, :] = h_all

    if not whole_seqs:
        @pl.when(j == nt - 1)
        def _():
            sout_ref[0, 0] = carry_scr[...]


def _lru_kernel(xp_ref, xc_ref, xn_ref, lg_ref, cw_ref, cb_ref, wg_ref, bg_ref,
                lam_ref, h0_ref, o_ref, sout_ref, carry_scr, a_scr, ab_scr, hl_scr, pl_scr,
                h_scr, hb_scr, xcv_scr, *, nt, tt, seq_rows):
    sweep = pl.program_id(1)
    j = pl.program_id(2)
    args = (j, nt, tt, seq_rows, xp_ref, xc_ref, xn_ref, lg_ref, cw_ref, cb_ref, wg_ref, bg_ref,
            lam_ref, h0_ref, o_ref, sout_ref, carry_scr, a_scr, ab_scr, hl_scr, pl_scr,
            h_scr, hb_scr, xcv_scr)

    @pl.when(sweep == 0)
    def _():
        _lru_sweep(False, *args)

    @pl.when(sweep == 1)
    def _():
        _lru_sweep(True, *args)


def _lru(lx, lg, conv_w, conv_b, wg, bg, lam, h0, batch, seq, tt):
    nt = max(seq // tt, 1)
    group = max(tt // seq, 1)
    w = LRU_WIDTH
    t8 = tt // 8
    n8 = nt * t8
    per_batch_state = h0.shape[0] == batch
    h0_group = group if per_batch_state else 1
    tile_of = lambda s, j: j * (2 * s - 1) + (1 - s) * (nt - 1)
    return pl.pallas_call(
        functools.partial(_lru_kernel, nt=nt, tt=tt, seq_rows=min(seq, tt)),
        out_shape=[jax.ShapeDtypeStruct((batch * seq, w), F32),
                   jax.ShapeDtypeStruct((batch, 2, 8, w), F32)],
        grid=(batch // group, 2, nt),
        in_specs=[
            pl.BlockSpec((8, w), lambda b, s, j:
                         (b * n8 + jnp.maximum(tile_of(s, j) * t8 - 1, 0), 0)),
            pl.BlockSpec((tt, w), lambda b, s, j: (b * nt + tile_of(s, j), 0)),
            pl.BlockSpec((8, w), lambda b, s, j:
                         (b * n8 + jnp.minimum((tile_of(s, j) + 1) * t8, n8 - 1), 0)),
            pl.BlockSpec((tt, w), lambda b, s, j: (b * nt + j * s, 0)),
            _const_spec((4, w)),
            _const_spec((1, w)),
            pl.BlockSpec((1, w, 2 * w), lambda b, s, j: (1 - s, 0, 0)),
            pl.BlockSpec((1, 1, 2 * w), lambda b, s, j: (1 - s, 0, 0)),
            pl.BlockSpec((1, 1, w), lambda b, s, j: (1 - s, 0, 0)),
            pl.BlockSpec((h0_group, 1, 1, w),
                         lambda b, s, j: (b if per_batch_state else 0, 1 - s, 0, 0)),
        ],
        out_specs=[
            pl.BlockSpec((tt, w), lambda b, s, j: (b * nt + j * s, 0)),
            pl.BlockSpec((group, 1, 8, w), lambda b, s, j: (b, 1 - s, 0, 0)),
        ],
        scratch_shapes=[pltpu.VMEM((8, w), F32)]
        + [pltpu.VMEM((w // LANE, tt + LRU_SEGMENTS * LRU_SEG_PAD, LANE), F32)] * 2
        + [pltpu.VMEM((w // LANE * (LRU_SEGMENTS // 8), 8 * tt // LRU_SEGMENTS, LANE), F32)] * 2
        + [pltpu.VMEM((w // LANE, tt, LANE), F32)]
        + [pltpu.VMEM((nt * tt, w), F32)] * 2,
        compiler_params=_params(3),
        name="lru",
    )(lx, lx, lx, lg, conv_w, conv_b.reshape(1, w), wg, bg, lam, h0)


def _mlp_kernel(x_ref, att_ref, gla_ref, lru_ref, mod_ref, n2_ref, fn_ref,
                wo_ref, w1_ref, w2_ref, o_ref, *, final, ff_chunk):
    d = D_MODEL
    mix = _dot(att_ref[...], wo_ref[0, 0:ATTN_WIDTH, :])
    mix = mix + _dot(gla_ref[...].astype(BF16), wo_ref[0, ATTN_WIDTH:ATTN_WIDTH + GLA_WIDTH, :])
    mix = mix + _dot(lru_ref[...].astype(BF16), wo_ref[0, ATTN_WIDTH + GLA_WIDTH:d, :])
    g1 = mod_ref[0, 0, :, 2 * d:3 * d]
    sh2 = mod_ref[0, 0, :, 3 * d:4 * d]
    sc2 = mod_ref[0, 0, :, 4 * d:5 * d]
    g2 = mod_ref[0, 0, :, 5 * d:6 * d]
    x = x_ref[...] + g1 * mix
    ms = jnp.mean(x * x, axis=-1, keepdims=True)
    h = (x * lax.rsqrt(ms + EPS) * n2_ref[...] * (1.0 + sc2) + sh2).astype(BF16)
    y = None
    for c in range(D_FF // ff_chunk):
        cols = slice(c * ff_chunk, (c + 1) * ff_chunk)
        u = jnp.maximum(_dot(h, w1_ref[0, :, cols]), 0.0)
        part = _dot((u * u).astype(BF16), w2_ref[0, cols, :])
        y = part if y is None else y + part
    x = x + g2 * y
    if final:
        ms = jnp.mean(x * x, axis=-1, keepdims=True)
        x = x * lax.rsqrt(ms + EPS) * fn_ref[...]
    o_ref[...] = x


def _out_mlp(x2d, att, gla, lru, mod, norm2, final_norm, wo, w1, w2, layer,
             rows_per_mod, mod_row0, tt, final):
    n, d = x2d.shape
    nt = n // tt
    mod_map = lambda i: (layer, mod_row0 + (i * tt) // rows_per_mod, 0, 0)
    row = lambda wd: pl.BlockSpec((tt, wd), lambda i: (i, 0))
    resident = lambda shape: pl.BlockSpec((1,) + shape, lambda i: (layer, 0, 0),
                                          pipeline_mode=pl.Buffered(1))
    return pl.pallas_call(
        functools.partial(_mlp_kernel, final=final, ff_chunk=512),
        out_shape=jax.ShapeDtypeStruct((n, d), F32),
        grid=(nt,),
        in_specs=[row(d), row(ATTN_WIDTH), row(GLA_WIDTH), row(LRU_WIDTH),
                  pl.BlockSpec((1, 1, 1, 6 * d), mod_map),
                  _const_spec((1, d)), _const_spec((1, d)),
                  resident((d, d)), resident((d, D_FF)), resident((D_FF, d))],
        out_specs=row(d),
        compiler_params=_params(1),
        name="out_mlp",
    )(x2d, att, gla, lru, mod, norm2.reshape(1, d), final_norm.reshape(1, d), wo, w1, w2)


def _prep_w_in(w_in):
    wz = w_in[..., C_A_END:C_B].astype(BF16)
    pad = jnp.zeros(w_in.shape[:-1] + (GZ_PAD - (C_B - C_A_END),), BF16)
    return (w_in[..., :C_A_END].astype(BF16), jnp.concatenate([wz, pad], axis=-1),
            w_in[..., C_B:].astype(BF16))


def _prep_gate_w(gw):
    out = []
    nz = 2 * GLA_GATE_RANK
    for dr in (0, 1):
        wf = jnp.zeros((nz, GLA_WIDTH), F32)
        wf = wf.at[dr * GLA_GATE_RANK:(dr + 1) * GLA_GATE_RANK].set(gw[dr])
        hi, lo = _split_bf16(wf)
        out.append(jnp.concatenate([hi, hi, lo, jnp.zeros((GZ_PAD - 3 * nz, GLA_WIDTH), BF16)],
                                   axis=0))
    return jnp.stack(out)


def _block_diag(w4):
    n, a, b = w4.shape
    eye = jnp.eye(n, dtype=w4.dtype)
    return jnp.einsum("nab,nm->namb", w4, eye).reshape(n * a, n * b)


def _rope_tables(seq):
    rows = seq // GRID_W
    nf = HEAD_DIM // 4
    inv = ROPE_BASE ** (-jnp.arange(nf, dtype=F32) / nf)
    ang_r = jnp.arange(rows, dtype=F32)[:, None] * inv
    ang_c = jnp.arange(GRID_W, dtype=F32)[:, None] * inv
    by_row = lambda t: jnp.repeat(t, GRID_W, axis=0)
    by_col = lambda t: jnp.tile(t, (rows, 1))
    cos_r, sin_r = by_row(jnp.cos(ang_r)), by_row(jnp.sin(ang_r))
    cos_c, sin_c = by_col(jnp.cos(ang_c)), by_col(jnp.sin(ang_c))
    cos = jnp.concatenate([cos_r, cos_r, cos_c, cos_c], axis=1)
    sin = jnp.concatenate([-sin_r, sin_r, -sin_c, sin_c], axis=1)
    return jnp.concatenate([cos] * 2, axis=1), jnp.concatenate([sin] * 2, axis=1)


def _tiles(latent, seq):
    if latent:
        return 1024, 2 * ATTN_SUB, 1024, 1024, 1024
    return 1024, 2 * seq, 4 * seq, 4 * seq, 1024


def _gla_state_to_internal(s):
    b = s.shape[0]
    return jnp.transpose(s, (0, 1, 4, 2, 3)).reshape(b, 2, GLA_DK, GLA_WIDTH)


def kernel(x_prompt, x_sample, c, cache_k, cache_v, state_gla, state_lru, c_ctx, w_mod, b_mod, norm1, norm2, w_in, attn_sink, gla_gate_w, gla_gate_b, gla_norm, lru_conv_w, lru_conv_b, lru_wa, lru_ba, lru_wx, lru_bx, lru_lambda, w_out, w_mlp1, w_mlp2, final_norm):
    depth = w_in.shape[0]
    bc, sc_len, d = x_prompt.shape
    bl, sl_len, _ = x_sample.shape
    past = cache_k.shape[2]

    cond = jnp.concatenate([c_ctx[None], c, jnp.zeros((MOD_ROWS - 1 - bl, d), F32)], axis=0)
    mod = _ada_mod(cond, w_mod, b_mod)

    rope_tabs = _rope_tables(sl_len)
    ck = cache_k.reshape(bl, depth, past, KV_WIDTH)
    cv = cache_v.reshape(bl, depth, past, KV_WIDTH)
    gla_s0_ctx = jnp.zeros((1, 2, GLA_DK, GLA_WIDTH), F32)
    lru_s0_ctx = jnp.zeros((1, 2, 1, LRU_WIDTH), F32)

    xp = x_prompt.reshape(bc * sc_len, d)
    xs = x_sample.reshape(bl * sl_len, d)
    w_in_b = _prep_w_in(w_in)
    wo = w_out.astype(BF16)
    w1 = w_mlp1.astype(BF16)
    w2 = w_mlp2.astype(BF16)
    ks, vs, sgs, sls = [], [], [], []
    for l in range(depth):
        gate_w = _prep_gate_w(gla_gate_w[l])
        gate_b = gla_gate_b[l].reshape(2, 1, GLA_WIDTH)
        lru_wg = (0.5 * jnp.stack([jnp.concatenate(
            [_block_diag(lru_wa[l, dr]), _block_diag(lru_wx[l, dr])], axis=1)
            for dr in (0, 1)])).astype(BF16)
        lru_bg = (0.5 * jnp.stack([jnp.concatenate([lru_ba[l, dr], lru_bx[l, dr]])
                                   for dr in (0, 1)])).reshape(2, 1, 2 * LRU_WIDTH)
        lam = lru_lambda[l].reshape(2, 1, LRU_WIDTH)
        final = l == depth - 1

        for is_lat in (False, True):
            if is_lat:
                x2d, batch, seq, row0, rpm, tabs = xs, bl, sl_len, 1, sl_len, rope_tabs
                gla_s0 = _gla_state_to_internal(state_gla[:, l])
                lru_s0 = state_lru[:, l].reshape(bl, 2, 1, LRU_WIDTH)
            else:
                x2d, batch, seq, row0, rpm, tabs = xp, bc, sc_len, 0, bc * sc_len, None
                gla_s0, lru_s0 = gla_s0_ctx, lru_s0_ctx
            tt_proj, tq, tt_gla, tt_lru, tt_mlp = _tiles(is_lat, seq)
            q, k, v, gin, lx, lg = _in_proj(x2d, mod, norm1[l], w_in_b, l, tabs, rpm, row0,
                                            tt_proj)
            att = _attention(q, k, v, attn_sink[l], batch, seq, tq, ATTN_SUB,
                             ctx_kv=(ck, cv, l) if is_lat else None)
            gla, sg = _gla(gin, gate_w, gate_b, gla_norm[l], gla_s0, batch, seq, tt_gla)
            lru, sl = _lru(lx, lg, lru_conv_w[l], lru_conv_b[l], lru_wg, lru_bg, lam, lru_s0,
                           batch, seq, tt_lru)
            x2d = _out_mlp(x2d, att, gla, lru, mod, norm2[l], final_norm, wo, w1, w2, l,
                           rpm, row0, tt_mlp, final)
            if is_lat:
                xs = x2d
            else:
                xp = x2d
                ks.append(k.reshape(bc, sc_len, KV_WIDTH))
                vs.append(v.reshape(bc, sc_len, KV_WIDTH))
                sgs.append(sg)
                sls.append(sl[:, :, 0, :])

    y_prompt = xp.reshape(bc, sc_len, d)
    y_sample = xs.reshape(bl, sl_len, d)
    kv_shape = (bc, depth, sc_len, KV_WIDTH // HEAD_DIM, HEAD_DIM)
    new_k = jnp.stack(ks, axis=1).reshape(kv_shape)
    new_v = jnp.stack(vs, axis=1).reshape(kv_shape)
    sg_all = jnp.stack(sgs, axis=1)
    new_sg = jnp.transpose(
        sg_all.reshape(bc, depth, 2, GLA_DK, GLA_HEADS, GLA_DK), (0, 1, 2, 4, 5, 3))
    new_sl = jnp.stack(sls, axis=1)
    return (y_prompt, y_sample, new_k, new_v, new_sg, new_sl)
```

```python
import functools

import jax
import jax.numpy as jnp
from jax import lax
from jax.experimental import pallas as pl
from jax.experimental.pallas import tpu as pltpu

F32 = jnp.float32
BF16 = jnp.bfloat16
LANE = 128

D_MODEL = 1024
GRID_W = 64
EPS = 1e-6
HEAD_DIM = 64
ATTN_WIDTH = 512
ATTN_HEADS = 8
KV_WIDTH = 128
WINDOW = 128
ATTN_LOOKAHEAD = 1
ATTN_SUB = 256
ROPE_BASE = 10000.0
NEG_INF = -1e30
GLA_WIDTH = 256
GLA_DK = 64
GLA_HEADS = 4
GLA_GATE_RANK = 16
GLA_GATE_NORM = 16.0
GLA_CHUNK = 64
GLA_SKEW = 2
LRU_WIDTH = 256
LRU_C = 8.0
LRU_SEG_PAD = 4
LRU_SEGMENTS = 16
D_FF = 4096
MOD_ROWS = 8
GZ_PAD = 128
C_Q, C_K, C_V, C_GLA, C_A_END, C_B = 0, 512, 640, 768, 1536, 1568
GLA_IN = 4 * GLA_WIDTH + GZ_PAD
VMEM_LIMIT = 56 * 1024 * 1024

NT_DIMS = (((1,), (1,)), ((), ()))


def _split_bf16(a):
    hi = a.astype(BF16)
    lo = (a - hi.astype(F32)).astype(BF16)
    return hi, lo


def _dot(a, b):
    return jnp.dot(a, b, preferred_element_type=F32)


def _dot_nt(a, b):
    return lax.dot_general(a, b, NT_DIMS, preferred_element_type=F32)


def _dot_x3(a, b):
    ah, al = _split_bf16(a)
    bh, bl = _split_bf16(b)
    return _dot(ah, bh) + (_dot(ah, bl) + _dot(al, bh))


def _softplus(y):
    return jnp.maximum(y, 0.0) + jnp.log1p(jnp.exp(-jnp.abs(y)))


def _log_sigmoid(y):
    return jnp.minimum(y, 0.0) - jnp.log(1.0 + jnp.exp(-jnp.abs(y)))


def _sigmoid(y):
    return 0.5 * jnp.tanh(0.5 * y) + 0.5


def _silu(y):
    return y * _sigmoid(y)


def _gelu_tanh(y):
    c = 0.7978845608028654
    hy = 0.5 * y
    return hy * jnp.tanh(y * (c + (c * 0.044715) * (y * y))) + hy


def _params(n_axes):
    return pltpu.CompilerParams(
        dimension_semantics=("arbitrary",) * n_axes, vmem_limit_bytes=VMEM_LIMIT)


def _const_spec(shape):
    nd = len(shape)
    return pl.BlockSpec(shape, lambda *_: (0,) * nd)


def _ada_kernel(c_ref, w_ref, b_ref, o_ref):
    s = _silu(c_ref[...])
    res = _dot_x3(s, w_ref[0]) + b_ref[0]
    for r in range(MOD_ROWS):
        o_ref[0, r] = res[r:r + 1, :]


def _ada_mod(cond, w_mod, b_mod):
    depth, d, n = w_mod.shape
    tn = 1536
    return pl.pallas_call(
        _ada_kernel,
        out_shape=jax.ShapeDtypeStruct((depth, MOD_ROWS, 1, n), F32),
        grid=(depth, n // tn),
        in_specs=[
            pl.BlockSpec((MOD_ROWS, d), lambda l, j: (0, 0)),
            pl.BlockSpec((1, d, tn), lambda l, j: (l, 0, j)),
            pl.BlockSpec((1, 1, tn), lambda l, j: (l, 0, j)),
        ],
        out_specs=pl.BlockSpec((1, MOD_ROWS, 1, tn), lambda l, j: (l, 0, 0, j)),
        compiler_params=_params(2),
        name="ada_mod",
    )(cond, w_mod, b_mod.reshape(depth, 1, n))


def _rope(x, cos, sin_signed):
    w = x.shape[1]
    lane = lax.broadcasted_iota(jnp.int32, (1, w), 1)
    first = (lane % 32) < 16
    swapped = jnp.where(first, pltpu.roll(x, w - 16, 1), pltpu.roll(x, 16, 1))
    return x * cos + swapped * sin_signed


def _half_variants(a):
    lane = lax.broadcasted_iota(jnp.int32, (1, LANE), 1)
    low = lane < HEAD_DIM
    ar = pltpu.roll(a, HEAD_DIM, 1)
    z = jnp.zeros_like(a)
    parts = [jnp.where(low, a, z), jnp.where(low, z, ar),
             jnp.where(low, ar, z), jnp.where(low, z, a)]
    return jnp.concatenate([t.astype(BF16) for t in parts], axis=1)


def _inproj_kernel(*refs, rope):
    if rope:
        (x_ref, mod_ref, n1_ref, wa_ref, wz_ref, wb_ref, cos_ref, sin_ref,
         q_ref, k_ref, v_ref, gla_ref, lx_ref, lg_ref) = refs
    else:
        (x_ref, mod_ref, n1_ref, wa_ref, wz_ref, wb_ref,
         q_ref, k_ref, v_ref, gla_ref, lx_ref, lg_ref) = refs
    d = D_MODEL
    x = x_ref[...]
    ms = jnp.mean(x * x, axis=-1, keepdims=True)
    sh = mod_ref[0, 0, :, 0:d]
    sc = mod_ref[0, 0, :, d:2 * d]
    gain = n1_ref[...] * (1.0 + sc)
    h = (x * lax.rsqrt(ms + EPS) * gain + sh).astype(BF16)
    p_att = _dot(h, wa_ref[0, :, C_Q:C_GLA])
    p_gla = _dot(h, wa_ref[0, :, C_GLA:C_A_END])
    q = p_att[:, C_Q:C_K]
    k = p_att[:, C_K:C_V]
    v = p_att[:, C_V:C_GLA]
    if rope:
        cos = cos_ref[...]
        sin = sin_ref[...]
        k = _rope(k, cos, sin)
        q = _rope(q, jnp.concatenate([cos] * 4, axis=1), jnp.concatenate([sin] * 4, axis=1))
    q_ref[...] = (q * (HEAD_DIM ** -0.5)).astype(BF16)
    k_ref[...] = k
    v_ref[...] = v
    w = GLA_WIDTH
    pb = _dot(h, wb_ref[0])
    gla_ref[:, 0:3 * w] = p_gla
    gla_ref[:, 3 * w:4 * w] = pb[:, 0:w]
    gla_ref[:, 4 * w:4 * w + GZ_PAD] = _dot(h, wz_ref[0])
    lx_ref[...] = pb[:, w:w + LRU_WIDTH]
    lg_ref[...] = pb[:, w + LRU_WIDTH:w + 2 * LRU_WIDTH]


def _in_proj(x2d, mod, norm1, w_abz, layer, rope_tabs, rows_per_mod, mod_row0, tt):
    n, d = x2d.shape
    nt = n // tt
    rope = rope_tabs is not None
    mod_map = lambda i: (layer, mod_row0 + (i * tt) // rows_per_mod, 0, 0)
    by_layer = lambda arr: pl.BlockSpec((1,) + arr.shape[1:], lambda i: (layer, 0, 0))
    in_specs = [
        pl.BlockSpec((tt, d), lambda i: (i, 0)),
        pl.BlockSpec((1, 1, 1, 6 * d), mod_map),
        _const_spec((1, d)),
    ] + [by_layer(arr) for arr in w_abz]
    args = [x2d, mod, norm1.reshape(1, d)] + list(w_abz)
    if rope:
        t = rope_tabs[0].shape[0]
        tpb = t // tt
        in_specs += [pl.BlockSpec((tt, LANE), lambda i: (i % tpb, 0))] * 2
        args += list(rope_tabs)
    outs = [(ATTN_WIDTH, BF16), (KV_WIDTH, F32), (KV_WIDTH, F32),
            (GLA_IN, F32), (LRU_WIDTH, F32), (LRU_WIDTH, F32)]
    widths = [o[0] for o in outs]
    dtypes = [o[1] for o in outs]
    return pl.pallas_call(
        functools.partial(_inproj_kernel, rope=rope),
        out_shape=[jax.ShapeDtypeStruct((n, wd), dt) for wd, dt in zip(widths, dtypes)],
        grid=(nt,),
        in_specs=in_specs,
        out_specs=[pl.BlockSpec((tt, wd), lambda i: (i, 0)) for wd in widths],
        compiler_params=_params(1),
        name="in_proj_lat" if rope else "in_proj_ctx",
    )(*args)


def _attn_kernel(*refs, latent, tq, sub, seq):
    n_units = tq // sub
    if latent:
        (sink_ref, q_ref, band_ref, kp_ref, kc_ref, kn_ref, vp_ref, vc_ref, vn_ref,
         ck_ref, cv_ref, o_ref) = refs
        i = pl.program_id(1)
        k_loc = _half_variants(jnp.concatenate([kp_ref[...], kc_ref[...], kn_ref[...]], axis=0))
        v_loc = _half_variants(jnp.concatenate([vp_ref[...], vc_ref[...], vn_ref[...]], axis=0))
        nk = sub + 2 * WINDOW
        k_ctx = _half_variants(ck_ref[0, 0])
        v_ctx = _half_variants(cv_ref[0, 0])
        key_rows = [slice(u * sub, u * sub + nk) for u in range(n_units)]
        biases = []
        for u in range(n_units):
            kpos = i * tq + u * sub - WINDOW + lax.broadcasted_iota(jnp.int32, (1, nk), 1)
            biases.append(band_ref[...] + jnp.where((kpos >= 0) & (kpos < seq), 0.0, NEG_INF))
    else:
        sink_ref, q_ref, kc_ref, vc_ref, o_ref = refs
        k_loc = _half_variants(kc_ref[...])
        v_loc = _half_variants(vc_ref[...])
        key_rows = [slice(u * sub, (u + 1) * sub) for u in range(n_units)]
        biases = [None] * n_units
    block = lambda t, kv, half: t[:, (2 * kv + half) * LANE:(2 * kv + half + 1) * LANE]
    with_ones = lambda t: [jnp.concatenate([block(t, kv, half), jnp.ones((t.shape[0], LANE), BF16)],
                                           axis=1) for kv in range(2) for half in range(2)]
    v_loc_ones = with_ones(v_loc)
    v_ctx_ones = with_ones(v_ctx) if latent else None

    def scores(item):
        u, head = divmod(item, ATTN_HEADS)
        c, half = divmod(head, 2)
        qc = q_ref[u * sub:(u + 1) * sub, c * LANE:(c + 1) * LANE]
        s = _dot_nt(qc, block(k_loc[key_rows[u]], c // 2, half))
        if biases[u] is not None:
            s = s + biases[u]
        return s, (_dot_nt(qc, block(k_ctx, c // 2, half)) if latent else None)

    n_items = n_units * ATTN_HEADS
    ready = [scores(it) for it in range(ATTN_LOOKAHEAD)]
    acc = None
    for item in range(n_items):
        u, head = divmod(item, ATTN_HEADS)
        c, half = divmod(head, 2)
        s, s_c = ready.pop(0)
        if item + ATTN_LOOKAHEAD < n_items:
            ready.append(scores(item + ATTN_LOOKAHEAD))
        sink = sink_ref[head]
        m = jnp.maximum(jnp.max(s, axis=-1, keepdims=True), sink)
        if latent:
            m = jnp.maximum(m, jnp.max(s_c, axis=-1, keepdims=True))
        variant = 2 * (c // 2) + half
        o = _dot(jnp.exp(s - m).astype(BF16), v_loc_ones[variant][key_rows[u]])
        if latent:
            o = o + _dot(jnp.exp(s_c - m).astype(BF16), v_ctx_ones[variant])
        denom = jnp.exp(sink - m) + o[:, LANE:2 * LANE]
        o = o[:, 0:LANE] * (1.0 / denom)
        if half == 0:
            acc = o
        else:
            o_ref[u * sub:(u + 1) * sub, c * LANE:(c + 1) * LANE] = (acc + o).astype(BF16)


def _attention(q, k, v, sink, batch, seq, tq, sub, ctx_kv=None):
    latent = ctx_kv is not None
    nq = max(seq // tq, 1)
    group = max(tq // seq, 1)
    kern = functools.partial(_attn_kernel, latent=latent, tq=tq, sub=sub, seq=seq)
    smem = pl.BlockSpec(memory_space=pltpu.SMEM)
    kvw = KV_WIDTH
    if latent:
        ck, cv, layer = ctx_kv
        past = ck.shape[2]
        wb = tq // WINDOW
        nwb = seq // WINDOW
        nk = sub + 2 * WINDOW
        r = lax.broadcasted_iota(jnp.int32, (sub, nk), 0)
        j = lax.broadcasted_iota(jnp.int32, (sub, nk), 1)
        band = jnp.where((j >= r) & (j <= r + 2 * WINDOW), 0.0, NEG_INF).astype(F32)
        cur = pl.BlockSpec((tq, kvw), lambda b, i: (b * nq + i, 0))
        prev = pl.BlockSpec((WINDOW, kvw),
                            lambda b, i: (b * nwb + jnp.maximum(i * wb - 1, 0), 0))
        nxt = pl.BlockSpec((WINDOW, kvw),
                           lambda b, i: (b * nwb + jnp.minimum((i + 1) * wb, nwb - 1), 0))
        cspec = pl.BlockSpec((1, 1, past, kvw), lambda b, i: (b, layer, 0, 0))
        in_specs = [smem, pl.BlockSpec((tq, ATTN_WIDTH), lambda b, i: (b * nq + i, 0)),
                    _const_spec((sub, nk)), prev, cur, nxt, prev, cur, nxt, cspec, cspec]
        args = (sink, q, band, k, k, k, v, v, v, ck, cv)
    else:
        cur = pl.BlockSpec((tq, kvw), lambda b, i: (b * nq + i, 0))
        in_specs = [smem, pl.BlockSpec((tq, ATTN_WIDTH), lambda b, i: (b * nq + i, 0)), cur, cur]
        args = (sink, q, k, v)
    return pl.pallas_call(
        kern,
        out_shape=jax.ShapeDtypeStruct((batch * seq, ATTN_WIDTH), BF16),
        grid=(batch // group, nq),
        in_specs=in_specs,
        out_specs=pl.BlockSpec((tq, ATTN_WIDTH), lambda b, i: (b * nq + i, 0)),
        compiler_params=_params(2),
        name="attn_lat" if latent else "attn_ctx",
    )(*args)


def _gla_sweep(fwd, j, nt, tt, seq_rows, gin_ref, wg_ref, bg_ref, ng_ref, s0_ref, o_ref, sout_ref,
               st_scr, g_scr, ob_scr, qin_scr, kin_scr, kend_scr, a_scr, stc_scr, dec_scr):
    w = GLA_WIDTH
    ck = GLA_CHUNK
    nc = tt // ck
    cps = seq_rows // ck
    whole_seqs = nt == 1
    tile = j if fwd else nt - 1 - j
    starts = [(ci if fwd else nc - 1 - ci) * ck for ci in range(nc)]

    rr = lax.broadcasted_iota(jnp.int32, (w, w), 0) // GLA_DK
    cc = lax.broadcasted_iota(jnp.int32, (w, w), 1) // GLA_DK
    block_diag = rr == cc
    bd_bf = block_diag.astype(F32).astype(BF16)
    tr = lax.broadcasted_iota(jnp.int32, (ck, 2 * ck), 0)
    tc = lax.broadcasted_iota(jnp.int32, (ck, 2 * ck), 1) % ck
    tri2 = ((tr >= tc) if fwd else (tr <= tc)).astype(F32).astype(BF16)
    ar = lax.broadcasted_iota(jnp.int32, (ck, w), 0)
    ac = lax.broadcasted_iota(jnp.int32, (ck, w), 1) % ck
    causal = (ar >= ac) if fwd else (ar <= ac)

    if not whole_seqs:
        @pl.when(j == 0)
        def _():
            st_scr[...] = s0_ref[0, 0]

    zpad = gin_ref[:, 4 * w:4 * w + GZ_PAD]
    zh = zpad.astype(BF16).astype(F32)
    zcat = zh + pltpu.roll(zpad - zh, 32, 1) + pltpu.roll(zh, 64, 1)
    logit = _dot(zcat.astype(BF16), wg_ref[0]) + bg_ref[0]
    g_scr[...] = _log_sigmoid(logit) * (1.0 / GLA_GATE_NORM)

    tile_rows = pl.multiple_of(tile * tt, ck)

    def decayed(ci, c0):
        rows = pl.ds(c0, ck)
        g = g_scr[rows, :]
        gh, gl = _split_bf16(g)
        b = _dot(tri2, jnp.concatenate([gh, gl], axis=0))
        btot = jnp.sum(g, axis=0, keepdims=True)
        k = gin_ref[rows, w:2 * w]
        qin_scr[rows, :] = (gin_ref[rows, 0:w] * (GLA_DK ** -0.5) * jnp.exp(b)).astype(BF16)
        kin_scr[rows, :] = (k * jnp.exp(-b)).astype(BF16)
        kend_scr[rows, :] = (k * jnp.exp(btot - b)).astype(BF16)
        dec_scr[ci] = jnp.broadcast_to(jnp.exp(btot), (8, w))

    def scores(ci, c0):
        rows = pl.ds(c0, ck)
        k_bd = jnp.concatenate([kin_scr[rows, :]] * GLA_HEADS, axis=0) * bd_bf
        a_scr[rows, :] = jnp.where(causal, _dot_nt(qin_scr[rows, :], k_bd), 0.0).astype(BF16)

    def state(ci, c0):
        rows = pl.ds(c0, ck)
        seq_i = c0 // seq_rows
        first = whole_seqs and ci % cps == 0
        st = s0_ref[min(seq_i, s0_ref.shape[0] - 1), 0] if first else st_scr[...]
        stc_scr[ci] = jnp.concatenate([st.astype(BF16)] * GLA_HEADS, axis=0) * bd_bf
        vt = gin_ref[rows, 2 * w:3 * w].T
        vt_heads = jnp.concatenate([vt[h * GLA_DK:(h + 1) * GLA_DK, :] for h in range(GLA_HEADS)],
                                   axis=1).astype(BF16)
        kend_bd = jnp.concatenate([kend_scr[rows, :]] * GLA_HEADS, axis=0) * bd_bf
        st = st * dec_scr[ci][0:1, :] + _dot(vt_heads, kend_bd)
        st_scr[...] = st
        if whole_seqs and ci % cps == cps - 1:
            sout_ref[seq_i, 0] = st

    def output(ci, c0):
        rows = pl.ds(c0, ck)
        v_bd = jnp.concatenate([gin_ref[rows, 2 * w:3 * w].astype(BF16)] * GLA_HEADS,
                               axis=0) * bd_bf
        o = _dot(a_scr[rows, :], v_bd) + _dot_nt(qin_scr[rows, :], stc_scr[ci])
        if fwd:
            o_ref[rows, :] = o
        else:
            ob_scr[pl.ds(tile_rows + c0, ck), :] = o

    stages = (decayed, scores, state, output)
    for step in range(nc + GLA_SKEW * (len(stages) - 1)):
        for si, stage in enumerate(stages):
            ci = step - GLA_SKEW * si
            if 0 <= ci < nc:
                stage(ci, starts[ci])

    if fwd:
        o = o_ref[...] + ob_scr[pl.ds(tile_rows, tt), :]
        ms = _dot((o * o).astype(BF16), bd_bf) * (1.0 / GLA_DK)
        r = gin_ref[:, 3 * w:4 * w]
        o_ref[...] = o * lax.rsqrt(ms + EPS) * ng_ref[...] * _silu(r)

    if not whole_seqs:
        @pl.when(j == nt - 1)
        def _():
            sout_ref[0, 0] = st_scr[...]


def _gla_kernel(gin_ref, wg_ref, bg_ref, ng_ref, s0_ref, o_ref, sout_ref,
                st_scr, g_scr, ob_scr, qin_scr, kin_scr, kend_scr, a_scr, stc_scr, dec_scr,
                *, nt, tt, seq_rows):
    sweep = pl.program_id(1)
    j = pl.program_id(2)
    args = (j, nt, tt, seq_rows, gin_ref, wg_ref, bg_ref, ng_ref, s0_ref, o_ref, sout_ref,
            st_scr, g_scr, ob_scr, qin_scr, kin_scr, kend_scr, a_scr, stc_scr, dec_scr)

    @pl.when(sweep == 0)
    def _():
        _gla_sweep(False, *args)

    @pl.when(sweep == 1)
    def _():
        _gla_sweep(True, *args)


def _gla(gin, wg, bg, norm_g, s0, batch, seq, tt):
    nt = max(seq // tt, 1)
    group = max(tt // seq, 1)
    w = GLA_WIDTH
    per_batch_state = s0.shape[0] == batch
    s0_group = group if per_batch_state else 1
    tile_of = lambda s, j: j * (2 * s - 1) + (1 - s) * (nt - 1)
    return pl.pallas_call(
        functools.partial(_gla_kernel, nt=nt, tt=tt, seq_rows=min(seq, tt)),
        out_shape=[jax.ShapeDtypeStruct((batch * seq, w), F32),
                   jax.ShapeDtypeStruct((batch, 2, GLA_DK, w), F32)],
        grid=(batch // group, 2, nt),
        in_specs=[
            pl.BlockSpec((tt, GLA_IN), lambda b, s, j: (b * nt + tile_of(s, j), 0)),
            pl.BlockSpec((1, GZ_PAD, w), lambda b, s, j: (1 - s, 0, 0)),
            pl.BlockSpec((1, 1, w), lambda b, s, j: (1 - s, 0, 0)),
            _const_spec((1, w)),
            pl.BlockSpec((s0_group, 1, GLA_DK, w),
                         lambda b, s, j: (b if per_batch_state else 0, 1 - s, 0, 0)),
        ],
        out_specs=[
            pl.BlockSpec((tt, w), lambda b, s, j: (b * nt + j * s, 0)),
            pl.BlockSpec((group, 1, GLA_DK, w), lambda b, s, j: (b, 1 - s, 0, 0)),
        ],
        scratch_shapes=[pltpu.VMEM((GLA_DK, w), F32), pltpu.VMEM((tt, w), F32),
                        pltpu.VMEM((nt * tt, w), F32)]
        + [pltpu.VMEM((tt, w), BF16)] * 4
        + [pltpu.VMEM((tt // GLA_CHUNK, w, w), BF16),
           pltpu.VMEM((tt // GLA_CHUNK, 8, w), F32)],
        compiler_params=_params(3),
        name="gla",
    )(gin, wg, bg, norm_g.reshape(1, w), s0)


def _lru_sweep(fwd, j, nt, tt, seq_rows, xp_ref, xc_ref, xn_ref, lg_ref, cw_ref, cb_ref,
               wg_ref, bg_ref, lam_ref, h0_ref, o_ref, sout_ref,
               carry_scr, a_scr, ab_scr, hl_scr, pl_scr, h_scr, hb_scr, xcv_scr):
    w = LRU_WIDTH
    n_seg = LRU_SEGMENTS
    n_groups = n_seg // 8
    seg_len = tt // n_seg
    tile = j if fwd else nt - 1 - j
    whole_seqs = nt == 1
    segs_per_seq = seq_rows // seg_len if whole_seqs else n_seg

    if not whole_seqs:
        @pl.when(j == 0)
        def _():
            carry_scr[...] = jnp.broadcast_to(h0_ref[0, 0], (8, w))

    all_rows = pl.ds(pl.multiple_of(tile * tt, 8), tt)
    if fwd:
        xc = xcv_scr[all_rows, :]
    else:
        pre = jnp.where(tile > 0, xp_ref[...], 0.0)
        post = jnp.where(tile < nt - 1, xn_ref[...], 0.0)
        xe = jnp.concatenate([pre, xc_ref[...], post], axis=0)
        ne = tt + 16
        taps = [pltpu.roll(xe, 2, 0)[8:8 + tt], pltpu.roll(xe, 1, 0)[8:8 + tt], xe[8:8 + tt],
                pltpu.roll(xe, ne - 1, 0)[8:8 + tt]]
        if whole_seqs and seq_rows < tt:
            pos = lax.broadcasted_iota(jnp.int32, (tt, 1), 0) % seq_rows
            taps[0] = jnp.where(pos >= 2, taps[0], 0.0)
            taps[1] = jnp.where(pos >= 1, taps[1], 0.0)
            taps[3] = jnp.where(pos < seq_rows - 1, taps[3], 0.0)
        xc = cb_ref[...] + cw_ref[2:3, :] * taps[2]
        xc = xc + cw_ref[0:1, :] * taps[0]
        xc = xc + cw_ref[1:2, :] * taps[1]
        xc = xc + cw_ref[3:4, :] * taps[3]
        xcv_scr[all_rows, :] = xc

    ri = _dot(xc.astype(BF16), wg_ref[0]) + bg_ref[0]
    t_a = jnp.tanh(ri[:, 0:w])
    t_x = jnp.tanh(ri[:, w:2 * w])
    c2 = (-0.5 * LRU_C) * _softplus(-lam_ref[0])
    a = jnp.exp((t_a + 1.0) * c2)
    hx = 0.5 * xc
    bt = jnp.sqrt(1.0 - a * a) * (t_x * hx + hx)

    nl = w // LANE
    pitch = seg_len + LRU_SEG_PAD
    for lt in range(nl):
        for k in range(n_seg):
            dst = slice(k * pitch, k * pitch + seg_len)
            src = slice(k * seg_len, (k + 1) * seg_len)
            a_scr[lt, dst, :] = a[src, lt * LANE:(lt + 1) * LANE]
            ab_scr[lt, dst, :] = bt[src, lt * LANE:(lt + 1) * LANE]

    chains = [(lt, g) for lt in range(nl) for g in range(n_groups)]

    def step(ii, hp):
        i = ii if fwd else seg_len - 1 - ii
        rows = pl.ds(pl.multiple_of(i * 8, 8), 8)
        out = []
        for ch, (lt, g) in enumerate(chains):
            h, p = hp[ch]
            seg_rows = pl.ds(g * 8 * pitch + i, 8, stride=pitch)
            a_i = a_scr[lt, seg_rows, :]
            h = a_i * h + ab_scr[lt, seg_rows, :]
            p = a_i * p
            hl_scr[ch, rows, :] = h
            pl_scr[ch, rows, :] = p
            out.append((h, p))
        return tuple(out)

    init = tuple((jnp.zeros((8, LANE), F32), jnp.ones((8, LANE), F32)) for _ in chains)
    ends = lax.fori_loop(0, seg_len, step, init, unroll=8)
    h_end = jnp.concatenate([jnp.concatenate([ends[lt * n_groups + g][0] for lt in range(nl)],
                                             axis=1) for g in range(n_groups)], axis=0)
    p_end = jnp.concatenate([jnp.concatenate([ends[lt * n_groups + g][1] for lt in range(nl)],
                                             axis=1) for g in range(n_groups)], axis=0)

    c = None if whole_seqs else carry_scr[0:1, :]
    c_in = [None] * n_seg
    for n, k in enumerate(range(n_seg) if fwd else range(n_seg - 1, -1, -1)):
        seq_i = k // segs_per_seq
        if whole_seqs and n % segs_per_seq == 0:
            c = h0_ref[min(seq_i, h0_ref.shape[0] - 1), 0]
        c_in[k] = c
        c = h_end[k:k + 1, :] + p_end[k:k + 1, :] * c
        if whole_seqs and n % segs_per_seq == segs_per_seq - 1:
            sout_ref[seq_i, 0] = jnp.broadcast_to(c, (8, w))
    if not whole_seqs:
        carry_scr[...] = jnp.broadcast_to(c, (8, w))

    for k in range(n_seg):
        g, kk = divmod(k, 8)
        for i0 in range(0, seg_len, 8):
            src = pl.ds(i0 * 8 + kk, 8, stride=8)
            r0 = k * seg_len + i0
            for lt in range(nl):
                ch = lt * n_groups + g
                c_k = c_in[k][:, lt * LANE:(lt + 1) * LANE]
                h_scr[lt, r0:r0 + 8, :] = hl_scr[ch, src, :] + pl_scr[ch, src, :] * c_k
    h_all = jnp.concatenate([h_scr[lt] for lt in range(nl)], axis=1)
    if fwd:
        o_ref[...] = (h_all + hb_scr[all_rows, :]) * _gelu_tanh(lg_ref[...])
    else:
        hb_scr[all_rows, :] = h_all

    if not whole_seqs:
        @pl.when(j == nt - 1)
        def _():
            sout_ref[0, 0] = carry_scr[...]


def _lru_kernel(xp_ref, xc_ref, xn_ref, lg_ref, cw_ref, cb_ref, wg_ref, bg_ref,
                lam_ref, h0_ref, o_ref, sout_ref, carry_scr, a_scr, ab_scr, hl_scr, pl_scr,
                h_scr, hb_scr, xcv_scr, *, nt, tt, seq_rows):
    sweep = pl.program_id(1)
    j = pl.program_id(2)
    args = (j, nt, tt, seq_rows, xp_ref, xc_ref, xn_ref, lg_ref, cw_ref, cb_ref, wg_ref, bg_ref,
            lam_ref, h0_ref, o_ref, sout_ref, carry_scr, a_scr, ab_scr, hl_scr, pl_scr,
            h_scr, hb_scr, xcv_scr)

    @pl.when(sweep == 0)
    def _():
        _lru_sweep(False, *args)

    @pl.when(sweep == 1)
    def _():
        _lru_sweep(True, *args)


def _lru(lx, lg, conv_w, conv_b, wg, bg, lam, h0, batch, seq, tt):
    nt = max(seq // tt, 1)
    group = max(tt // seq, 1)
    w = LRU_WIDTH
    t8 = tt // 8
    n8 = nt * t8
    per_batch_state = h0.shape[0] == batch
    h0_group = group if per_batch_state else 1
    tile_of = lambda s, j: j * (2 * s - 1) + (1 - s) * (nt - 1)
    return pl.pallas_call(
        functools.partial(_lru_kernel, nt=nt, tt=tt, seq_rows=min(seq, tt)),
        out_shape=[jax.ShapeDtypeStruct((batch * seq, w), F32),
                   jax.ShapeDtypeStruct((batch, 2, 8, w), F32)],
        grid=(batch // group, 2, nt),
        in_specs=[
            pl.BlockSpec((8, w), lambda b, s, j:
                         (b * n8 + jnp.maximum(tile_of(s, j) * t8 - 1, 0), 0)),
            pl.BlockSpec((tt, w), lambda b, s, j: (b * nt + tile_of(s, j), 0)),
            pl.BlockSpec((8, w), lambda b, s, j:
                         (b * n8 + jnp.minimum((tile_of(s, j) + 1) * t8, n8 - 1), 0)),
            pl.BlockSpec((tt, w), lambda b, s, j: (b * nt + j * s, 0)),
            _const_spec((4, w)),
            _const_spec((1, w)),
            pl.BlockSpec((1, w, 2 * w), lambda b, s, j: (1 - s, 0, 0)),
            pl.BlockSpec((1, 1, 2 * w), lambda b, s, j: (1 - s, 0, 0)),
            pl.BlockSpec((1, 1, w), lambda b, s, j: (1 - s, 0, 0)),
            pl.BlockSpec((h0_group, 1, 1, w),
                         lambda b, s, j: (b if per_batch_state else 0, 1 - s, 0, 0)),
        ],
        out_specs=[
            pl.BlockSpec((tt, w), lambda b, s, j: (b * nt + j * s, 0)),
            pl.BlockSpec((group, 1, 8, w), lambda b, s, j: (b, 1 - s, 0, 0)),
        ],
        scratch_shapes=[pltpu.VMEM((8, w), F32)]
        + [pltpu.VMEM((w // LANE, tt + LRU_SEGMENTS * LRU_SEG_PAD, LANE), F32)] * 2
        + [pltpu.VMEM((w // LANE * (LRU_SEGMENTS // 8), 8 * tt // LRU_SEGMENTS, LANE), F32)] * 2
        + [pltpu.VMEM((w // LANE, tt, LANE), F32)]
        + [pltpu.VMEM((nt * tt, w), F32)] * 2,
        compiler_params=_params(3),
        name="lru",
    )(lx, lx, lx, lg, conv_w, conv_b.reshape(1, w), wg, bg, lam, h0)


def _mlp_kernel(x_ref, att_ref, gla_ref, lru_ref, mod_ref, n2_ref, fn_ref,
                wo_ref, w1_ref, w2_ref, o_ref, *, final, ff_chunk):
    d = D_MODEL
    mix = _dot(att_ref[...], wo_ref[0, 0:ATTN_WIDTH, :])
    mix = mix + _dot(gla_ref[...].astype(BF16), wo_ref[0, ATTN_WIDTH:ATTN_WIDTH + GLA_WIDTH, :])
    mix = mix + _dot(lru_ref[...].astype(BF16), wo_ref[0, ATTN_WIDTH + GLA_WIDTH:d, :])
    g1 = mod_ref[0, 0, :, 2 * d:3 * d]
    sh2 = mod_ref[0, 0, :, 3 * d:4 * d]
    sc2 = mod_ref[0, 0, :, 4 * d:5 * d]
    g2 = mod_ref[0, 0, :, 5 * d:6 * d]
    x = x_ref[...] + g1 * mix
    ms = jnp.mean(x * x, axis=-1, keepdims=True)
    gain = n2_ref[...] * (1.0 + sc2)
    h = (x * lax.rsqrt(ms + EPS) * gain + sh2).astype(BF16)
    y = None
    for c in range(D_FF // ff_chunk):
        cols = slice(c * ff_chunk, (c + 1) * ff_chunk)
        u = jnp.maximum(_dot(h, w1_ref[0, :, cols]), 0.0)
        part = _dot((u * u).astype(BF16), w2_ref[0, cols, :])
        y = part if y is None else y + part
    x = x + g2 * y
    if final:
        ms = jnp.mean(x * x, axis=-1, keepdims=True)
        x = x * lax.rsqrt(ms + EPS) * fn_ref[...]
    o_ref[...] = x


def _out_mlp(x2d, att, gla, lru, mod, norm2, final_norm, wo, w1, w2, layer,
             rows_per_mod, mod_row0, tt, final):
    n, d = x2d.shape
    nt = n // tt
    mod_map = lambda i: (layer, mod_row0 + (i * tt) // rows_per_mod, 0, 0)
    row = lambda wd: pl.BlockSpec((tt, wd), lambda i: (i, 0))
    resident = lambda shape: pl.BlockSpec((1,) + shape, lambda i: (layer, 0, 0),
                                          pipeline_mode=pl.Buffered(1))
    return pl.pallas_call(
        functools.partial(_mlp_kernel, final=final, ff_chunk=512),
        out_shape=jax.ShapeDtypeStruct((n, d), F32),
        grid=(nt,),
        in_specs=[row(d), row(ATTN_WIDTH), row(GLA_WIDTH), row(LRU_WIDTH),
                  pl.BlockSpec((1, 1, 1, 6 * d), mod_map),
                  _const_spec((1, d)), _const_spec((1, d)),
                  resident((d, d)), resident((d, D_FF)), resident((D_FF, d))],
        out_specs=row(d),
        compiler_params=_params(1),
        name="out_mlp",
    )(x2d, att, gla, lru, mod, norm2.reshape(1, d), final_norm.reshape(1, d), wo, w1, w2)


def _prep_w_in(w_in):
    wz = w_in[..., C_A_END:C_B].astype(BF16)
    pad = jnp.zeros(w_in.shape[:-1] + (GZ_PAD - (C_B - C_A_END),), BF16)
    return (w_in[..., :C_A_END].astype(BF16), jnp.concatenate([wz, pad], axis=-1),
            w_in[..., C_B:].astype(BF16))


def _prep_gate_w(gw):
    out = []
    nz = 2 * GLA_GATE_RANK
    for dr in (0, 1):
        wf = jnp.zeros((nz, GLA_WIDTH), F32)
        wf = wf.at[dr * GLA_GATE_RANK:(dr + 1) * GLA_GATE_RANK].set(gw[dr])
        hi, lo = _split_bf16(wf)
        out.append(jnp.concatenate([hi, hi, lo, jnp.zeros((GZ_PAD - 3 * nz, GLA_WIDTH), BF16)],
                                   axis=0))
    return jnp.stack(out)


def _block_diag(w4):
    n, a, b = w4.shape
    eye = jnp.eye(n, dtype=w4.dtype)
    return jnp.einsum("nab,nm->namb", w4, eye).reshape(n * a, n * b)


def _rope_tables(seq):
    rows = seq // GRID_W
    nf = HEAD_DIM // 4
    inv = ROPE_BASE ** (-jnp.arange(nf, dtype=F32) / nf)
    ang_r = jnp.arange(rows, dtype=F32)[:, None] * inv
    ang_c = jnp.arange(GRID_W, dtype=F32)[:, None] * inv
    by_row = lambda t: jnp.repeat(t, GRID_W, axis=0)
    by_col = lambda t: jnp.tile(t, (rows, 1))
    cos_r, sin_r = by_row(jnp.cos(ang_r)), by_row(jnp.sin(ang_r))
    cos_c, sin_c = by_col(jnp.cos(ang_c)), by_col(jnp.sin(ang_c))
    cos = jnp.concatenate([cos_r, cos_r, cos_c, cos_c], axis=1)
    sin = jnp.concatenate([-sin_r, sin_r, -sin_c, sin_c], axis=1)
    return jnp.concatenate([cos] * 2, axis=1), jnp.concatenate([sin] * 2, axis=1)


def _tiles(latent, seq):
    if latent:
        return 1024, 4 * ATTN_SUB, 1024, 1024, 1024
    return 1024, 4 * seq, 4 * seq, 4 * seq, 1024


def _gla_state_to_internal(s):
    b = s.shape[0]
    return jnp.transpose(s, (0, 1, 4, 2, 3)).reshape(b, 2, GLA_DK, GLA_WIDTH)


def kernel(x_prompt, x_sample, c, cache_k, cache_v, state_gla, state_lru, c_ctx, w_mod, b_mod, norm1, norm2, w_in, attn_sink, gla_gate_w, gla_gate_b, gla_norm, lru_conv_w, lru_conv_b, lru_wa, lru_ba, lru_wx, lru_bx, lru_lambda, w_out, w_mlp1, w_mlp2, final_norm):
    depth = w_in.shape[0]
    bc, sc_len, d = x_prompt.shape
    bl, sl_len, _ = x_sample.shape
    past = cache_k.shape[2]

    cond = jnp.concatenate([c_ctx[None], c, jnp.zeros((MOD_ROWS - 1 - bl, d), F32)], axis=0)
    mod = _ada_mod(cond, w_mod, b_mod)

    rope_tabs = _rope_tables(sl_len)
    ck = cache_k.reshape(bl, depth, past, KV_WIDTH)
    cv = cache_v.reshape(bl, depth, past, KV_WIDTH)
    gla_s0_ctx = jnp.zeros((1, 2, GLA_DK, GLA_WIDTH), F32)
    lru_s0_ctx = jnp.zeros((1, 2, 1, LRU_WIDTH), F32)

    xp = x_prompt.reshape(bc * sc_len, d)
    xs = x_sample.reshape(bl * sl_len, d)
    w_in_b = _prep_w_in(w_in)
    wo = w_out.astype(BF16)
    w1 = w_mlp1.astype(BF16)
    w2 = w_mlp2.astype(BF16)
    ks, vs, sgs, sls = [], [], [], []
    for l in range(depth):
        gate_w = _prep_gate_w(gla_gate_w[l])
        gate_b = gla_gate_b[l].reshape(2, 1, GLA_WIDTH)
        lru_wg = (0.5 * jnp.stack([jnp.concatenate(
            [_block_diag(lru_wa[l, dr]), _block_diag(lru_wx[l, dr])], axis=1)
            for dr in (0, 1)])).astype(BF16)
        lru_bg = (0.5 * jnp.stack([jnp.concatenate([lru_ba[l, dr], lru_bx[l, dr]])
                                   for dr in (0, 1)])).reshape(2, 1, 2 * LRU_WIDTH)
        lam = lru_lambda[l].reshape(2, 1, LRU_WIDTH)
        final = l == depth - 1

        for is_lat in (False, True):
            if is_lat:
                x2d, batch, seq, row0, rpm, tabs = xs, bl, sl_len, 1, sl_len, rope_tabs
                gla_s0 = _gla_state_to_internal(state_gla[:, l])
                lru_s0 = state_lru[:, l].reshape(bl, 2, 1, LRU_WIDTH)
            else:
                x2d, batch, seq, row0, rpm, tabs = xp, bc, sc_len, 0, bc * sc_len, None
                gla_s0, lru_s0 = gla_s0_ctx, lru_s0_ctx
            tt_proj, tq, tt_gla, tt_lru, tt_mlp = _tiles(is_lat, seq)
            q, k, v, gin, lx, lg = _in_proj(x2d, mod, norm1[l], w_in_b, l, tabs, rpm, row0,
                                            tt_proj)
            att = _attention(q, k, v, attn_sink[l], batch, seq, tq, ATTN_SUB,
                             ctx_kv=(ck, cv, l) if is_lat else None)
            gla, sg = _gla(gin, gate_w, gate_b, gla_norm[l], gla_s0, batch, seq, tt_gla)
            lru, sl = _lru(lx, lg, lru_conv_w[l], lru_conv_b[l], lru_wg, lru_bg, lam, lru_s0,
                           batch, seq, tt_lru)
            x2d = _out_mlp(x2d, att, gla, lru, mod, norm2[l], final_norm, wo, w1, w2, l,
                           rpm, row0, tt_mlp, final)
            if is_lat:
                xs = x2d
            else:
                xp = x2d
                ks.append(k.reshape(bc, sc_len, KV_WIDTH))
                vs.append(v.reshape(bc, sc_len, KV_WIDTH))
                sgs.append(sg)
                sls.append(sl[:, :, 0, :])

    y_prompt = xp.reshape(bc, sc_len, d)
    y_sample = xs.reshape(bl, sl_len, d)
    kv_shape = (bc, depth, sc_len, KV_WIDTH // HEAD_DIM, HEAD_DIM)
    new_k = jnp.stack(ks, axis=1).reshape(kv_shape)
    new_v = jnp.stack(vs, axis=1).reshape(kv_shape)
    sg_all = jnp.stack(sgs, axis=1)
    new_sg = jnp.transpose(
        sg_all.reshape(bc, depth, 2, GLA_DK, GLA_HEADS, GLA_DK), (0, 1, 2, 4, 5, 3))
    new_sl = jnp.stack(sls, axis=1)
    return (y_prompt, y_sample, new_k, new_v, new_sg, new_sl)
```

```python
import functools

import jax
import jax.numpy as jnp
from jax import lax
from jax.experimental import pallas as pl
from jax.experimental.pallas import tpu as pltpu

F32 = jnp.float32
BF16 = jnp.bfloat16
LANE = 128

D_MODEL = 1024
GRID_W = 64
EPS = 1e-6
HEAD_DIM = 64
ATTN_WIDTH = 512
ATTN_HEADS = 8
KV_WIDTH = 128
WINDOW = 128
ATTN_LOOKAHEAD = 1
ATTN_SUB = 256
ROPE_BASE = 10000.0
NEG_INF = -1e30
GLA_WIDTH = 256
GLA_DK = 64
GLA_HEADS = 4
GLA_GATE_RANK = 16
GLA_GATE_NORM = 16.0
GLA_CHUNK = 64
GLA_SKEW = 2
LRU_WIDTH = 256
LRU_C = 8.0
LRU_SEG_PAD = 4
LRU_SEGMENTS = 16
D_FF = 4096
MLP_FF_CHUNK = 512
ADA_COL_TILE = 1536
MOD_ROWS = 8
GZ_PAD = 128
C_Q, C_K, C_V, C_GLA, C_A_END, C_B = 0, 512, 640, 768, 1536, 1568
GLA_IN = 4 * GLA_WIDTH + GZ_PAD
VMEM_LIMIT = 56 * 1024 * 1024

NT_DIMS = (((1,), (1,)), ((), ()))


def _split_bf16(a):
    hi = a.astype(BF16)
    lo = (a - hi.astype(F32)).astype(BF16)
    return hi, lo


def _dot(a, b):
    return jnp.dot(a, b, preferred_element_type=F32)


def _dot_nt(a, b):
    return lax.dot_general(a, b, NT_DIMS, preferred_element_type=F32)


def _dot_x3(a, b):
    ah, al = _split_bf16(a)
    bh, bl = _split_bf16(b)
    return _dot(ah, bh) + (_dot(ah, bl) + _dot(al, bh))


def _softplus(y):
    return jnp.maximum(y, 0.0) + jnp.log1p(jnp.exp(-jnp.abs(y)))


def _log_sigmoid(y):
    return jnp.minimum(y, 0.0) - jnp.log(1.0 + jnp.exp(-jnp.abs(y)))


def _sigmoid(y):
    return 0.5 * jnp.tanh(0.5 * y) + 0.5


def _silu(y):
    return y * _sigmoid(y)


def _gelu_tanh(y):
    c = 0.7978845608028654
    hy = 0.5 * y
    return hy * jnp.tanh(y * (c + (c * 0.044715) * (y * y))) + hy


def _params(n_axes):
    return pltpu.CompilerParams(
        dimension_semantics=("arbitrary",) * n_axes, vmem_limit_bytes=VMEM_LIMIT)


def _const_spec(shape):
    nd = len(shape)
    return pl.BlockSpec(shape, lambda *_: (0,) * nd)


def _ada_kernel(c_ref, w_ref, b_ref, o_ref):
    s = _silu(c_ref[...])
    res = _dot_x3(s, w_ref[0]) + b_ref[0]
    for r in range(MOD_ROWS):
        o_ref[0, r] = res[r:r + 1, :]


def _ada_mod(cond, w_mod, b_mod):
    depth, d, n = w_mod.shape
    tn = ADA_COL_TILE
    return pl.pallas_call(
        _ada_kernel,
        out_shape=jax.ShapeDtypeStruct((depth, MOD_ROWS, 1, n), F32),
        grid=(depth, n // tn),
        in_specs=[
            pl.BlockSpec((MOD_ROWS, d), lambda l, j: (0, 0)),
            pl.BlockSpec((1, d, tn), lambda l, j: (l, 0, j)),
            pl.BlockSpec((1, 1, tn), lambda l, j: (l, 0, j)),
        ],
        out_specs=pl.BlockSpec((1, MOD_ROWS, 1, tn), lambda l, j: (l, 0, 0, j)),
        compiler_params=_params(2),
        name="ada_mod",
    )(cond, w_mod, b_mod.reshape(depth, 1, n))


def _rope(x, cos, sin_signed):
    w = x.shape[1]
    lane = lax.broadcasted_iota(jnp.int32, (1, w), 1)
    first = (lane % 32) < 16
    swapped = jnp.where(first, pltpu.roll(x, w - 16, 1), pltpu.roll(x, 16, 1))
    return x * cos + swapped * sin_signed


def _half_variants(a):
    lane = lax.broadcasted_iota(jnp.int32, (1, LANE), 1)
    low = lane < HEAD_DIM
    ar = pltpu.roll(a, HEAD_DIM, 1)
    z = jnp.zeros_like(a)
    parts = [jnp.where(low, a, z), jnp.where(low, z, ar),
             jnp.where(low, ar, z), jnp.where(low, z, a)]
    return jnp.concatenate([t.astype(BF16) for t in parts], axis=1)


def _inproj_kernel(*refs, rope):
    if rope:
        (x_ref, mod_ref, n1_ref, wa_ref, wz_ref, wb_ref, cos_ref, sin_ref,
         q_ref, k_ref, v_ref, gla_ref, lx_ref, lg_ref) = refs
    else:
        (x_ref, mod_ref, n1_ref, wa_ref, wz_ref, wb_ref,
         q_ref, k_ref, v_ref, gla_ref, lx_ref, lg_ref) = refs
    d = D_MODEL
    x = x_ref[...]
    ms = jnp.mean(x * x, axis=-1, keepdims=True)
    sh = mod_ref[0, 0, :, 0:d]
    sc = mod_ref[0, 0, :, d:2 * d]
    gain = n1_ref[...] * (1.0 + sc)
    h = (x * lax.rsqrt(ms + EPS) * gain + sh).astype(BF16)
    p_att = _dot(h, wa_ref[0, :, C_Q:C_GLA])
    p_gla = _dot(h, wa_ref[0, :, C_GLA:C_A_END])
    q = p_att[:, C_Q:C_K]
    k = p_att[:, C_K:C_V]
    v = p_att[:, C_V:C_GLA]
    if rope:
        cos = cos_ref[...]
        sin = sin_ref[...]
        k = _rope(k, cos, sin)
        q = _rope(q, jnp.concatenate([cos] * 4, axis=1), jnp.concatenate([sin] * 4, axis=1))
    q_ref[...] = (q * (HEAD_DIM ** -0.5)).astype(BF16)
    k_ref[...] = k
    v_ref[...] = v
    w = GLA_WIDTH
    pb = _dot(h, wb_ref[0])
    gla_ref[:, 0:3 * w] = p_gla
    gla_ref[:, 3 * w:4 * w] = pb[:, 0:w]
    gla_ref[:, 4 * w:4 * w + GZ_PAD] = _dot(h, wz_ref[0])
    lx_ref[...] = pb[:, w:w + LRU_WIDTH]
    lg_ref[...] = pb[:, w + LRU_WIDTH:w + 2 * LRU_WIDTH]


def _in_proj(x2d, mod, norm1, w_abz, layer, rope_tabs, rows_per_mod, mod_row0, tt):
    n, d = x2d.shape
    nt = n // tt
    rope = rope_tabs is not None
    mod_map = lambda i: (layer, mod_row0 + (i * tt) // rows_per_mod, 0, 0)
    by_layer = lambda arr: pl.BlockSpec((1,) + arr.shape[1:], lambda i: (layer, 0, 0))
    in_specs = [
        pl.BlockSpec((tt, d), lambda i: (i, 0)),
        pl.BlockSpec((1, 1, 1, 6 * d), mod_map),
        _const_spec((1, d)),
    ] + [by_layer(arr) for arr in w_abz]
    args = [x2d, mod, norm1.reshape(1, d)] + list(w_abz)
    if rope:
        t = rope_tabs[0].shape[0]
        tpb = t // tt
        in_specs += [pl.BlockSpec((tt, LANE), lambda i: (i % tpb, 0))] * 2
        args += list(rope_tabs)
    outs = [(ATTN_WIDTH, BF16), (KV_WIDTH, F32), (KV_WIDTH, F32),
            (GLA_IN, F32), (LRU_WIDTH, F32), (LRU_WIDTH, F32)]
    widths = [o[0] for o in outs]
    dtypes = [o[1] for o in outs]
    return pl.pallas_call(
        functools.partial(_inproj_kernel, rope=rope),
        out_shape=[jax.ShapeDtypeStruct((n, wd), dt) for wd, dt in zip(widths, dtypes)],
        grid=(nt,),
        in_specs=in_specs,
        out_specs=[pl.BlockSpec((tt, wd), lambda i: (i, 0)) for wd in widths],
        compiler_params=_params(1),
        name="in_proj_lat" if rope else "in_proj_ctx",
    )(*args)


def _attn_kernel(*refs, latent, tq, sub, seq):
    n_units = tq // sub
    if latent:
        (sink_ref, q_ref, band_ref, kp_ref, kc_ref, kn_ref, vp_ref, vc_ref, vn_ref,
         ck_ref, cv_ref, o_ref) = refs
        i = pl.program_id(1)
        k_loc = _half_variants(jnp.concatenate([kp_ref[...], kc_ref[...], kn_ref[...]], axis=0))
        v_loc = _half_variants(jnp.concatenate([vp_ref[...], vc_ref[...], vn_ref[...]], axis=0))
        nk = sub + 2 * WINDOW
        k_ctx = _half_variants(ck_ref[0, 0])
        v_ctx = _half_variants(cv_ref[0, 0])
        key_rows = [slice(u * sub, u * sub + nk) for u in range(n_units)]
        biases = []
        for u in range(n_units):
            kpos = i * tq + u * sub - WINDOW + lax.broadcasted_iota(jnp.int32, (1, nk), 1)
            biases.append(band_ref[...] + jnp.where((kpos >= 0) & (kpos < seq), 0.0, NEG_INF))
    else:
        sink_ref, q_ref, kc_ref, vc_ref, o_ref = refs
        k_loc = _half_variants(kc_ref[...])
        v_loc = _half_variants(vc_ref[...])
        key_rows = [slice(u * sub, (u + 1) * sub) for u in range(n_units)]
        biases = [None] * n_units
    block = lambda t, kv, half: t[:, (2 * kv + half) * LANE:(2 * kv + half + 1) * LANE]
    with_ones = lambda t: [jnp.concatenate([block(t, kv, half), jnp.ones((t.shape[0], LANE), BF16)],
                                           axis=1) for kv in range(2) for half in range(2)]
    v_loc_ones = with_ones(v_loc)
    v_ctx_ones = with_ones(v_ctx) if latent else None

    def scores(item):
        u, head = divmod(item, ATTN_HEADS)
        c, half = divmod(head, 2)
        qc = q_ref[u * sub:(u + 1) * sub, c * LANE:(c + 1) * LANE]
        s = _dot_nt(qc, block(k_loc[key_rows[u]], c // 2, half))
        if biases[u] is not None:
            s = s + biases[u]
        return s, (_dot_nt(qc, block(k_ctx, c // 2, half)) if latent else None)

    n_items = n_units * ATTN_HEADS
    ready = [scores(it) for it in range(ATTN_LOOKAHEAD)]
    acc = None
    for item in range(n_items):
        u, head = divmod(item, ATTN_HEADS)
        c, half = divmod(head, 2)
        s, s_c = ready.pop(0)
        if item + ATTN_LOOKAHEAD < n_items:
            ready.append(scores(item + ATTN_LOOKAHEAD))
        sink = sink_ref[head]
        m = jnp.maximum(jnp.max(s, axis=-1, keepdims=True), sink)
        if latent:
            m = jnp.maximum(m, jnp.max(s_c, axis=-1, keepdims=True))
        variant = 2 * (c // 2) + half
        o = _dot(jnp.exp(s - m).astype(BF16), v_loc_ones[variant][key_rows[u]])
        if latent:
            o = o + _dot(jnp.exp(s_c - m).astype(BF16), v_ctx_ones[variant])
        denom = jnp.exp(sink - m) + o[:, LANE:2 * LANE]
        o = o[:, 0:LANE] * (1.0 / denom)
        if half == 0:
            acc = o
        else:
            o_ref[u * sub:(u + 1) * sub, c * LANE:(c + 1) * LANE] = (acc + o).astype(BF16)


def _attention(q, k, v, sink, batch, seq, tq, sub, ctx_kv=None):
    latent = ctx_kv is not None
    nq = max(seq // tq, 1)
    group = max(tq // seq, 1)
    kern = functools.partial(_attn_kernel, latent=latent, tq=tq, sub=sub, seq=seq)
    smem = pl.BlockSpec(memory_space=pltpu.SMEM)
    kvw = KV_WIDTH
    if latent:
        ck, cv, layer = ctx_kv
        past = ck.shape[2]
        wb = tq // WINDOW
        nwb = seq // WINDOW
        nk = sub + 2 * WINDOW
        r = lax.broadcasted_iota(jnp.int32, (sub, nk), 0)
        j = lax.broadcasted_iota(jnp.int32, (sub, nk), 1)
        band = jnp.where((j >= r) & (j <= r + 2 * WINDOW), 0.0, NEG_INF).astype(F32)
        cur = pl.BlockSpec((tq, kvw), lambda b, i: (b * nq + i, 0))
        prev = pl.BlockSpec((WINDOW, kvw),
                            lambda b, i: (b * nwb + jnp.maximum(i * wb - 1, 0), 0))
        nxt = pl.BlockSpec((WINDOW, kvw),
                           lambda b, i: (b * nwb + jnp.minimum((i + 1) * wb, nwb - 1), 0))
        cspec = pl.BlockSpec((1, 1, past, kvw), lambda b, i: (b, layer, 0, 0))
        in_specs = [smem, pl.BlockSpec((tq, ATTN_WIDTH), lambda b, i: (b * nq + i, 0)),
                    _const_spec((sub, nk)), prev, cur, nxt, prev, cur, nxt, cspec, cspec]
        args = (sink, q, band, k, k, k, v, v, v, ck, cv)
    else:
        cur = pl.BlockSpec((tq, kvw), lambda b, i: (b * nq + i, 0))
        in_specs = [smem, pl.BlockSpec((tq, ATTN_WIDTH), lambda b, i: (b * nq + i, 0)), cur, cur]
        args = (sink, q, k, v)
    return pl.pallas_call(
        kern,
        out_shape=jax.ShapeDtypeStruct((batch * seq, ATTN_WIDTH), BF16),
        grid=(batch // group, nq),
        in_specs=in_specs,
        out_specs=pl.BlockSpec((tq, ATTN_WIDTH), lambda b, i: (b * nq + i, 0)),
        compiler_params=_params(2),
        name="attn_lat" if latent else "attn_ctx",
    )(*args)


def _gla_sweep(fwd, j, nt, tt, seq_rows, gin_ref, wg_ref, bg_ref, ng_ref, s0_ref, o_ref, sout_ref,
               st_scr, g_scr, ob_scr, qin_scr, kin_scr, kend_scr, a_scr, stc_scr, dec_scr):
    w = GLA_WIDTH
    ck = GLA_CHUNK
    nc = tt // ck
    cps = seq_rows // ck
    whole_seqs = nt == 1
    tile = j if fwd else nt - 1 - j
    starts = [(ci if fwd else nc - 1 - ci) * ck for ci in range(nc)]

    rr = lax.broadcasted_iota(jnp.int32, (w, w), 0) // GLA_DK
    cc = lax.broadcasted_iota(jnp.int32, (w, w), 1) // GLA_DK
    bd_bf = (rr == cc).astype(F32).astype(BF16)
    tr = lax.broadcasted_iota(jnp.int32, (ck, 2 * ck), 0)
    tc = lax.broadcasted_iota(jnp.int32, (ck, 2 * ck), 1) % ck
    tri2 = ((tr >= tc) if fwd else (tr <= tc)).astype(F32).astype(BF16)
    ar = lax.broadcasted_iota(jnp.int32, (ck, w), 0)
    ac = lax.broadcasted_iota(jnp.int32, (ck, w), 1) % ck
    causal = (ar >= ac) if fwd else (ar <= ac)

    if not whole_seqs:
        @pl.when(j == 0)
        def _():
            st_scr[...] = s0_ref[0, 0]

    zpad = gin_ref[:, 4 * w:4 * w + GZ_PAD]
    zh = zpad.astype(BF16).astype(F32)
    zcat = zh + pltpu.roll(zpad - zh, 32, 1) + pltpu.roll(zh, 64, 1)
    logit = _dot(zcat.astype(BF16), wg_ref[0]) + bg_ref[0]
    g_scr[...] = _log_sigmoid(logit) * (1.0 / GLA_GATE_NORM)

    tile_rows = pl.multiple_of(tile * tt, ck)

    def decayed(ci, c0):
        rows = pl.ds(c0, ck)
        g = g_scr[rows, :]
        gh, gl = _split_bf16(g)
        b = _dot(tri2, jnp.concatenate([gh, gl], axis=0))
        btot = jnp.sum(g, axis=0, keepdims=True)
        k = gin_ref[rows, w:2 * w]
        qin_scr[rows, :] = (gin_ref[rows, 0:w] * (GLA_DK ** -0.5) * jnp.exp(b)).astype(BF16)
        kin_scr[rows, :] = (k * jnp.exp(-b)).astype(BF16)
        kend_scr[rows, :] = (k * jnp.exp(btot - b)).astype(BF16)
        dec_scr[ci] = jnp.broadcast_to(jnp.exp(btot), (8, w))

    def scores(ci, c0):
        rows = pl.ds(c0, ck)
        k_bd = jnp.concatenate([kin_scr[rows, :]] * GLA_HEADS, axis=0) * bd_bf
        a_scr[rows, :] = jnp.where(causal, _dot_nt(qin_scr[rows, :], k_bd), 0.0).astype(BF16)

    def state(ci, c0):
        rows = pl.ds(c0, ck)
        seq_i = c0 // seq_rows
        first = whole_seqs and ci % cps == 0
        st = s0_ref[min(seq_i, s0_ref.shape[0] - 1), 0] if first else st_scr[...]
        stc_scr[ci] = jnp.concatenate([st.astype(BF16)] * GLA_HEADS, axis=0) * bd_bf
        vt = gin_ref[rows, 2 * w:3 * w].T
        vt_heads = jnp.concatenate([vt[h * GLA_DK:(h + 1) * GLA_DK, :] for h in range(GLA_HEADS)],
                                   axis=1).astype(BF16)
        kend_bd = jnp.concatenate([kend_scr[rows, :]] * GLA_HEADS, axis=0) * bd_bf
        st = st * dec_scr[ci][0:1, :] + _dot(vt_heads, kend_bd)
        st_scr[...] = st
        if whole_seqs and ci % cps == cps - 1:
            sout_ref[seq_i, 0] = st

    def output(ci, c0):
        rows = pl.ds(c0, ck)
        v_bd = jnp.concatenate([gin_ref[rows, 2 * w:3 * w].astype(BF16)] * GLA_HEADS,
                               axis=0) * bd_bf
        o = _dot(a_scr[rows, :], v_bd) + _dot_nt(qin_scr[rows, :], stc_scr[ci])
        if fwd:
            o_ref[rows, :] = o
        else:
            ob_scr[pl.ds(tile_rows + c0, ck), :] = o

    stages = (decayed, scores, state, output)
    for step in range(nc + GLA_SKEW * (len(stages) - 1)):
        for si, stage in enumerate(stages):
            ci = step - GLA_SKEW * si
            if 0 <= ci < nc:
                stage(ci, starts[ci])

    if fwd:
        o = o_ref[...] + ob_scr[pl.ds(tile_rows, tt), :]
        ms = _dot((o * o).astype(BF16), bd_bf) * (1.0 / GLA_DK)
        r = gin_ref[:, 3 * w:4 * w]
        o_ref[...] = o * lax.rsqrt(ms + EPS) * ng_ref[...] * _silu(r)

    if not whole_seqs:
        @pl.when(j == nt - 1)
        def _():
            sout_ref[0, 0] = st_scr[...]


def _gla_kernel(gin_ref, wg_ref, bg_ref, ng_ref, s0_ref, o_ref, sout_ref,
                st_scr, g_scr, ob_scr, qin_scr, kin_scr, kend_scr, a_scr, stc_scr, dec_scr,
                *, nt, tt, seq_rows):
    sweep = pl.program_id(1)
    j = pl.program_id(2)
    args = (j, nt, tt, seq_rows, gin_ref, wg_ref, bg_ref, ng_ref, s0_ref, o_ref, sout_ref,
            st_scr, g_scr, ob_scr, qin_scr, kin_scr, kend_scr, a_scr, stc_scr, dec_scr)

    @pl.when(sweep == 0)
    def _():
        _gla_sweep(False, *args)

    @pl.when(sweep == 1)
    def _():
        _gla_sweep(True, *args)


def _gla(gin, wg, bg, norm_g, s0, batch, seq, tt):
    nt = max(seq // tt, 1)
    group = max(tt // seq, 1)
    w = GLA_WIDTH
    per_batch_state = s0.shape[0] == batch
    s0_group = group if per_batch_state else 1
    tile_of = lambda s, j: j * (2 * s - 1) + (1 - s) * (nt - 1)
    return pl.pallas_call(
        functools.partial(_gla_kernel, nt=nt, tt=tt, seq_rows=min(seq, tt)),
        out_shape=[jax.ShapeDtypeStruct((batch * seq, w), F32),
                   jax.ShapeDtypeStruct((batch, 2, GLA_DK, w), F32)],
        grid=(batch // group, 2, nt),
        in_specs=[
            pl.BlockSpec((tt, GLA_IN), lambda b, s, j: (b * nt + tile_of(s, j), 0)),
            pl.BlockSpec((1, GZ_PAD, w), lambda b, s, j: (1 - s, 0, 0)),
            pl.BlockSpec((1, 1, w), lambda b, s, j: (1 - s, 0, 0)),
            _const_spec((1, w)),
            pl.BlockSpec((s0_group, 1, GLA_DK, w),
                         lambda b, s, j: (b if per_batch_state else 0, 1 - s, 0, 0)),
        ],
        out_specs=[
            pl.BlockSpec((tt, w), lambda b, s, j: (b * nt + j * s, 0)),
            pl.BlockSpec((group, 1, GLA_DK, w), lambda b, s, j: (b, 1 - s, 0, 0)),
        ],
        scratch_shapes=[pltpu.VMEM((GLA_DK, w), F32), pltpu.VMEM((tt, w), F32),
                        pltpu.VMEM((nt * tt, w), F32)]
        + [pltpu.VMEM((tt, w), BF16)] * 4
        + [pltpu.VMEM((tt // GLA_CHUNK, w, w), BF16),
           pltpu.VMEM((tt // GLA_CHUNK, 8, w), F32)],
        compiler_params=_params(3),
        name="gla",
    )(gin, wg, bg, norm_g.reshape(1, w), s0)


def _lru_sweep(fwd, j, nt, tt, seq_rows, xp_ref, xc_ref, xn_ref, lg_ref, cw_ref, cb_ref,
               wg_ref, bg_ref, lam_ref, h0_ref, o_ref, sout_ref,
               carry_scr, a_scr, ab_scr, hl_scr, pl_scr, h_scr, hb_scr, xcv_scr):
    w = LRU_WIDTH
    n_seg = LRU_SEGMENTS
    n_groups = n_seg // 8
    seg_len = tt // n_seg
    tile = j if fwd else nt - 1 - j
    whole_seqs = nt == 1
    segs_per_seq = seq_rows // seg_len if whole_seqs else n_seg

    if not whole_seqs:
        @pl.when(j == 0)
        def _():
            carry_scr[...] = jnp.broadcast_to(h0_ref[0, 0], (8, w))

    all_rows = pl.ds(pl.multiple_of(tile * tt, 8), tt)
    if fwd:
        xc = xcv_scr[all_rows, :]
    else:
        pre = jnp.where(tile > 0, xp_ref[...], 0.0)
        post = jnp.where(tile < nt - 1, xn_ref[...], 0.0)
        xe = jnp.concatenate([pre, xc_ref[...], post], axis=0)
        ne = tt + 16
        taps = [pltpu.roll(xe, 2, 0)[8:8 + tt], pltpu.roll(xe, 1, 0)[8:8 + tt], xe[8:8 + tt],
                pltpu.roll(xe, ne - 1, 0)[8:8 + tt]]
        if whole_seqs and seq_rows < tt:
            pos = lax.broadcasted_iota(jnp.int32, (tt, 1), 0) % seq_rows
            taps[0] = jnp.where(pos >= 2, taps[0], 0.0)
            taps[1] = jnp.where(pos >= 1, taps[1], 0.0)
            taps[3] = jnp.where(pos < seq_rows - 1, taps[3], 0.0)
        xc = cb_ref[...] + cw_ref[2:3, :] * taps[2]
        xc = xc + cw_ref[0:1, :] * taps[0]
        xc = xc + cw_ref[1:2, :] * taps[1]
        xc = xc + cw_ref[3:4, :] * taps[3]
        xcv_scr[all_rows, :] = xc

    ri = _dot(xc.astype(BF16), wg_ref[0]) + bg_ref[0]
    t_a = jnp.tanh(ri[:, 0:w])
    t_x = jnp.tanh(ri[:, w:2 * w])
    c2 = (-0.5 * LRU_C) * _softplus(-lam_ref[0])
    a = jnp.exp((t_a + 1.0) * c2)
    hx = 0.5 * xc
    bt = jnp.sqrt(1.0 - a * a) * (t_x * hx + hx)

    nl = w // LANE
    pitch = seg_len + LRU_SEG_PAD
    for lt in range(nl):
        for k in range(n_seg):
            dst = slice(k * pitch, k * pitch + seg_len)
            src = slice(k * seg_len, (k + 1) * seg_len)
            a_scr[lt, dst, :] = a[src, lt * LANE:(lt + 1) * LANE]
            ab_scr[lt, dst, :] = bt[src, lt * LANE:(lt + 1) * LANE]

    chains = [(lt, g) for lt in range(nl) for g in range(n_groups)]

    def step(ii, hp):
        i = ii if fwd else seg_len - 1 - ii
        rows = pl.ds(pl.multiple_of(i * 8, 8), 8)
        out = []
        for ch, (lt, g) in enumerate(chains):
            h, p = hp[ch]
            seg_rows = pl.ds(g * 8 * pitch + i, 8, stride=pitch)
            a_i = a_scr[lt, seg_rows, :]
            h = a_i * h + ab_scr[lt, seg_rows, :]
            p = a_i * p
            hl_scr[ch, rows, :] = h
            pl_scr[ch, rows, :] = p
            out.append((h, p))
        return tuple(out)

    init = tuple((jnp.zeros((8, LANE), F32), jnp.ones((8, LANE), F32)) for _ in chains)
    ends = lax.fori_loop(0, seg_len, step, init, unroll=8)
    h_end = jnp.concatenate([jnp.concatenate([ends[lt * n_groups + g][0] for lt in range(nl)],
                                             axis=1) for g in range(n_groups)], axis=0)
    p_end = jnp.concatenate([jnp.concatenate([ends[lt * n_groups + g][1] for lt in range(nl)],
                                             axis=1) for g in range(n_groups)], axis=0)

    c = None if whole_seqs else carry_scr[0:1, :]
    c_in = [None] * n_seg
    for n, k in enumerate(range(n_seg) if fwd else range(n_seg - 1, -1, -1)):
        seq_i = k // segs_per_seq
        if whole_seqs and n % segs_per_seq == 0:
            c = h0_ref[min(seq_i, h0_ref.shape[0] - 1), 0]
        c_in[k] = c
        c = h_end[k:k + 1, :] + p_end[k:k + 1, :] * c
        if whole_seqs and n % segs_per_seq == segs_per_seq - 1:
            sout_ref[seq_i, 0] = jnp.broadcast_to(c, (8, w))
    if not whole_seqs:
        carry_scr[...] = jnp.broadcast_to(c, (8, w))

    for k in range(n_seg):
        g, kk = divmod(k, 8)
        for i0 in range(0, seg_len, 8):
            src = pl.ds(i0 * 8 + kk, 8, stride=8)
            r0 = k * seg_len + i0
            for lt in range(nl):
                ch = lt * n_groups + g
                c_k = c_in[k][:, lt * LANE:(lt + 1) * LANE]
                h_scr[lt, r0:r0 + 8, :] = hl_scr[ch, src, :] + pl_scr[ch, src, :] * c_k
    h_all = jnp.concatenate([h_scr[lt] for lt in range(nl)], axis=1)
    if fwd:
        o_ref[...] = (h_all + hb_scr[all_rows, :]) * _gelu_tanh(lg_ref[...])
    else:
        hb_scr[all_rows, :] = h_all

    if not whole_seqs:
        @pl.when(j == nt - 1)
        def _():
            sout_ref[0, 0] = carry_scr[...]


def _lru_kernel(xp_ref, xc_ref, xn_ref, lg_ref, cw_ref, cb_ref, wg_ref, bg_ref,
                lam_ref, h0_ref, o_ref, sout_ref, carry_scr, a_scr, ab_scr, hl_scr, pl_scr,
                h_scr, hb_scr, xcv_scr, *, nt, tt, seq_rows):
    sweep = pl.program_id(1)
    j = pl.program_id(2)
    args = (j, nt, tt, seq_rows, xp_ref, xc_ref, xn_ref, lg_ref, cw_ref, cb_ref, wg_ref, bg_ref,
            lam_ref, h0_ref, o_ref, sout_ref, carry_scr, a_scr, ab_scr, hl_scr, pl_scr,
            h_scr, hb_scr, xcv_scr)

    @pl.when(sweep == 0)
    def _():
        _lru_sweep(False, *args)

    @pl.when(sweep == 1)
    def _():
        _lru_sweep(True, *args)


def _lru(lx, lg, conv_w, conv_b, wg, bg, lam, h0, batch, seq, tt):
    nt = max(seq // tt, 1)
    group = max(tt // seq, 1)
    w = LRU_WIDTH
    t8 = tt // 8
    n8 = nt * t8
    per_batch_state = h0.shape[0] == batch
    h0_group = group if per_batch_state else 1
    tile_of = lambda s, j: j * (2 * s - 1) + (1 - s) * (nt - 1)
    return pl.pallas_call(
        functools.partial(_lru_kernel, nt=nt, tt=tt, seq_rows=min(seq, tt)),
        out_shape=[jax.ShapeDtypeStruct((batch * seq, w), F32),
                   jax.ShapeDtypeStruct((batch, 2, 8, w), F32)],
        grid=(batch // group, 2, nt),
        in_specs=[
            pl.BlockSpec((8, w), lambda b, s, j:
                         (b * n8 + jnp.maximum(tile_of(s, j) * t8 - 1, 0), 0)),
            pl.BlockSpec((tt, w), lambda b, s, j: (b * nt + tile_of(s, j), 0)),
            pl.BlockSpec((8, w), lambda b, s, j:
                         (b * n8 + jnp.minimum((tile_of(s, j) + 1) * t8, n8 - 1), 0)),
            pl.BlockSpec((tt, w), lambda b, s, j: (b * nt + j * s, 0)),
            _const_spec((4, w)),
            _const_spec((1, w)),
            pl.BlockSpec((1, w, 2 * w), lambda b, s, j: (1 - s, 0, 0)),
            pl.BlockSpec((1, 1, 2 * w), lambda b, s, j: (1 - s, 0, 0)),
            pl.BlockSpec((1, 1, w), lambda b, s, j: (1 - s, 0, 0)),
            pl.BlockSpec((h0_group, 1, 1, w),
                         lambda b, s, j: (b if per_batch_state else 0, 1 - s, 0, 0)),
        ],
        out_specs=[
            pl.BlockSpec((tt, w), lambda b, s, j: (b * nt + j * s, 0)),
            pl.BlockSpec((group, 1, 8, w), lambda b, s, j: (b, 1 - s, 0, 0)),
        ],
        scratch_shapes=[pltpu.VMEM((8, w), F32)]
        + [pltpu.VMEM((w // LANE, tt + LRU_SEGMENTS * LRU_SEG_PAD, LANE), F32)] * 2
        + [pltpu.VMEM((w // LANE * (LRU_SEGMENTS // 8), 8 * tt // LRU_SEGMENTS, LANE), F32)] * 2
        + [pltpu.VMEM((w // LANE, tt, LANE), F32)]
        + [pltpu.VMEM((nt * tt, w), F32)] * 2,
        compiler_params=_params(3),
        name="lru",
    )(lx, lx, lx, lg, conv_w, conv_b.reshape(1, w), wg, bg, lam, h0)


def _mlp_kernel(x_ref, att_ref, gla_ref, lru_ref, mod_ref, n2_ref, fn_ref,
                wo_ref, w1_ref, w2_ref, o_ref, *, final, ff_chunk):
    d = D_MODEL
    mix = _dot(att_ref[...], wo_ref[0, 0:ATTN_WIDTH, :])
    mix = mix + _dot(gla_ref[...].astype(BF16), wo_ref[0, ATTN_WIDTH:ATTN_WIDTH + GLA_WIDTH, :])
    mix = mix + _dot(lru_ref[...].astype(BF16), wo_ref[0, ATTN_WIDTH + GLA_WIDTH:d, :])
    g1 = mod_ref[0, 0, :, 2 * d:3 * d]
    sh2 = mod_ref[0, 0, :, 3 * d:4 * d]
    sc2 = mod_ref[0, 0, :, 4 * d:5 * d]
    g2 = mod_ref[0, 0, :, 5 * d:6 * d]
    x = x_ref[...] + g1 * mix
    ms = jnp.mean(x * x, axis=-1, keepdims=True)
    gain = n2_ref[...] * (1.0 + sc2)
    h = (x * lax.rsqrt(ms + EPS) * gain + sh2).astype(BF16)
    y = None
    for c in range(D_FF // ff_chunk):
        cols = slice(c * ff_chunk, (c + 1) * ff_chunk)
        u = jnp.maximum(_dot(h, w1_ref[0, :, cols]), 0.0)
        part = _dot((u * u).astype(BF16), w2_ref[0, cols, :])
        y = part if y is None else y + part
    x = x + g2 * y
    if final:
        ms = jnp.mean(x * x, axis=-1, keepdims=True)
        x = x * lax.rsqrt(ms + EPS) * fn_ref[...]
    o_ref[...] = x


def _out_mlp(x2d, att, gla, lru, mod, norm2, final_norm, wo, w1, w2, layer,
             rows_per_mod, mod_row0, tt, final):
    n, d = x2d.shape
    nt = n // tt
    mod_map = lambda i: (layer, mod_row0 + (i * tt) // rows_per_mod, 0, 0)
    row = lambda wd: pl.BlockSpec((tt, wd), lambda i: (i, 0))
    resident = lambda shape: pl.BlockSpec((1,) + shape, lambda i: (layer, 0, 0),
                                          pipeline_mode=pl.Buffered(1))
    return pl.pallas_call(
        functools.partial(_mlp_kernel, final=final, ff_chunk=MLP_FF_CHUNK),
        out_shape=jax.ShapeDtypeStruct((n, d), F32),
        grid=(nt,),
        in_specs=[row(d), row(ATTN_WIDTH), row(GLA_WIDTH), row(LRU_WIDTH),
                  pl.BlockSpec((1, 1, 1, 6 * d), mod_map),
                  _const_spec((1, d)), _const_spec((1, d)),
                  resident((d, d)), resident((d, D_FF)), resident((D_FF, d))],
        out_specs=row(d),
        compiler_params=_params(1),
        name="out_mlp",
    )(x2d, att, gla, lru, mod, norm2.reshape(1, d), final_norm.reshape(1, d), wo, w1, w2)


def _prep_w_in(w_in):
    wz = w_in[..., C_A_END:C_B].astype(BF16)
    pad = jnp.zeros(w_in.shape[:-1] + (GZ_PAD - (C_B - C_A_END),), BF16)
    return (w_in[..., :C_A_END].astype(BF16), jnp.concatenate([wz, pad], axis=-1),
            w_in[..., C_B:].astype(BF16))


def _prep_gate_w(gw):
    out = []
    nz = 2 * GLA_GATE_RANK
    for dr in (0, 1):
        wf = jnp.zeros((nz, GLA_WIDTH), F32)
        wf = wf.at[dr * GLA_GATE_RANK:(dr + 1) * GLA_GATE_RANK].set(gw[dr])
        hi, lo = _split_bf16(wf)
        out.append(jnp.concatenate([hi, hi, lo, jnp.zeros((GZ_PAD - 3 * nz, GLA_WIDTH), BF16)],
                                   axis=0))
    return jnp.stack(out)


def _block_diag(w4):
    n, a, b = w4.shape
    eye = jnp.eye(n, dtype=w4.dtype)
    return jnp.einsum("nab,nm->namb", w4, eye).reshape(n * a, n * b)


def _rope_tables(seq):
    rows = seq // GRID_W
    nf = HEAD_DIM // 4
    inv = ROPE_BASE ** (-jnp.arange(nf, dtype=F32) / nf)
    ang_r = jnp.arange(rows, dtype=F32)[:, None] * inv
    ang_c = jnp.arange(GRID_W, dtype=F32)[:, None] * inv
    by_row = lambda t: jnp.repeat(t, GRID_W, axis=0)
    by_col = lambda t: jnp.tile(t, (rows, 1))
    cos_r, sin_r = by_row(jnp.cos(ang_r)), by_row(jnp.sin(ang_r))
    cos_c, sin_c = by_col(jnp.cos(ang_c)), by_col(jnp.sin(ang_c))
    cos = jnp.concatenate([cos_r, cos_r, cos_c, cos_c], axis=1)
    sin = jnp.concatenate([-sin_r, sin_r, -sin_c, sin_c], axis=1)
    return jnp.concatenate([cos] * 2, axis=1), jnp.concatenate([sin] * 2, axis=1)


def _tiles(latent, seq):
    if latent:
        return 1024, 4 * ATTN_SUB, 1024, 2048, 1024
    return 1024, 4 * seq, 4 * seq, 4 * seq, 1024


def _gla_state_to_internal(s):
    b = s.shape[0]
    return jnp.transpose(s, (0, 1, 4, 2, 3)).reshape(b, 2, GLA_DK, GLA_WIDTH)


def kernel(x_prompt, x_sample, c, cache_k, cache_v, state_gla, state_lru, c_ctx, w_mod, b_mod, norm1, norm2, w_in, attn_sink, gla_gate_w, gla_gate_b, gla_norm, lru_conv_w, lru_conv_b, lru_wa, lru_ba, lru_wx, lru_bx, lru_lambda, w_out, w_mlp1, w_mlp2, final_norm):
    depth = w_in.shape[0]
    bc, sc_len, d = x_prompt.shape
    bl, sl_len, _ = x_sample.shape
    past = cache_k.shape[2]

    cond = jnp.concatenate([c_ctx[None], c, jnp.zeros((MOD_ROWS - 1 - bl, d), F32)], axis=0)
    mod = _ada_mod(cond, w_mod, b_mod)

    rope_tabs = _rope_tables(sl_len)
    ck = cache_k.reshape(bl, depth, past, KV_WIDTH)
    cv = cache_v.reshape(bl, depth, past, KV_WIDTH)
    gla_s0_ctx = jnp.zeros((1, 2, GLA_DK, GLA_WIDTH), F32)
    lru_s0_ctx = jnp.zeros((1, 2, 1, LRU_WIDTH), F32)

    xp = x_prompt.reshape(bc * sc_len, d)
    xs = x_sample.reshape(bl * sl_len, d)
    w_in_b = _prep_w_in(w_in)
    wo = w_out.astype(BF16)
    w1 = w_mlp1.astype(BF16)
    w2 = w_mlp2.astype(BF16)
    ks, vs, sgs, sls = [], [], [], []
    for l in range(depth):
        gate_w = _prep_gate_w(gla_gate_w[l])
        gate_b = gla_gate_b[l].reshape(2, 1, GLA_WIDTH)
        lru_wg = (0.5 * jnp.stack([jnp.concatenate(
            [_block_diag(lru_wa[l, dr]), _block_diag(lru_wx[l, dr])], axis=1)
            for dr in (0, 1)])).astype(BF16)
        lru_bg = (0.5 * jnp.stack([jnp.concatenate([lru_ba[l, dr], lru_bx[l, dr]])
                                   for dr in (0, 1)])).reshape(2, 1, 2 * LRU_WIDTH)
        lam = lru_lambda[l].reshape(2, 1, LRU_WIDTH)
        final = l == depth - 1

        for is_lat in (False, True):
            if is_lat:
                x2d, batch, seq, row0, rpm, tabs = xs, bl, sl_len, 1, sl_len, rope_tabs
                gla_s0 = _gla_state_to_internal(state_gla[:, l])
                lru_s0 = state_lru[:, l].reshape(bl, 2, 1, LRU_WIDTH)
            else:
                x2d, batch, seq, row0, rpm, tabs = xp, bc, sc_len, 0, bc * sc_len, None
                gla_s0, lru_s0 = gla_s0_ctx, lru_s0_ctx
            tt_proj, tq, tt_gla, tt_lru, tt_mlp = _tiles(is_lat, seq)
            q, k, v, gin, lx, lg = _in_proj(x2d, mod, norm1[l], w_in_b, l, tabs, rpm, row0,
                                            tt_proj)
            att = _attention(q, k, v, attn_sink[l], batch, seq, tq, ATTN_SUB,
                             ctx_kv=(ck, cv, l) if is_lat else None)
            gla, sg = _gla(gin, gate_w, gate_b, gla_norm[l], gla_s0, batch, seq, tt_gla)
            lru, sl = _lru(lx, lg, lru_conv_w[l], lru_conv_b[l], lru_wg, lru_bg, lam, lru_s0,
                           batch, seq, tt_lru)
            x2d = _out_mlp(x2d, att, gla, lru, mod, norm2[l], final_norm, wo, w1, w2, l,
                           rpm, row0, tt_mlp, final)
            if is_lat:
                xs = x2d
            else:
                xp = x2d
                ks.append(k.reshape(bc, sc_len, KV_WIDTH))
                vs.append(v.reshape(bc, sc_len, KV_WIDTH))
                sgs.append(sg)
                sls.append(sl[:, :, 0, :])

    y_prompt = xp.reshape(bc, sc_len, d)
    y_sample = xs.reshape(bl, sl_len, d)
    kv_shape = (bc, depth, sc_len, KV_WIDTH // HEAD_DIM, HEAD_DIM)
    new_k = jnp.stack(ks, axis=1).reshape(kv_shape)
    new_v = jnp.stack(vs, axis=1).reshape(kv_shape)
    sg_all = jnp.stack(sgs, axis=1)
    new_sg = jnp.transpose(
        sg_all.reshape(bc, depth, 2, GLA_DK, GLA_HEADS, GLA_DK), (0, 1, 2, 4, 5, 3))
    new_sl = jnp.stack(sls, axis=1)
    return (y_prompt, y_sample, new_k, new_v, new_sg, new_sl)
```
